```python
import jax, jax.numpy as jnp
from jax import lax
import numpy as np

D_MODEL = 1024
BATCH = 2
SEQ = 8192
DEPTH = 1
DEC_BATCH = 32
DEC_SEQ = 1
PAST_LEN = 8192
PAGE_SIZE = 128

HEAD_DIM = 64
HEADS_PER_GROUP = 8
WINDOWS = (128, 512, 2048)
DILATIONS = (1, 4, 16)
N_GROUPS = 3
GROUP_WIDTH = HEADS_PER_GROUP * HEAD_DIM
ATTN_WIDTH = N_GROUPS * GROUP_WIDTH
D_CONV = D_MODEL
CONV_WIDTH = 3
D_FF = 4 * D_MODEL
Q_BLOCK = 128
EPS = 1e-6
NEG = -1e30
IN_WIDTH = 3 * ATTN_WIDTH + 3 * D_CONV + 2 * D_MODEL
SPLIT_POINTS = (ATTN_WIDTH, 2 * ATTN_WIDTH, 3 * ATTN_WIDTH,
                3 * ATTN_WIDTH + D_CONV, 3 * ATTN_WIDTH + 2 * D_CONV,
                3 * ATTN_WIDTH + 3 * D_CONV, 3 * ATTN_WIDTH + 3 * D_CONV + D_MODEL)

kernel_name = "hybrid_dilated_attn_shortconv_decoder_step"


def _rms_norm(x, g):
    xf = x.astype(jnp.float32)
    y = xf * lax.rsqrt(jnp.mean(xf * xf, axis=-1, keepdims=True) + EPS)
    return (y * g.astype(jnp.float32)).astype(x.dtype)


def _project(h, w_in):
    b, t, _ = h.shape
    p = jnp.einsum('btd,de->bte', h, w_in)
    q, k, v, c_b, c_c, c_h, g_a, g_c = jnp.split(p, SPLIT_POINTS, axis=-1)
    heads = (b, t, N_GROUPS, HEADS_PER_GROUP, HEAD_DIM)
    return q.reshape(heads), k.reshape(heads), v.reshape(heads), c_b, c_c * c_h, g_a, g_c


def _band_attention(q, k, v, band):
    n, mlen, h, d = q.shape
    nb = -(-mlen // Q_BLOCK)
    pad = nb * Q_BLOCK - mlen
    qb = jnp.pad(q, ((0, 0), (0, pad), (0, 0), (0, 0))).reshape(n, nb, Q_BLOCK, h, d)

    def key_blocks(t):
        tp = jnp.pad(t, ((0, 0), (Q_BLOCK, pad), (0, 0), (0, 0))).reshape(n, nb + 1, Q_BLOCK, h, d)
        return jnp.concatenate([tp[:, :-1], tp[:, 1:]], axis=2)

    kb, vb = key_blocks(k), key_blocks(v)
    s = jnp.einsum('nbqhd,nbkhd->nbhqk', qb, kb, preferred_element_type=jnp.float32) * (d ** -0.5)
    qi = jnp.arange(Q_BLOCK)[:, None]
    ki = jnp.arange(2 * Q_BLOCK)[None, :]
    rel = qi + Q_BLOCK - ki
    kpos = jnp.arange(nb)[:, None, None] * Q_BLOCK - Q_BLOCK + ki
    valid = (rel >= 0) & (rel <= band) & (kpos >= 0)
    s = jnp.where(valid[None, :, None], s, NEG)
    m = jnp.max(s, axis=-1)
    p = jnp.exp(s - m[..., None])
    den = jnp.sum(p, axis=-1)
    o = jnp.einsum('nbhqk,nbkhd->nbqhd', p, vb.astype(jnp.float32))
    o = o / jnp.swapaxes(den, -1, -2)[..., None]
    lse = jnp.swapaxes(m + jnp.log(den), -1, -2)
    o = o.reshape(n, nb * Q_BLOCK, h, d)[:, :mlen]
    lse = lse.reshape(n, nb * Q_BLOCK, h)[:, :mlen]
    return o, lse


def _dilated_prompt(q, k, v, dil, band):
    b, s, h, d = q.shape

    def split(t):
        return t.reshape(b, s // dil, dil, h, d).transpose(0, 2, 1, 3, 4).reshape(b * dil, s // dil, h, d)

    o, lse = _band_attention(split(q), split(k), split(v), band)
    o = o.reshape(b, dil, s // dil, h, d).transpose(0, 2, 1, 3, 4).reshape(b, s, h, d)
    lse = lse.reshape(b, dil, s // dil, h).transpose(0, 2, 1, 3).reshape(b, s, h)
    return o, lse


def _dilated_sample(q, kcat, vcat, n_past, dil, band):
    t = q.shape[1]
    idx = n_past + jnp.arange(t)[:, None] - dil * jnp.arange(band + 1)[None, :]
    valid = idx >= 0
    idx = jnp.maximum(idx, 0)
    kg = kcat[:, idx]
    vg = vcat[:, idx]
    s = jnp.einsum('bthd,btjhd->bthj', q, kg, preferred_element_type=jnp.float32) * (q.shape[-1] ** -0.5)
    s = jnp.where(valid[None, :, None, :], s, NEG)
    m = jnp.max(s, axis=-1)
    p = jnp.exp(s - m[..., None])
    den = jnp.sum(p, axis=-1)
    o = jnp.einsum('bthj,btjhd->bthd', p, vg.astype(jnp.float32)) / den[..., None]
    return o, m + jnp.log(den)


def _merge_groups(outs, lses):
    o = jnp.stack(outs, axis=0)
    alpha = jax.nn.softmax(jnp.stack(lses, axis=0), axis=0)
    a = jnp.sum(alpha[..., None] * o, axis=0)
    b, t = a.shape[:2]
    return a.reshape(b, t, GROUP_WIDTH)


def _causal_conv(u_ext, w):
    t = u_ext.shape[1] - (CONV_WIDTH - 1)
    y = w[0] * u_ext[:, 0:t]
    for j in range(1, CONV_WIDTH):
        y = y + w[j] * u_ext[:, j:j + t]
    return y


def _block_output(x, attn, conv_v, c_b, g_a, g_c, w_attn_o, w_conv_o, w_o,
                  w_ff1, w_ff2, g_mix_post, g_ffn_pre, g_ffn_post):
    a = attn.astype(x.dtype) @ w_attn_o
    c = (c_b * conv_v) @ w_conv_o
    mix = (jax.nn.sigmoid(g_a) * a + jax.nn.sigmoid(g_c) * c) @ w_o
    x = x + _rms_norm(mix, g_mix_post)
    h = _rms_norm(x, g_ffn_pre)
    f = jnp.square(jax.nn.relu(h @ w_ff1)) @ w_ff2
    return x + _rms_norm(f, g_ffn_post)


def setup_inputs(seed: int = 0) -> dict:
    key = jax.random.key(seed)
    ks = jax.random.split(key, 20)
    f32 = jnp.float32
    nrm = lambda k, shape, scale: jax.random.normal(k, shape, f32) * scale
    cache_shape = lambda w: (DEPTH, DEC_BATCH, min(w, PAST_LEN), 2, HEADS_PER_GROUP, HEAD_DIM)
    return {
        "x_prompt": nrm(ks[0], (BATCH, SEQ, D_MODEL), 1.0),
        "x_sample": nrm(ks[1], (DEC_BATCH, DEC_SEQ, D_MODEL), 1.0),
        "cache_kv_w128": nrm(ks[2], cache_shape(WINDOWS[0]), 1.0),
        "cache_kv_w512": nrm(ks[3], cache_shape(WINDOWS[1]), 1.0),
        "cache_kv_w2048": nrm(ks[4], cache_shape(WINDOWS[2]), 1.0),
        "state_conv": nrm(ks[5], (DEPTH, DEC_BATCH, CONV_WIDTH - 1, D_CONV), 1.0),
        "w_in": nrm(ks[6], (DEPTH, D_MODEL, IN_WIDTH), D_MODEL ** -0.5),
        "conv_w": nrm(ks[7], (DEPTH, CONV_WIDTH, D_CONV), CONV_WIDTH ** -0.5),
        "w_attn_o": nrm(ks[8], (DEPTH, GROUP_WIDTH, D_MODEL), GROUP_WIDTH ** -0.5),
        "w_conv_o": nrm(ks[9], (DEPTH, D_CONV, D_MODEL), D_CONV ** -0.5),
        "w_o": nrm(ks[10], (DEPTH, D_MODEL, D_MODEL), D_MODEL ** -0.5),
        "w_ff1": nrm(ks[11], (DEPTH, D_MODEL, D_FF), D_MODEL ** -0.5),
        "w_ff2": nrm(ks[12], (DEPTH, D_FF, D_MODEL), D_FF ** -0.5),
        "g_mix_pre": 1.0 + nrm(ks[13], (DEPTH, D_MODEL), 0.02),
        "g_mix_post": 1.0 + nrm(ks[14], (DEPTH, D_MODEL), 0.02),
        "g_ffn_pre": 1.0 + nrm(ks[15], (DEPTH, D_MODEL), 0.02),
        "g_ffn_post": 1.0 + nrm(ks[16], (DEPTH, D_MODEL), 0.02),
    }


def reference(x_prompt, x_sample, cache_kv_w128, cache_kv_w512, cache_kv_w2048, state_conv,
              w_in, conv_w, w_attn_o, w_conv_o, w_o, w_ff1, w_ff2,
              g_mix_pre, g_mix_post, g_ffn_pre, g_ffn_post):
    caches = (cache_kv_w128, cache_kv_w512, cache_kv_w2048)
    yp, ys = x_prompt, x_sample
    kv_p = [[] for _ in range(N_GROUPS)]
    kv_s = [[] for _ in range(N_GROUPS)]
    conv_p, conv_s = [], []
    for l in range(DEPTH):
        h = _rms_norm(yp, g_mix_pre[l])
        q, k, v, c_b, u, g_a, g_c = _project(h, w_in[l])
        outs, lses = [], []
        for g in range(N_GROUPS):
            o, lse = _dilated_prompt(q[:, :, g], k[:, :, g], v[:, :, g],
                                     DILATIONS[g], WINDOWS[g] // DILATIONS[g])
            outs.append(o)
            lses.append(lse)
            kv = jnp.stack([k[:, :, g], v[:, :, g]], axis=2)
            kv_p[g].append(kv[:, -min(WINDOWS[g], kv.shape[1]):])
        u_ext = jnp.pad(u, ((0, 0), (CONV_WIDTH - 1, 0), (0, 0)))
        conv_v = _causal_conv(u_ext, conv_w[l])
        conv_p.append(u_ext[:, -(CONV_WIDTH - 1):])
        yp = _block_output(yp, _merge_groups(outs, lses), conv_v, c_b, g_a, g_c,
                           w_attn_o[l], w_conv_o[l], w_o[l], w_ff1[l], w_ff2[l],
                           g_mix_post[l], g_ffn_pre[l], g_ffn_post[l])

        h = _rms_norm(ys, g_mix_pre[l])
        q, k, v, c_b, u, g_a, g_c = _project(h, w_in[l])
        outs, lses = [], []
        for g in range(N_GROUPS):
            buf = caches[g][l]
            n_past = buf.shape[1]
            cat = jnp.concatenate([buf, jnp.stack([k[:, :, g], v[:, :, g]], axis=2).astype(buf.dtype)], axis=1)
            o, lse = _dilated_sample(q[:, :, g], cat[:, :, 0], cat[:, :, 1], n_past,
                                     DILATIONS[g], WINDOWS[g] // DILATIONS[g])
            outs.append(o)
            lses.append(lse)
            kv_s[g].append(cat[:, -n_past:])
        u_ext = jnp.concatenate([state_conv[l].astype(u.dtype), u], axis=1)
        conv_v = _causal_conv(u_ext, conv_w[l])
        conv_s.append(u_ext[:, -(CONV_WIDTH - 1):])
        ys = _block_output(ys, _merge_groups(outs, lses), conv_v, c_b, g_a, g_c,
                           w_attn_o[l], w_conv_o[l], w_o[l], w_ff1[l], w_ff2[l],
                           g_mix_post[l], g_ffn_pre[l], g_ffn_post[l])

    kv_w128_prompt = jnp.stack(kv_p[0], axis=0)
    kv_w512_prompt = jnp.stack(kv_p[1], axis=0)
    kv_w2048_prompt = jnp.stack(kv_p[2], axis=0)
    conv_prompt = jnp.stack(conv_p, axis=0)
    kv_w128_sample = jnp.stack(kv_s[0], axis=0)
    kv_w512_sample = jnp.stack(kv_s[1], axis=0)
    kv_w2048_sample = jnp.stack(kv_s[2], axis=0)
    conv_sample = jnp.stack(conv_s, axis=0)
    return (yp, ys, kv_w128_prompt, kv_w512_prompt, kv_w2048_prompt, conv_prompt,
            kv_w128_sample, kv_w512_sample, kv_w2048_sample, conv_sample)
```

```python
import functools

import jax
import jax.numpy as jnp
from jax import lax
from jax.experimental import pallas as pl
from jax.experimental.pallas import tpu as pltpu

HEAD_DIM = 64
HEADS = 8
GROUP_WIDTH = HEADS * HEAD_DIM
N_GROUPS = 3
WINDOWS = (128, 512, 2048)
DILATIONS = (1, 4, 16)
BAND = 128
CONV_WIDTH = 3
EPS = 1e-6
NEG = -1e30
LANES = 128
SUBLANES = 8
VMEM_LIMIT = 56 * 1024 * 1024

F32 = jnp.float32
BF16 = jnp.bfloat16


def _rms(x, g):
    return (x * lax.rsqrt(jnp.mean(x * x, axis=-1, keepdims=True) + EPS)) * g


def _resident(shape):
    nd = len(shape)
    return pl.BlockSpec(shape, lambda *_: (0,) * nd, pipeline_mode=pl.Buffered(1))


def _proj_kernel(x_ref, g_ref, w_ref, qkv_ref, rest_ref, *, tn):
    h = _rms(x_ref[...], g_ref[...]).astype(BF16)
    nq = qkv_ref.shape[-1]
    n = w_ref.shape[-1]
    for j in range(n // tn):
        c0 = j * tn
        r = jnp.dot(h, w_ref[:, c0:c0 + tn], preferred_element_type=F32)
        if c0 < nq:
            qkv_ref[:, c0:c0 + tn] = r.astype(qkv_ref.dtype)
        else:
            rest_ref[:, c0 - nq:c0 - nq + tn] = r.astype(rest_ref.dtype)


def _project(x2d, g, w_bf16, *, tm, out_dtype):
    m, d = x2d.shape
    n = w_bf16.shape[1]
    nq = 3 * N_GROUPS * GROUP_WIDTH
    tn = GROUP_WIDTH
    return pl.pallas_call(
        functools.partial(_proj_kernel, tn=tn),
        out_shape=(jax.ShapeDtypeStruct((m, nq), out_dtype),
                   jax.ShapeDtypeStruct((m, n - nq), out_dtype)),
        grid=(m // tm,),
        in_specs=[pl.BlockSpec((tm, d), lambda i: (i, 0)),
                  _resident((1, d)),
                  _resident((d, n))],
        out_specs=(pl.BlockSpec((tm, nq), lambda i: (i, 0)),
                   pl.BlockSpec((tm, n - nq), lambda i: (i, 0))),
        compiler_params=pltpu.CompilerParams(dimension_semantics=("arbitrary",),
                                             vmem_limit_bytes=VMEM_LIMIT),
        name="proj",
    )(x2d, g, w_bf16)


def _attn_kernel(q_ref, kc_ref, vc_ref, kp_ref, vp_ref, o_ref, l_ref, kcat, vcat, *, tq):
    i = pl.program_id(2)
    kcat[0:BAND, :] = kp_ref[...]
    kcat[BAND:, :] = kc_ref[...]
    vcat[0:BAND, :] = vp_ref[...]
    vcat[BAND:, :] = vc_ref[...]

    qi = lax.broadcasted_iota(jnp.int32, (BAND, 2 * BAND), 0)
    ki = lax.broadcasted_iota(jnp.int32, (BAND, 2 * BAND), 1)
    rel = qi + BAND - ki
    band_ok = (rel >= 0) & (rel <= BAND)
    lane = lax.broadcasted_iota(jnp.int32, (BAND, LANES), 1)
    low = lane < HEAD_DIM

    for qb in range(tq // BAND):
        r0 = qb * BAND
        if qb == 0:
            valid = band_ok & ((ki >= BAND) | (i > 0))
        else:
            valid = band_ok
        for hp in range(GROUP_WIDTH // LANES):
            c0 = hp * LANES
            q2 = q_ref[r0:r0 + BAND, c0:c0 + LANES]
            k2 = kcat[r0:r0 + 2 * BAND, c0:c0 + LANES]
            v2 = vcat[r0:r0 + 2 * BAND, c0:c0 + LANES]
            outs, lses = [], []
            for half in range(2):
                sel = low if half == 0 else ~low
                qm = jnp.where(sel, q2, jnp.zeros_like(q2))
                s = lax.dot_general(qm, k2, (((1,), (1,)), ((), ())),
                                    preferred_element_type=F32) * (HEAD_DIM ** -0.5)
                s = jnp.where(valid, s, NEG)
                m = jnp.max(s, axis=-1, keepdims=True)
                p = jnp.exp(s - m)
                den = jnp.sum(p, axis=-1, keepdims=True)
                pv = jnp.dot(p.astype(BF16), v2, preferred_element_type=F32)
                outs.append(pv / den)
                lses.append(m + jnp.log(den))
            o_ref[r0:r0 + BAND, c0:c0 + LANES] = jnp.where(low, outs[0], outs[1]).astype(o_ref.dtype)
            l_ref[r0:r0 + BAND, c0:c0 + LANES] = jnp.where(low, lses[0], lses[1])


def _prompt_attention(qkv3, g, *, tq):
    b, s, n = qkv3.shape
    r = DILATIONS[g]
    sr = s // r
    tq = min(tq, sr)
    nblk = n // GROUP_WIDTH
    view = qkv3.reshape(b, sr, r * n)
    per = tq // BAND

    def cur(off):
        return pl.BlockSpec((None, tq, GROUP_WIDTH), lambda bb, c, i: (bb, i, c * nblk + off))

    def prev(off):
        return pl.BlockSpec((None, BAND, GROUP_WIDTH),
                            lambda bb, c, i: (bb, jnp.maximum(i * per - 1, 0), c * nblk + off))

    out_spec = pl.BlockSpec((None, tq, GROUP_WIDTH), lambda bb, c, i: (bb, i, c))
    o, l = pl.pallas_call(
        functools.partial(_attn_kernel, tq=tq),
        out_shape=(jax.ShapeDtypeStruct((b, sr, r * GROUP_WIDTH), F32),
                   jax.ShapeDtypeStruct((b, sr, r * GROUP_WIDTH), F32)),
        grid=(b, r, sr // tq),
        in_specs=[cur(g), cur(N_GROUPS + g), cur(2 * N_GROUPS + g),
                  prev(N_GROUPS + g), prev(2 * N_GROUPS + g)],
        out_specs=(out_spec, out_spec),
        scratch_shapes=[pltpu.VMEM((BAND + tq, GROUP_WIDTH), BF16),
                        pltpu.VMEM((BAND + tq, GROUP_WIDTH), BF16)],
        compiler_params=pltpu.CompilerParams(
            dimension_semantics=("arbitrary", "arbitrary", "arbitrary"),
            vmem_limit_bytes=VMEM_LIMIT),
        name=f"attn_g{g}",
    )(view, view, view, view, view)
    return o.reshape(b, s, GROUP_WIDTH), l.reshape(b, s, GROUP_WIDTH)


def _merge_groups(o_refs, l_refs):
    ls = [r[...] for r in l_refs]
    m = jnp.maximum(jnp.maximum(ls[0], ls[1]), ls[2])
    es = [jnp.exp(l - m) for l in ls]
    num = es[0] * o_refs[0][...] + es[1] * o_refs[1][...] + es[2] * o_refs[2][...]
    return num / (es[0] + es[1] + es[2])


def _block_core(x, attn, c_b, conv_v, g_a, g_c, wao_ref, wco_ref, wo_ref, w1_ref, w2_ref,
                gpost, gpre, gfpost, *, ff_chunk):
    a = jnp.dot(attn.astype(BF16), wao_ref[...], preferred_element_type=F32)
    c = jnp.dot((c_b * conv_v).astype(BF16), wco_ref[...], preferred_element_type=F32)
    mix_in = jax.nn.sigmoid(g_a) * a + jax.nn.sigmoid(g_c) * c
    mix = jnp.dot(mix_in.astype(BF16), wo_ref[...], preferred_element_type=F32)
    x1 = x + _rms(mix, gpost)
    h2 = _rms(x1, gpre).astype(BF16)
    f = None
    for k0 in range(0, w1_ref.shape[1], ff_chunk):
        t = jnp.dot(h2, w1_ref[:, k0:k0 + ff_chunk], preferred_element_type=F32)
        t = jnp.square(jnp.maximum(t, 0.0)).astype(BF16)
        part = jnp.dot(t, w2_ref[k0:k0 + ff_chunk, :], preferred_element_type=F32)
        f = part if f is None else f + part
    return x1 + _rms(f, gfpost)


def _block_prompt_kernel(x_ref, rest_ref, hcc_ref, hch_ref, o0, l0, o1, l1, o2, l2,
                         convw_ref, wao_ref, wco_ref, wo_ref, w1_ref, w2_ref,
                         gpost_ref, gpre_ref, gfpost_ref, y_ref, tail_ref, u_scr, *, ff_chunk):
    i = pl.program_id(1)
    tm, d = x_ref.shape
    attn = _merge_groups((o0, o1, o2), (l0, l1, l2))
    c_b = rest_ref[:, 0:d].astype(F32)
    u = rest_ref[:, d:2 * d].astype(F32) * rest_ref[:, 2 * d:3 * d].astype(F32)
    g_a = rest_ref[:, 3 * d:4 * d].astype(F32)
    g_c = rest_ref[:, 4 * d:5 * d].astype(F32)
    hu = hcc_ref[...].astype(F32) * hch_ref[...].astype(F32)
    u_scr[0:SUBLANES, :] = jnp.where(i > 0, hu, jnp.zeros_like(hu))
    u_scr[SUBLANES:, :] = u
    w = convw_ref[...]
    conv_v = (w[0:1, :] * u_scr[SUBLANES - 2:SUBLANES - 2 + tm, :]
              + w[1:2, :] * u_scr[SUBLANES - 1:SUBLANES - 1 + tm, :]
              + w[2:3, :] * u)
    y_ref[...] = _block_core(x_ref[...], attn, c_b, conv_v, g_a, g_c,
                             wao_ref, wco_ref, wo_ref, w1_ref, w2_ref,
                             gpost_ref[...], gpre_ref[...], gfpost_ref[...], ff_chunk=ff_chunk)
    tail_ref[...] = u[tm - SUBLANES:tm, :]


def _block_sample_kernel(x_ref, rest_ref, st_ref, o0, l0, o1, l1, o2, l2,
                         convw_ref, wao_ref, wco_ref, wo_ref, w1_ref, w2_ref,
                         gpost_ref, gpre_ref, gfpost_ref, y_ref, u_ref, *, ff_chunk):
    d = x_ref.shape[1]
    attn = _merge_groups((o0, o1, o2), (l0, l1, l2))
    c_b = rest_ref[:, 0:d].astype(F32)
    u = rest_ref[:, d:2 * d].astype(F32) * rest_ref[:, 2 * d:3 * d].astype(F32)
    g_a = rest_ref[:, 3 * d:4 * d].astype(F32)
    g_c = rest_ref[:, 4 * d:5 * d].astype(F32)
    w = convw_ref[...]
    conv_v = w[0:1, :] * st_ref[:, 0:d] + w[1:2, :] * st_ref[:, d:2 * d] + w[2:3, :] * u
    y_ref[...] = _block_core(x_ref[...], attn, c_b, conv_v, g_a, g_c,
                             wao_ref, wco_ref, wo_ref, w1_ref, w2_ref,
                             gpost_ref[...], gpre_ref[...], gfpost_ref[...], ff_chunk=ff_chunk)
    u_ref[...] = u


def _weight_specs(ws):
    return [_resident(w.shape) for w in ws]


def _block_prompt(x3, rest3, os_, ls_, weights, *, tm, ff_chunk):
    b, s, d = x3.shape
    nr = rest3.shape[-1]
    per = tm // SUBLANES
    row = lambda width: pl.BlockSpec((None, tm, width), lambda bb, i: (bb, i, 0))
    halo = lambda col: pl.BlockSpec((None, SUBLANES, d),
                                    lambda bb, i: (bb, jnp.maximum(i * per - 1, 0), col))
    in_specs = [row(d), row(nr), halo(1), halo(2)]
    args = [x3, rest3, rest3, rest3]
    for o, l in zip(os_, ls_):
        in_specs += [row(GROUP_WIDTH), row(GROUP_WIDTH)]
        args += [o, l]
    in_specs += _weight_specs(weights)
    args += list(weights)
    y, tail = pl.pallas_call(
        functools.partial(_block_prompt_kernel, ff_chunk=ff_chunk),
        out_shape=(jax.ShapeDtypeStruct((b, s, d), F32),
                   jax.ShapeDtypeStruct((b, SUBLANES, d), F32)),
        grid=(b, s // tm),
        in_specs=in_specs,
        out_specs=(row(d), pl.BlockSpec((None, SUBLANES, d), lambda bb, i: (bb, 0, 0))),
        scratch_shapes=[pltpu.VMEM((SUBLANES + tm, d), F32)],
        compiler_params=pltpu.CompilerParams(dimension_semantics=("arbitrary", "arbitrary"),
                                             vmem_limit_bytes=VMEM_LIMIT),
        name="block_prompt",
    )(*args)
    return y, tail


def _block_sample(x2, rest2, state2, os_, ls_, weights, *, ff_chunk):
    m, d = x2.shape
    full = lambda a: pl.BlockSpec(a.shape, lambda i: (0,) * a.ndim)
    args = [x2, rest2, state2]
    for o, l in zip(os_, ls_):
        args += [o, l]
    in_specs = [full(a) for a in args] + _weight_specs(weights)
    args += list(weights)
    return pl.pallas_call(
        functools.partial(_block_sample_kernel, ff_chunk=ff_chunk),
        out_shape=(jax.ShapeDtypeStruct((m, d), F32), jax.ShapeDtypeStruct((m, d), F32)),
        grid=(1,),
        in_specs=in_specs,
        out_specs=(pl.BlockSpec((m, d), lambda i: (0, 0)), pl.BlockSpec((m, d), lambda i: (0, 0))),
        compiler_params=pltpu.CompilerParams(dimension_semantics=("arbitrary",),
                                             vmem_limit_bytes=VMEM_LIMIT),
        name="block_sample",
    )(*args)


def _sample_cache_kernel(cols_ref, cache_ref, out_ref, o_ref, l_ref, *, dil):
    b = pl.program_id(0)
    length = cache_ref.shape[-1]
    gw = GROUP_WIDTH
    scale = HEAD_DIM ** -0.5

    @pl.when(b == 0)
    def _():
        o_ref[...] = jnp.zeros_like(o_ref)
        l_ref[...] = jnp.zeros_like(l_ref)

    lane = lax.broadcasted_iota(jnp.int32, (1, LANES), 1)
    mine = lane == b
    cols = jnp.sum(jnp.where(mine, cols_ref[...], 0.0), axis=-1, keepdims=True)
    qc, knc, vnc = cols[0:gw], cols[gw:2 * gw], cols[2 * gw:3 * gw]

    row = lax.broadcasted_iota(jnp.int32, (1, length), 1)
    dist = length - row
    on_grid = ((dist % dil) == 0) & (dist <= BAND * dil)
    for h in range(HEADS):
        rows = slice(h * HEAD_DIM, (h + 1) * HEAD_DIM)
        s = jnp.sum(cache_ref[0, rows, :] * qc[rows], axis=0, keepdims=True) * scale
        s = jnp.where(on_grid, s, NEG)
        s_new = jnp.sum(qc[rows] * knc[rows], axis=0, keepdims=True) * scale
        m = jnp.maximum(jnp.max(s, axis=-1, keepdims=True), s_new)
        p = jnp.exp(s - m)
        p_new = jnp.exp(s_new - m)
        den = jnp.sum(p, axis=-1, keepdims=True) + p_new
        acc = jnp.sum(cache_ref[1, rows, :] * p, axis=-1, keepdims=True) + p_new * vnc[rows]
        o_ref[rows, :] = jnp.where(mine, acc / den, o_ref[rows, :])
        l_ref[rows, :] = jnp.where(mine, m + jnp.log(den), l_ref[rows, :])

    last = row == length - 1
    for kv, newc in ((0, knc), (1, vnc)):
        rolled = pltpu.roll(cache_ref[kv], length - 1, axis=1)
        out_ref[kv] = jnp.where(last, newc, rolled)


def _sample_cache(cols, cache_t, g):
    db, _, gw, length = cache_t.shape
    blk = pl.BlockSpec((None, 2, gw, length), lambda i: (i, 0, 0, 0))
    acc = pl.BlockSpec((gw, LANES), lambda i: (0, 0))
    return pl.pallas_call(
        functools.partial(_sample_cache_kernel, dil=DILATIONS[g]),
        out_shape=(jax.ShapeDtypeStruct(cache_t.shape, cache_t.dtype),
                   jax.ShapeDtypeStruct((gw, LANES), F32),
                   jax.ShapeDtypeStruct((gw, LANES), F32)),
        grid=(db,),
        in_specs=[_resident(cols.shape), blk],
        out_specs=(blk, acc, acc),
        compiler_params=pltpu.CompilerParams(dimension_semantics=("arbitrary",),
                                             vmem_limit_bytes=VMEM_LIMIT),
        name=f"sample_cache_g{g}",
    )(cols, cache_t)


def kernel(x_prompt, x_sample, cache_kv_w128, cache_kv_w512, cache_kv_w2048, state_conv, w_in, conv_w, w_attn_o, w_conv_o, w_o, w_ff1, w_ff2, g_mix_pre, g_mix_post, g_ffn_pre, g_ffn_post):
    depth = w_in.shape[0]
    b, s, d = x_prompt.shape
    db, ds_, _ = x_sample.shape
    assert ds_ == 1 and d % LANES == 0 and db <= LANES
    caches = (cache_kv_w128, cache_kv_w512, cache_kv_w2048)
    gw = GROUP_WIDTH

    yp, ys = x_prompt, x_sample.reshape(db, d)
    kv_p = [[] for _ in range(N_GROUPS)]
    kv_s = [[] for _ in range(N_GROUPS)]
    conv_p, conv_s = [], []
    for l in range(depth):
        w_in_b = w_in[l].astype(BF16)
        weights = (conv_w[l], w_attn_o[l].astype(BF16), w_conv_o[l].astype(BF16),
                   w_o[l].astype(BF16), w_ff1[l].astype(BF16), w_ff2[l].astype(BF16),
                   g_mix_post[l][None, :], g_ffn_pre[l][None, :], g_ffn_post[l][None, :])
        g_pre = g_mix_pre[l][None, :]

        qkv, rest = _project(yp.reshape(b * s, d), g_pre, w_in_b, tm=512, out_dtype=BF16)
        qkv3 = qkv.reshape(b, s, -1)
        rest3 = rest.reshape(b, s, -1)
        os_, ls_ = [], []
        for g in range(N_GROUPS):
            o, lse = _prompt_attention(qkv3, g, tq=512)
            os_.append(o)
            ls_.append(lse)
            w = min(WINDOWS[g], s)
            k_tail = qkv3[:, s - w:, (N_GROUPS + g) * gw:(N_GROUPS + g + 1) * gw]
            v_tail = qkv3[:, s - w:, (2 * N_GROUPS + g) * gw:(2 * N_GROUPS + g + 1) * gw]
            kv = jnp.stack([k_tail, v_tail], axis=2).astype(F32)
            kv_p[g].append(kv.reshape(b, w, 2, HEADS, HEAD_DIM))
        yp, tail = _block_prompt(yp, rest3, os_, ls_, weights, tm=256, ff_chunk=1024)
        conv_p.append(tail[:, SUBLANES - (CONV_WIDTH - 1):, :])

        qkv_s, rest_s = _project(ys, g_pre, w_in_b, tm=db, out_dtype=F32)
        os_, ls_ = [], []
        for g in range(N_GROUPS):
            length = caches[g].shape[2]
            cache_t = jnp.transpose(caches[g][l], (0, 2, 3, 4, 1)).reshape(db, 2, gw, length)
            qkv_g = jnp.concatenate([qkv_s[:, (j * N_GROUPS + g) * gw:(j * N_GROUPS + g + 1) * gw]
                                     for j in range(3)], axis=1)
            cols = jnp.pad(qkv_g, ((0, LANES - db), (0, 0))).T
            new_t, o_t, l_t = _sample_cache(cols, cache_t, g)
            os_.append(o_t.T[:db])
            ls_.append(l_t.T[:db])
            new = jnp.transpose(new_t.reshape(db, 2, HEADS, HEAD_DIM, length), (0, 4, 1, 2, 3))
            kv_s[g].append(new)
        state = state_conv[l]
        ys, u_s = _block_sample(ys, rest_s, state.reshape(db, (CONV_WIDTH - 1) * d), os_, ls_,
                                weights, ff_chunk=1024)
        conv_s.append(jnp.stack([state[:, 1, :], u_s], axis=1))

    stack = lambda xs: jnp.stack(xs, axis=0)
    return (yp, ys.reshape(db, 1, d),
            stack(kv_p[0]), stack(kv_p[1]), stack(kv_p[2]), stack(conv_p),
            stack(kv_s[0]), stack(kv_s[1]), stack(kv_s[2]), stack(conv_s))
```

```python
import functools

import jax
import jax.numpy as jnp
from jax import lax
from jax.experimental import pallas as pl
from jax.experimental.pallas import tpu as pltpu

HEAD_DIM = 64
HEADS = 8
GROUP_WIDTH = HEADS * HEAD_DIM
N_GROUPS = 3
WINDOWS = (128, 512, 2048)
DILATIONS = (1, 4, 16)
BAND = 128
CONV_WIDTH = 3
EPS = 1e-6
NEG = -1e30
LANES = 128
SUBLANES = 8
VMEM_LIMIT = 56 * 1024 * 1024

F32 = jnp.float32
BF16 = jnp.bfloat16


def _rms(x, g):
    return (x * lax.rsqrt(jnp.mean(x * x, axis=-1, keepdims=True) + EPS)) * g


def _resident(shape):
    nd = len(shape)
    return pl.BlockSpec(shape, lambda *_: (0,) * nd, pipeline_mode=pl.Buffered(1))


def _proj_kernel(x_ref, g_ref, w_ref, q0_ref, q1_ref, q2_ref, rest_ref, h_scr, hb_scr, *, tn):
    tm = x_ref.shape[0]
    h = _rms(x_ref[...], g_ref[...])
    hb_scr[0] = h.astype(BF16)
    for j in range(h_scr.shape[0]):
        h_scr[j] = h[:, j * LANES:(j + 1) * LANES]
    for g in range(1, N_GROUPS):
        r = DILATIONS[g]
        for c in range(r):
            for j in range(h_scr.shape[0]):
                hb_scr[g, c * (tm // r):(c + 1) * (tm // r), j * LANES:(j + 1) * LANES] = (
                    h_scr[j, pl.ds(c, tm // r, stride=r), :].astype(BF16))
    nq = 3 * N_GROUPS * tn
    for g, out in enumerate((q0_ref, q1_ref, q2_ref)):
        r = DILATIONS[g]
        for j in range(3):
            c0 = (j * N_GROUPS + g) * tn
            res = jnp.dot(hb_scr[g], w_ref[:, c0:c0 + tn], preferred_element_type=F32)
            out[:, :, j * tn:(j + 1) * tn] = res.reshape(r, tm // r, tn).astype(out.dtype)
    for c0 in range(nq, w_ref.shape[-1], tn):
        res = jnp.dot(hb_scr[0], w_ref[:, c0:c0 + tn], preferred_element_type=F32)
        rest_ref[:, c0 - nq:c0 - nq + tn] = res.astype(rest_ref.dtype)


def _project(x3, g, w_bf16, *, tm, out_dtype):
    b, s, d = x3.shape
    n = w_bf16.shape[1]
    tn = GROUP_WIDTH
    nq = 3 * N_GROUPS * tn
    grp_shape = lambda r: jax.ShapeDtypeStruct((b, r, s // r, 3 * tn), out_dtype)
    grp_spec = lambda r: pl.BlockSpec((None, r, tm // r, 3 * tn), lambda bb, i: (bb, 0, i, 0))
    return pl.pallas_call(
        functools.partial(_proj_kernel, tn=tn),
        out_shape=tuple(grp_shape(r) for r in DILATIONS)
        + (jax.ShapeDtypeStruct((b, s, n - nq), out_dtype),),
        grid=(b, s // tm),
        in_specs=[pl.BlockSpec((None, tm, d), lambda bb, i: (bb, i, 0)),
                  _resident((1, d)),
                  _resident((d, n))],
        out_specs=tuple(grp_spec(r) for r in DILATIONS)
        + (pl.BlockSpec((None, tm, n - nq), lambda bb, i: (bb, i, 0)),),
        scratch_shapes=[pltpu.VMEM((d // LANES, tm, LANES), F32),
                        pltpu.VMEM((N_GROUPS, tm, d), BF16)],
        compiler_params=pltpu.CompilerParams(dimension_semantics=("arbitrary", "arbitrary"),
                                             vmem_limit_bytes=VMEM_LIMIT),
        name="proj",
    )(x3, g, w_bf16)


def _proj_rows_kernel(x_ref, g_ref, w_ref, qkv_ref, rest_ref, *, tn):
    h = _rms(x_ref[...], g_ref[...]).astype(BF16)
    nq = qkv_ref.shape[-1]
    for c0 in range(0, w_ref.shape[-1], tn):
        res = jnp.dot(h, w_ref[:, c0:c0 + tn], preferred_element_type=F32)
        if c0 < nq:
            qkv_ref[:, c0:c0 + tn] = res.astype(qkv_ref.dtype)
        else:
            rest_ref[:, c0 - nq:c0 - nq + tn] = res.astype(rest_ref.dtype)


def _project_rows(x2d, g, w_bf16, *, out_dtype):
    m, d = x2d.shape
    n = w_bf16.shape[1]
    tn = GROUP_WIDTH
    nq = 3 * N_GROUPS * tn
    return pl.pallas_call(
        functools.partial(_proj_rows_kernel, tn=tn),
        out_shape=(jax.ShapeDtypeStruct((m, nq), out_dtype),
                   jax.ShapeDtypeStruct((m, n - nq), out_dtype)),
        grid=(1,),
        in_specs=[pl.BlockSpec((m, d), lambda i: (0, 0)), _resident((1, d)), _resident((d, n))],
        out_specs=(pl.BlockSpec((m, nq), lambda i: (0, 0)),
                   pl.BlockSpec((m, n - nq), lambda i: (0, 0))),
        compiler_params=pltpu.CompilerParams(dimension_semantics=("arbitrary",),
                                             vmem_limit_bytes=VMEM_LIMIT),
        name="proj_sample",
    )(x2d, g, w_bf16)


def _attn_kernel(q_ref, kc_ref, vc_ref, kp_ref, vp_ref, o_ref, l_ref, kcat, vcat, *, tq):
    i = pl.program_id(2)
    kcat[0:BAND, :] = kp_ref[...]
    kcat[BAND:, :] = kc_ref[...]
    vcat[0:BAND, :] = vp_ref[...]
    vcat[BAND:, :] = vc_ref[...]

    qi = lax.broadcasted_iota(jnp.int32, (BAND, 2 * BAND), 0)
    ki = lax.broadcasted_iota(jnp.int32, (BAND, 2 * BAND), 1)
    rel = qi + BAND - ki
    band_ok = (rel >= 0) & (rel <= BAND)
    lane = lax.broadcasted_iota(jnp.int32, (BAND, LANES), 1)
    low = lane < HEAD_DIM

    for qb in range(tq // BAND):
        r0 = qb * BAND
        if qb == 0:
            valid = band_ok & ((ki >= BAND) | (i > 0))
        else:
            valid = band_ok
        for hp in range(GROUP_WIDTH // LANES):
            c0 = hp * LANES
            q2 = q_ref[r0:r0 + BAND, c0:c0 + LANES]
            k2 = kcat[r0:r0 + 2 * BAND, c0:c0 + LANES]
            v2 = vcat[r0:r0 + 2 * BAND, c0:c0 + LANES]
            outs, lses = [], []
            for half in range(2):
                sel = low if half == 0 else ~low
                qm = jnp.where(sel, q2, jnp.zeros_like(q2))
                s = lax.dot_general(qm, k2, (((1,), (1,)), ((), ())),
                                    preferred_element_type=F32) * (HEAD_DIM ** -0.5)
                s = jnp.where(valid, s, NEG)
                m = jnp.max(s, axis=-1, keepdims=True)
                p = jnp.exp(s - m)
                den = jnp.sum(p, axis=-1, keepdims=True)
                pv = jnp.dot(p.astype(BF16), v2, preferred_element_type=F32)
                outs.append(pv / den)
                lses.append(m + jnp.log(den))
            o_ref[r0:r0 + BAND, c0:c0 + LANES] = jnp.where(low, outs[0], outs[1]).astype(o_ref.dtype)
            l_ref[r0:r0 + BAND, c0:c0 + LANES] = jnp.where(low, lses[0], lses[1])


def _prompt_attention(qkv_g, g, *, tq):
    b, r, sr, _ = qkv_g.shape
    tq = min(tq, sr)
    per = tq // BAND

    def cur(j):
        return pl.BlockSpec((None, None, tq, GROUP_WIDTH), lambda bb, c, i: (bb, c, i, j))

    def prev(j):
        return pl.BlockSpec((None, None, BAND, GROUP_WIDTH),
                            lambda bb, c, i: (bb, c, jnp.maximum(i * per - 1, 0), j))

    out_spec = pl.BlockSpec((None, None, tq, GROUP_WIDTH), lambda bb, c, i: (bb, c, i, 0))
    return pl.pallas_call(
        functools.partial(_attn_kernel, tq=tq),
        out_shape=(jax.ShapeDtypeStruct((b, r, sr, GROUP_WIDTH), F32),
                   jax.ShapeDtypeStruct((b, r, sr, GROUP_WIDTH), F32)),
        grid=(b, r, sr // tq),
        in_specs=[cur(0), cur(1), cur(2), prev(1), prev(2)],
        out_specs=(out_spec, out_spec),
        scratch_shapes=[pltpu.VMEM((BAND + tq, GROUP_WIDTH), BF16),
                        pltpu.VMEM((BAND + tq, GROUP_WIDTH), BF16)],
        compiler_params=pltpu.CompilerParams(
            dimension_semantics=("arbitrary", "arbitrary", "arbitrary"),
            vmem_limit_bytes=VMEM_LIMIT),
        name=f"attn_g{g}",
    )(qkv_g, qkv_g, qkv_g, qkv_g, qkv_g)


def _merge_groups(os_, ls):
    m = jnp.maximum(jnp.maximum(ls[0], ls[1]), ls[2])
    es = [jnp.exp(l - m) for l in ls]
    num = es[0] * os_[0] + es[1] * os_[1] + es[2] * os_[2]
    return num / (es[0] + es[1] + es[2])


def _block_core(x, attn, c_b, conv_v, g_a, g_c, wao_ref, wco_ref, wo_ref, w1_ref, w2_ref,
                gpost, gpre, gfpost, *, ff_chunk):
    a = jnp.dot(attn.astype(BF16), wao_ref[...], preferred_element_type=F32)
    c = jnp.dot((c_b * conv_v).astype(BF16), wco_ref[...], preferred_element_type=F32)
    mix_in = jax.nn.sigmoid(g_a) * a + jax.nn.sigmoid(g_c) * c
    mix = jnp.dot(mix_in.astype(BF16), wo_ref[...], preferred_element_type=F32)
    x1 = x + _rms(mix, gpost)
    h2 = _rms(x1, gpre).astype(BF16)
    f = None
    for k0 in range(0, w1_ref.shape[1], ff_chunk):
        t = jnp.dot(h2, w1_ref[:, k0:k0 + ff_chunk], preferred_element_type=F32)
        t = jnp.square(jnp.maximum(t, 0.0)).astype(BF16)
        part = jnp.dot(t, w2_ref[k0:k0 + ff_chunk, :], preferred_element_type=F32)
        f = part if f is None else f + part
    return x1 + _rms(f, gfpost)


def _block_prompt_kernel(x_ref, rest_ref, hcc_ref, hch_ref, o0, l0, o1, l1, o2, l2,
                         convw_ref, wao_ref, wco_ref, wo_ref, w1_ref, w2_ref,
                         gpost_ref, gpre_ref, gfpost_ref, y_ref, tail_ref, u_scr, nat_scr,
                         *, ff_chunk):
    i = pl.program_id(1)
    tm, d = x_ref.shape
    vals = []
    for k, ref in enumerate((o0, l0, o1, l1, o2, l2)):
        r = ref.shape[0]
        if r == 1:
            vals.append(ref[0])
        else:
            nl = GROUP_WIDTH // LANES
            for c in range(r):
                for j in range(nl):
                    nat_scr[k * nl + j, pl.ds(c, tm // r, stride=r), :] = (
                        ref[c, :, j * LANES:(j + 1) * LANES])
            vals.append(jnp.concatenate([nat_scr[k * nl + j] for j in range(nl)], axis=1))
    attn = _merge_groups(vals[0::2], vals[1::2])
    c_b = rest_ref[:, 0:d].astype(F32)
    u = rest_ref[:, d:2 * d].astype(F32) * rest_ref[:, 2 * d:3 * d].astype(F32)
    g_a = rest_ref[:, 3 * d:4 * d].astype(F32)
    g_c = rest_ref[:, 4 * d:5 * d].astype(F32)
    hu = hcc_ref[...].astype(F32) * hch_ref[...].astype(F32)
    u_scr[0:SUBLANES, :] = jnp.where(i > 0, hu, jnp.zeros_like(hu))
    u_scr[SUBLANES:, :] = u
    w = convw_ref[...]
    conv_v = (w[0:1, :] * u_scr[SUBLANES - 2:SUBLANES - 2 + tm, :]
              + w[1:2, :] * u_scr[SUBLANES - 1:SUBLANES - 1 + tm, :]
              + w[2:3, :] * u)
    y_ref[...] = _block_core(x_ref[...], attn, c_b, conv_v, g_a, g_c,
                             wao_ref, wco_ref, wo_ref, w1_ref, w2_ref,
                             gpost_ref[...], gpre_ref[...], gfpost_ref[...], ff_chunk=ff_chunk)
    tail_ref[...] = u[tm - SUBLANES:tm, :]


def _block_sample_kernel(x_ref, rest_ref, st_ref, o0, l0, o1, l1, o2, l2,
                         convw_ref, wao_ref, wco_ref, wo_ref, w1_ref, w2_ref,
                         gpost_ref, gpre_ref, gfpost_ref, y_ref, u_ref, *, ff_chunk):
    d = x_ref.shape[1]
    attn = _merge_groups((o0[...], o1[...], o2[...]), (l0[...], l1[...], l2[...]))
    c_b = rest_ref[:, 0:d].astype(F32)
    u = rest_ref[:, d:2 * d].astype(F32) * rest_ref[:, 2 * d:3 * d].astype(F32)
    g_a = rest_ref[:, 3 * d:4 * d].astype(F32)
    g_c = rest_ref[:, 4 * d:5 * d].astype(F32)
    w = convw_ref[...]
    conv_v = w[0:1, :] * st_ref[:, 0:d] + w[1:2, :] * st_ref[:, d:2 * d] + w[2:3, :] * u
    y_ref[...] = _block_core(x_ref[...], attn, c_b, conv_v, g_a, g_c,
                             wao_ref, wco_ref, wo_ref, w1_ref, w2_ref,
                             gpost_ref[...], gpre_ref[...], gfpost_ref[...], ff_chunk=ff_chunk)
    u_ref[...] = u


def _weight_specs(ws):
    return [_resident(w.shape) for w in ws]


def _block_prompt(x3, rest3, os_, ls_, weights, *, tm, ff_chunk):
    b, s, d = x3.shape
    nr = rest3.shape[-1]
    per = tm // SUBLANES
    row = lambda width: pl.BlockSpec((None, tm, width), lambda bb, i: (bb, i, 0))
    halo = lambda col: pl.BlockSpec((None, SUBLANES, d),
                                    lambda bb, i: (bb, jnp.maximum(i * per - 1, 0), col))
    in_specs = [row(d), row(nr), halo(1), halo(2)]
    args = [x3, rest3, rest3, rest3]
    for o, l in zip(os_, ls_):
        r = o.shape[1]
        res = pl.BlockSpec((None, r, tm // r, GROUP_WIDTH), lambda bb, i: (bb, 0, i, 0))
        in_specs += [res, res]
        args += [o, l]
    in_specs += _weight_specs(weights)
    args += list(weights)
    y, tail = pl.pallas_call(
        functools.partial(_block_prompt_kernel, ff_chunk=ff_chunk),
        out_shape=(jax.ShapeDtypeStruct((b, s, d), F32),
                   jax.ShapeDtypeStruct((b, SUBLANES, d), F32)),
        grid=(b, s // tm),
        in_specs=in_specs,
        out_specs=(row(d), pl.BlockSpec((None, SUBLANES, d), lambda bb, i: (bb, 0, 0))),
        scratch_shapes=[pltpu.VMEM((SUBLANES + tm, d), F32),
                        pltpu.VMEM((2 * N_GROUPS * GROUP_WIDTH // LANES, tm, LANES), F32)],
        compiler_params=pltpu.CompilerParams(dimension_semantics=("arbitrary", "arbitrary"),
                                             vmem_limit_bytes=VMEM_LIMIT),
        name="block_prompt",
    )(*args)
    return y, tail


def _block_sample(x2, rest2, state2, os_, ls_, weights, *, ff_chunk):
    m, d = x2.shape
    full = lambda a: pl.BlockSpec(a.shape, lambda i: (0,) * a.ndim)
    args = [x2, rest2, state2]
    for o, l in zip(os_, ls_):
        args += [o, l]
    in_specs = [full(a) for a in args] + _weight_specs(weights)
    args += list(weights)
    return pl.pallas_call(
        functools.partial(_block_sample_kernel, ff_chunk=ff_chunk),
        out_shape=(jax.ShapeDtypeStruct((m, d), F32), jax.ShapeDtypeStruct((m, d), F32)),
        grid=(1,),
        in_specs=in_specs,
        out_specs=(pl.BlockSpec((m, d), lambda i: (0, 0)), pl.BlockSpec((m, d), lambda i: (0, 0))),
        compiler_params=pltpu.CompilerParams(dimension_semantics=("arbitrary",),
                                             vmem_limit_bytes=VMEM_LIMIT),
        name="block_sample",
    )(*args)


def _sample_cache_kernel(cols_ref, cache_ref, out_ref, o_ref, l_ref, *, dil):
    b = pl.program_id(0)
    length = cache_ref.shape[-1]
    gw = GROUP_WIDTH
    scale = HEAD_DIM ** -0.5

    @pl.when(b == 0)
    def _():
        o_ref[...] = jnp.zeros_like(o_ref)
        l_ref[...] = jnp.zeros_like(l_ref)

    lane = lax.broadcasted_iota(jnp.int32, (1, LANES), 1)
    mine = lane == b
    cols = jnp.sum(jnp.where(mine, cols_ref[...], 0.0), axis=-1, keepdims=True)
    qc, knc, vnc = cols[0:gw], cols[gw:2 * gw], cols[2 * gw:3 * gw]

    row = lax.broadcasted_iota(jnp.int32, (1, length), 1)
    dist = length - row
    on_grid = ((dist % dil) == 0) & (dist <= BAND * dil)
    for h in range(HEADS):
        rows = slice(h * HEAD_DIM, (h + 1) * HEAD_DIM)
        s = jnp.sum(cache_ref[0, rows, :] * qc[rows], axis=0, keepdims=True) * scale
        s = jnp.where(on_grid, s, NEG)
        s_new = jnp.sum(qc[rows] * knc[rows], axis=0, keepdims=True) * scale
        m = jnp.maximum(jnp.max(s, axis=-1, keepdims=True), s_new)
        p = jnp.exp(s - m)
        p_new = jnp.exp(s_new - m)
        den = jnp.sum(p, axis=-1, keepdims=True) + p_new
        acc = jnp.sum(cache_ref[1, rows, :] * p, axis=-1, keepdims=True) + p_new * vnc[rows]
        o_ref[rows, :] = jnp.where(mine, acc / den, o_ref[rows, :])
        l_ref[rows, :] = jnp.where(mine, m + jnp.log(den), l_ref[rows, :])

    last = row == length - 1
    for kv, newc in ((0, knc), (1, vnc)):
        rolled = pltpu.roll(cache_ref[kv], length - 1, axis=1)
        out_ref[kv] = jnp.where(last, newc, rolled)


def _sample_cache(cols, cache_t, g):
    db, _, gw, length = cache_t.shape
    blk = pl.BlockSpec((None, 2, gw, length), lambda i: (i, 0, 0, 0))
    acc = pl.BlockSpec((gw, LANES), lambda i: (0, 0))
    return pl.pallas_call(
        functools.partial(_sample_cache_kernel, dil=DILATIONS[g]),
        out_shape=(jax.ShapeDtypeStruct(cache_t.shape, cache_t.dtype),
                   jax.ShapeDtypeStruct((gw, LANES), F32),
                   jax.ShapeDtypeStruct((gw, LANES), F32)),
        grid=(db,),
        in_specs=[_resident(cols.shape), blk],
        out_specs=(blk, acc, acc),
        compiler_params=pltpu.CompilerParams(dimension_semantics=("arbitrary",),
                                             vmem_limit_bytes=VMEM_LIMIT),
        name=f"sample_cache_g{g}",
    )(cols, cache_t)


def kernel(x_prompt, x_sample, cache_kv_w128, cache_kv_w512, cache_kv_w2048, state_conv, w_in, conv_w, w_attn_o, w_conv_o, w_o, w_ff1, w_ff2, g_mix_pre, g_mix_post, g_ffn_pre, g_ffn_post):
    depth = w_in.shape[0]
    b, s, d = x_prompt.shape
    db, ds_, _ = x_sample.shape
    assert ds_ == 1 and d % LANES == 0 and db <= LANES
    caches = (cache_kv_w128, cache_kv_w512, cache_kv_w2048)
    gw = GROUP_WIDTH

    yp, ys = x_prompt, x_sample.reshape(db, d)
    kv_p = [[] for _ in range(N_GROUPS)]
    kv_s = [[] for _ in range(N_GROUPS)]
    conv_p, conv_s = [], []
    for l in range(depth):
        w_in_b = w_in[l].astype(BF16)
        weights = (conv_w[l], w_attn_o[l].astype(BF16), w_conv_o[l].astype(BF16),
                   w_o[l].astype(BF16), w_ff1[l].astype(BF16), w_ff2[l].astype(BF16),
                   g_mix_post[l][None, :], g_ffn_pre[l][None, :], g_ffn_post[l][None, :])
        g_pre = g_mix_pre[l][None, :]

        *qkv_groups, rest3 = _project(yp, g_pre, w_in_b, tm=512, out_dtype=BF16)
        os_, ls_ = [], []
        for g in range(N_GROUPS):
            o, lse = _prompt_attention(qkv_groups[g], g, tq=512)
            os_.append(o)
            ls_.append(lse)
            r = DILATIONS[g]
            w = min(WINDOWS[g], s)
            kv_tail = qkv_groups[g][:, :, s // r - w // r:, gw:3 * gw]
            kv_tail = jnp.transpose(kv_tail, (0, 2, 1, 3)).reshape(b, w, 2, HEADS, HEAD_DIM)
            kv_p[g].append(kv_tail.astype(F32))
        yp, tail = _block_prompt(yp, rest3, os_, ls_, weights, tm=256, ff_chunk=1024)
        conv_p.append(tail[:, SUBLANES - (CONV_WIDTH - 1):, :])

        qkv_s, rest_s = _project_rows(ys, g_pre, w_in_b, out_dtype=F32)
        os_, ls_ = [], []
        for g in range(N_GROUPS):
            length = caches[g].shape[2]
            cache_t = jnp.transpose(caches[g][l], (0, 2, 3, 4, 1)).reshape(db, 2, gw, length)
            qkv_g = jnp.concatenate([qkv_s[:, (j * N_GROUPS + g) * gw:(j * N_GROUPS + g + 1) * gw]
                                     for j in range(3)], axis=1)
            cols = jnp.pad(qkv_g, ((0, LANES - db), (0, 0))).T
            new_t, o_t, l_t = _sample_cache(cols, cache_t, g)
            os_.append(o_t.T[:db])
            ls_.append(l_t.T[:db])
            new = jnp.transpose(new_t.reshape(db, 2, HEADS, HEAD_DIM, length), (0, 4, 1, 2, 3))
            kv_s[g].append(new)
        state = state_conv[l]
        ys, u_s = _block_sample(ys, rest_s, state.reshape(db, (CONV_WIDTH - 1) * d), os_, ls_,
                                weights, ff_chunk=1024)
        conv_s.append(jnp.stack([state[:, 1, :], u_s], axis=1))

    stack = lambda xs: jnp.stack(xs, axis=0)
    return (yp, ys.reshape(db, 1, d),
            stack(kv_p[0]), stack(kv_p[1]), stack(kv_p[2]), stack(conv_p),
            stack(kv_s[0]), stack(kv_s[1]), stack(kv_s[2]), stack(conv_s))
```

```python
import functools

import jax
import jax.numpy as jnp
from jax import lax
from jax.experimental import pallas as pl
from jax.experimental.pallas import tpu as pltpu

HEAD_DIM = 64
HEADS = 8
GROUP_WIDTH = HEADS * HEAD_DIM
N_GROUPS = 3
WINDOWS = (128, 512, 2048)
DILATIONS = (1, 4, 16)
BAND = 128
CONV_WIDTH = 3
EPS = 1e-6
NEG = -1e30
LOG2E = 1.4426950408889634
LN2 = 0.6931471805599453
LANES = 128
SUBLANES = 8
VMEM_LIMIT = 56 * 1024 * 1024

F32 = jnp.float32
BF16 = jnp.bfloat16


def _rms(x, g):
    return (x * lax.rsqrt(jnp.mean(x * x, axis=-1, keepdims=True) + EPS)) * g


def _resident(shape):
    nd = len(shape)
    return pl.BlockSpec(shape, lambda *_: (0,) * nd, pipeline_mode=pl.Buffered(1))


def _proj_kernel(x_ref, g_ref, w_ref, q0_ref, q1_ref, q2_ref, rest_ref, h_scr, hb_scr, *, tn):
    tm = x_ref.shape[0]
    h = _rms(x_ref[...], g_ref[...])
    hb_scr[0] = h.astype(BF16)
    for j in range(h_scr.shape[0]):
        h_scr[j] = h[:, j * LANES:(j + 1) * LANES]
    for g in range(1, N_GROUPS):
        r = DILATIONS[g]
        for c in range(r):
            for j in range(h_scr.shape[0]):
                hb_scr[g, c * (tm // r):(c + 1) * (tm // r), j * LANES:(j + 1) * LANES] = (
                    h_scr[j, pl.ds(c, tm // r, stride=r), :].astype(BF16))
    nq = 3 * N_GROUPS * tn
    for g, out in enumerate((q0_ref, q1_ref, q2_ref)):
        r = DILATIONS[g]
        for j in range(3):
            c0 = (j * N_GROUPS + g) * tn
            res = jnp.dot(hb_scr[g], w_ref[:, c0:c0 + tn], preferred_element_type=F32)
            if j == 0:
                res = res * (HEAD_DIM ** -0.5 * LOG2E)
            out[:, :, j * tn:(j + 1) * tn] = res.reshape(r, tm // r, tn).astype(out.dtype)
    for c0 in range(nq, w_ref.shape[-1], tn):
        res = jnp.dot(hb_scr[0], w_ref[:, c0:c0 + tn], preferred_element_type=F32)
        rest_ref[:, c0 - nq:c0 - nq + tn] = res.astype(rest_ref.dtype)


def _project(x3, g, w_bf16, *, tm, out_dtype):
    b, s, d = x3.shape
    n = w_bf16.shape[1]
    tn = GROUP_WIDTH
    nq = 3 * N_GROUPS * tn
    grp_shape = lambda r: jax.ShapeDtypeStruct((b, r, s // r, 3 * tn), out_dtype)
    grp_spec = lambda r: pl.BlockSpec((None, r, tm // r, 3 * tn), lambda bb, i: (bb, 0, i, 0))
    return pl.pallas_call(
        functools.partial(_proj_kernel, tn=tn),
        out_shape=tuple(grp_shape(r) for r in DILATIONS)
        + (jax.ShapeDtypeStruct((b, s, n - nq), out_dtype),),
        grid=(b, s // tm),
        in_specs=[pl.BlockSpec((None, tm, d), lambda bb, i: (bb, i, 0)),
                  _resident((1, d)),
                  _resident((d, n))],
        out_specs=tuple(grp_spec(r) for r in DILATIONS)
        + (pl.BlockSpec((None, tm, n - nq), lambda bb, i: (bb, i, 0)),),
        scratch_shapes=[pltpu.VMEM((d // LANES, tm, LANES), F32),
                        pltpu.VMEM((N_GROUPS, tm, d), BF16)],
        compiler_params=pltpu.CompilerParams(dimension_semantics=("arbitrary", "arbitrary"),
                                             vmem_limit_bytes=VMEM_LIMIT),
        name="proj",
    )(x3, g, w_bf16)


def _proj_rows_kernel(x_ref, g_ref, w_ref, qkv_ref, rest_ref, *, tn):
    h = _rms(x_ref[...], g_ref[...]).astype(BF16)
    nq = qkv_ref.shape[-1]
    for c0 in range(0, w_ref.shape[-1], tn):
        res = jnp.dot(h, w_ref[:, c0:c0 + tn], preferred_element_type=F32)
        if c0 < nq:
            qkv_ref[:, c0:c0 + tn] = res.astype(qkv_ref.dtype)
        else:
            rest_ref[:, c0 - nq:c0 - nq + tn] = res.astype(rest_ref.dtype)


def _project_rows(x2d, g, w_bf16, *, out_dtype):
    m, d = x2d.shape
    n = w_bf16.shape[1]
    tn = GROUP_WIDTH
    nq = 3 * N_GROUPS * tn
    return pl.pallas_call(
        functools.partial(_proj_rows_kernel, tn=tn),
        out_shape=(jax.ShapeDtypeStruct((m, nq), out_dtype),
                   jax.ShapeDtypeStruct((m, n - nq), out_dtype)),
        grid=(1,),
        in_specs=[pl.BlockSpec((m, d), lambda i: (0, 0)), _resident((1, d)), _resident((d, n))],
        out_specs=(pl.BlockSpec((m, nq), lambda i: (0, 0)),
                   pl.BlockSpec((m, n - nq), lambda i: (0, 0))),
        compiler_params=pltpu.CompilerParams(dimension_semantics=("arbitrary",),
                                             vmem_limit_bytes=VMEM_LIMIT),
        name="proj_sample",
    )(x2d, g, w_bf16)


def _attn_kernel(q_ref, kc_ref, vc_ref, kp_ref, vp_ref, o_ref, l_ref, kcat, vcat, *, tq):
    i = pl.program_id(2)
    kcat[0:BAND, :] = kp_ref[...]
    kcat[BAND:, :] = kc_ref[...]
    vcat[0:BAND, :] = vp_ref[...]
    vcat[BAND:, :] = vc_ref[...]

    qi = lax.broadcasted_iota(jnp.int32, (BAND, 2 * BAND), 0)
    ki = lax.broadcasted_iota(jnp.int32, (BAND, 2 * BAND), 1)
    rel = qi + BAND - ki
    band_ok = (rel >= 0) & (rel <= BAND)
    lane = lax.broadcasted_iota(jnp.int32, (BAND, LANES), 1)
    low = lane < HEAD_DIM

    for qb in range(tq // BAND):
        r0 = qb * BAND
        if qb == 0:
            valid = band_ok & ((ki >= BAND) | (i > 0))
        else:
            valid = band_ok
        valid2 = jnp.concatenate([valid, valid], axis=0)
        for hp in range(GROUP_WIDTH // LANES):
            c0 = hp * LANES
            q2 = q_ref[r0:r0 + BAND, c0:c0 + LANES]
            k2 = kcat[r0:r0 + 2 * BAND, c0:c0 + LANES]
            v2 = vcat[r0:r0 + 2 * BAND, c0:c0 + LANES]
            zero = jnp.zeros_like(q2)
            qm = jnp.concatenate([jnp.where(low, q2, zero), jnp.where(low, zero, q2)], axis=0)
            s = lax.dot_general(qm, k2, (((1,), (1,)), ((), ())), preferred_element_type=F32)
            s = jnp.where(valid2, s, NEG)
            m = jnp.max(s, axis=-1, keepdims=True)
            p = jnp.exp2(s - m)
            den = jnp.sum(p, axis=-1, keepdims=True)
            pv = jnp.dot(p.astype(BF16), v2, preferred_element_type=F32)
            o = pv / den
            lse = (m + jnp.log2(den)) * LN2
            o_ref[r0:r0 + BAND, c0:c0 + LANES] = jnp.where(low, o[0:BAND], o[BAND:]).astype(o_ref.dtype)
            l_ref[r0:r0 + BAND, c0:c0 + LANES] = jnp.where(low, lse[0:BAND], lse[BAND:])


def _prompt_attention(qkv_g, g, *, tq):
    b, r, sr, _ = qkv_g.shape
    tq = min(tq, sr)
    per = tq // BAND

    def cur(j):
        return pl.BlockSpec((None, None, tq, GROUP_WIDTH), lambda bb, c, i: (bb, c, i, j))

    def prev(j):
        return pl.BlockSpec((None, None, BAND, GROUP_WIDTH),
                            lambda bb, c, i: (bb, c, jnp.maximum(i * per - 1, 0), j))

    out_spec = pl.BlockSpec((None, None, tq, GROUP_WIDTH), lambda bb, c, i: (bb, c, i, 0))
    return pl.pallas_call(
        functools.partial(_attn_kernel, tq=tq),
        out_shape=(jax.ShapeDtypeStruct((b, r, sr, GROUP_WIDTH), F32),
                   jax.ShapeDtypeStruct((b, r, sr, GROUP_WIDTH), F32)),
        grid=(b, r, sr // tq),
        in_specs=[cur(0), cur(1), cur(2), prev(1), prev(2)],
        out_specs=(out_spec, out_spec),
        scratch_shapes=[pltpu.VMEM((BAND + tq, GROUP_WIDTH), BF16),
                        pltpu.VMEM((BAND + tq, GROUP_WIDTH), BF16)],
        compiler_params=pltpu.CompilerParams(
            dimension_semantics=("arbitrary", "arbitrary", "arbitrary"),
            vmem_limit_bytes=VMEM_LIMIT),
        name=f"attn_g{g}",
    )(qkv_g, qkv_g, qkv_g, qkv_g, qkv_g)


def _merge_groups(os_, ls):
    m = jnp.maximum(jnp.maximum(ls[0], ls[1]), ls[2])
    es = [jnp.exp(l - m) for l in ls]
    num = es[0] * os_[0] + es[1] * os_[1] + es[2] * os_[2]
    return num / (es[0] + es[1] + es[2])


def _block_core(x, attn, c_b, conv_v, g_a, g_c, wao_ref, wco_ref, wo_ref, w1_ref, w2_ref,
                gpost, gpre, gfpost, *, ff_chunk):
    a = jnp.dot(attn.astype(BF16), wao_ref[...], preferred_element_type=F32)
    c = jnp.dot((c_b * conv_v).astype(BF16), wco_ref[...], preferred_element_type=F32)
    mix_in = jax.nn.sigmoid(g_a) * a + jax.nn.sigmoid(g_c) * c
    mix = jnp.dot(mix_in.astype(BF16), wo_ref[...], preferred_element_type=F32)
    x1 = x + _rms(mix, gpost)
    h2 = _rms(x1, gpre).astype(BF16)
    f = None
    for k0 in range(0, w1_ref.shape[1], ff_chunk):
        t = jnp.dot(h2, w1_ref[:, k0:k0 + ff_chunk], preferred_element_type=F32)
        t = jnp.square(jnp.maximum(t, 0.0)).astype(BF16)
        part = jnp.dot(t, w2_ref[k0:k0 + ff_chunk, :], preferred_element_type=F32)
        f = part if f is None else f + part
    return x1 + _rms(f, gfpost)


def _block_prompt_kernel(x_ref, rest_ref, hcc_ref, hch_ref, o0, l0, o1, l1, o2, l2,
                         convw_ref, wao_ref, wco_ref, wo_ref, w1_ref, w2_ref,
                         gpost_ref, gpre_ref, gfpost_ref, y_ref, tail_ref, u_scr, nat_scr,
                         *, ff_chunk):
    i = pl.program_id(1)
    tm, d = x_ref.shape
    vals = []
    for k, ref in enumerate((o0, l0, o1, l1, o2, l2)):
        r = ref.shape[0]
        if r == 1:
            vals.append(ref[0])
        else:
            nl = GROUP_WIDTH // LANES
            for c in range(r):
                for j in range(nl):
                    nat_scr[k * nl + j, pl.ds(c, tm // r, stride=r), :] = (
                        ref[c, :, j * LANES:(j + 1) * LANES])
            vals.append(jnp.concatenate([nat_scr[k * nl + j] for j in range(nl)], axis=1))
    attn = _merge_groups(vals[0::2], vals[1::2])
    c_b = rest_ref[:, 0:d].astype(F32)
    u = rest_ref[:, d:2 * d].astype(F32) * rest_ref[:, 2 * d:3 * d].astype(F32)
    g_a = rest_ref[:, 3 * d:4 * d].astype(F32)
    g_c = rest_ref[:, 4 * d:5 * d].astype(F32)
    hu = hcc_ref[...].astype(F32) * hch_ref[...].astype(F32)
    u_scr[0:SUBLANES, :] = jnp.where(i > 0, hu, jnp.zeros_like(hu))
    u_scr[SUBLANES:, :] = u
    w = convw_ref[...]
    conv_v = (w[0:1, :] * u_scr[SUBLANES - 2:SUBLANES - 2 + tm, :]
              + w[1:2, :] * u_scr[SUBLANES - 1:SUBLANES - 1 + tm, :]
              + w[2:3, :] * u)
    y_ref[...] = _block_core(x_ref[...], attn, c_b, conv_v, g_a, g_c,
                             wao_ref, wco_ref, wo_ref, w1_ref, w2_ref,
                             gpost_ref[...], gpre_ref[...], gfpost_ref[...], ff_chunk=ff_chunk)
    tail_ref[...] = u[tm - SUBLANES:tm, :]


def _block_sample_kernel(x_ref, rest_ref, st_ref, o0, l0, o1, l1, o2, l2,
                         convw_ref, wao_ref, wco_ref, wo_ref, w1_ref, w2_ref,
                         gpost_ref, gpre_ref, gfpost_ref, y_ref, u_ref, *, ff_chunk):
    d = x_ref.shape[1]
    attn = _merge_groups((o0[...], o1[...], o2[...]), (l0[...], l1[...], l2[...]))
    c_b = rest_ref[:, 0:d].astype(F32)
    u = rest_ref[:, d:2 * d].astype(F32) * rest_ref[:, 2 * d:3 * d].astype(F32)
    g_a = rest_ref[:, 3 * d:4 * d].astype(F32)
    g_c = rest_ref[:, 4 * d:5 * d].astype(F32)
    w = convw_ref[...]
    conv_v = w[0:1, :] * st_ref[:, 0:d] + w[1:2, :] * st_ref[:, d:2 * d] + w[2:3, :] * u
    y_ref[...] = _block_core(x_ref[...], attn, c_b, conv_v, g_a, g_c,
                             wao_ref, wco_ref, wo_ref, w1_ref, w2_ref,
                             gpost_ref[...], gpre_ref[...], gfpost_ref[...], ff_chunk=ff_chunk)
    u_ref[...] = u


def _weight_specs(ws):
    return [_resident(w.shape) for w in ws]


def _block_prompt(x3, rest3, os_, ls_, weights, *, tm, ff_chunk):
    b, s, d = x3.shape
    nr = rest3.shape[-1]
    per = tm // SUBLANES
    row = lambda width: pl.BlockSpec((None, tm, width), lambda bb, i: (bb, i, 0))
    halo = lambda col: pl.BlockSpec((None, SUBLANES, d),
                                    lambda bb, i: (bb, jnp.maximum(i * per - 1, 0), col))
    in_specs = [row(d), row(nr), halo(1), halo(2)]
    args = [x3, rest3, rest3, rest3]
    for o, l in zip(os_, ls_):
        r = o.shape[1]
        res = pl.BlockSpec((None, r, tm // r, GROUP_WIDTH), lambda bb, i: (bb, 0, i, 0))
        in_specs += [res, res]
        args += [o, l]
    in_specs += _weight_specs(weights)
    args += list(weights)
    y, tail = pl.pallas_call(
        functools.partial(_block_prompt_kernel, ff_chunk=ff_chunk),
        out_shape=(jax.ShapeDtypeStruct((b, s, d), F32),
                   jax.ShapeDtypeStruct((b, SUBLANES, d), F32)),
        grid=(b, s // tm),
        in_specs=in_specs,
        out_specs=(row(d), pl.BlockSpec((None, SUBLANES, d), lambda bb, i: (bb, 0, 0))),
        scratch_shapes=[pltpu.VMEM((SUBLANES + tm, d), F32),
                        pltpu.VMEM((2 * N_GROUPS * GROUP_WIDTH // LANES, tm, LANES), F32)],
        compiler_params=pltpu.CompilerParams(dimension_semantics=("arbitrary", "arbitrary"),
                                             vmem_limit_bytes=VMEM_LIMIT),
        name="block_prompt",
    )(*args)
    return y, tail


def _block_sample(x2, rest2, state2, os_, ls_, weights, *, ff_chunk):
    m, d = x2.shape
    full = lambda a: pl.BlockSpec(a.shape, lambda i: (0,) * a.ndim)
    args = [x2, rest2, state2]
    for o, l in zip(os_, ls_):
        args += [o, l]
    in_specs = [full(a) for a in args] + _weight_specs(weights)
    args += list(weights)
    return pl.pallas_call(
        functools.partial(_block_sample_kernel, ff_chunk=ff_chunk),
        out_shape=(jax.ShapeDtypeStruct((m, d), F32), jax.ShapeDtypeStruct((m, d), F32)),
        grid=(1,),
        in_specs=in_specs,
        out_specs=(pl.BlockSpec((m, d), lambda i: (0, 0)), pl.BlockSpec((m, d), lambda i: (0, 0))),
        compiler_params=pltpu.CompilerParams(dimension_semantics=("arbitrary",),
                                             vmem_limit_bytes=VMEM_LIMIT),
        name="block_sample",
    )(*args)


def _sample_cache_kernel(cols_ref, cache_ref, out_ref, o_ref, l_ref, *, dil):
    step = pl.program_id(0)
    bt = cache_ref.shape[0]
    length = cache_ref.shape[-1]
    gw = GROUP_WIDTH
    scale = HEAD_DIM ** -0.5

    @pl.when(step == 0)
    def _():
        o_ref[...] = jnp.zeros_like(o_ref)
        l_ref[...] = jnp.zeros_like(l_ref)

    lane = lax.broadcasted_iota(jnp.int32, (1, LANES), 1)
    row = lax.broadcasted_iota(jnp.int32, (1, length), 1)
    dist = length - row
    on_grid = ((dist % dil) == 0) & (dist <= BAND * dil)
    last = row == length - 1
    for bi in range(bt):
        mine = lane == step * bt + bi
        cols = jnp.sum(jnp.where(mine, cols_ref[...], 0.0), axis=-1, keepdims=True)
        qc, knc, vnc = cols[0:gw], cols[gw:2 * gw], cols[2 * gw:3 * gw]
        for h in range(HEADS):
            rows = slice(h * HEAD_DIM, (h + 1) * HEAD_DIM)
            s = jnp.sum(cache_ref[bi, 0, rows, :] * qc[rows], axis=0, keepdims=True) * scale
            s = jnp.where(on_grid, s, NEG)
            s_new = jnp.sum(qc[rows] * knc[rows], axis=0, keepdims=True) * scale
            m = jnp.maximum(jnp.max(s, axis=-1, keepdims=True), s_new)
            p = jnp.exp(s - m)
            p_new = jnp.exp(s_new - m)
            den = jnp.sum(p, axis=-1, keepdims=True) + p_new
            acc = (jnp.sum(cache_ref[bi, 1, rows, :] * p, axis=-1, keepdims=True)
                   + p_new * vnc[rows])
            o_ref[rows, :] = jnp.where(mine, acc / den, o_ref[rows, :])
            l_ref[rows, :] = jnp.where(mine, m + jnp.log(den), l_ref[rows, :])

        for kv, newc in ((0, knc), (1, vnc)):
            rolled = pltpu.roll(cache_ref[bi, kv], length - 1, axis=1)
            out_ref[bi, kv] = jnp.where(last, newc, rolled)


def _sample_cache(cols, cache_t, g, *, block_bytes):
    db, _, gw, length = cache_t.shape
    bt = max(1, min(db, block_bytes // (2 * gw * length * 4)))
    while db % bt:
        bt -= 1
    blk = pl.BlockSpec((bt, 2, gw, length), lambda i: (i, 0, 0, 0))
    acc = pl.BlockSpec((gw, LANES), lambda i: (0, 0))
    return pl.pallas_call(
        functools.partial(_sample_cache_kernel, dil=DILATIONS[g]),
        out_shape=(jax.ShapeDtypeStruct(cache_t.shape, cache_t.dtype),
                   jax.ShapeDtypeStruct((gw, LANES), F32),
                   jax.ShapeDtypeStruct((gw, LANES), F32)),
        grid=(db // bt,),
        in_specs=[_resident(cols.shape), blk],
        out_specs=(blk, acc, acc),
        compiler_params=pltpu.CompilerParams(dimension_semantics=("arbitrary",),
                                             vmem_limit_bytes=VMEM_LIMIT),
        name=f"sample_cache_g{g}",
    )(cols, cache_t)


def kernel(x_prompt, x_sample, cache_kv_w128, cache_kv_w512, cache_kv_w2048, state_conv, w_in, conv_w, w_attn_o, w_conv_o, w_o, w_ff1, w_ff2, g_mix_pre, g_mix_post, g_ffn_pre, g_ffn_post):
    depth = w_in.shape[0]
    b, s, d = x_prompt.shape
    db, ds_, _ = x_sample.shape
    assert ds_ == 1 and d % LANES == 0 and db <= LANES
    caches = (cache_kv_w128, cache_kv_w512, cache_kv_w2048)
    gw = GROUP_WIDTH

    yp, ys = x_prompt, x_sample.reshape(db, d)
    kv_p = [[] for _ in range(N_GROUPS)]
    kv_s = [[] for _ in range(N_GROUPS)]
    conv_p, conv_s = [], []
    for l in range(depth):
        w_in_b = w_in[l].astype(BF16)
        weights = (conv_w[l], w_attn_o[l].astype(BF16), w_conv_o[l].astype(BF16),
                   w_o[l].astype(BF16), w_ff1[l].astype(BF16), w_ff2[l].astype(BF16),
                   g_mix_post[l][None, :], g_ffn_pre[l][None, :], g_ffn_post[l][None, :])
        g_pre = g_mix_pre[l][None, :]

        *qkv_groups, rest3 = _project(yp, g_pre, w_in_b, tm=512, out_dtype=BF16)
        os_, ls_ = [], []
        for g in range(N_GROUPS):
            o, lse = _prompt_attention(qkv_groups[g], g, tq=512)
            os_.append(o)
            ls_.append(lse)
            r = DILATIONS[g]
            w = min(WINDOWS[g], s)
            kv_tail = qkv_groups[g][:, :, s // r - w // r:, gw:3 * gw]
            kv_tail = jnp.transpose(kv_tail, (0, 2, 1, 3)).reshape(b, w, 2, HEADS, HEAD_DIM)
            kv_p[g].append(kv_tail.astype(F32))
        yp, tail = _block_prompt(yp, rest3, os_, ls_, weights, tm=256, ff_chunk=1024)
        conv_p.append(tail[:, SUBLANES - (CONV_WIDTH - 1):, :])

        qkv_s, rest_s = _project_rows(ys, g_pre, w_in_b, out_dtype=F32)
        os_, ls_ = [], []
        for g in range(N_GROUPS):
            length = caches[g].shape[2]
            cache_t = jnp.transpose(caches[g][l], (0, 2, 3, 4, 1)).reshape(db, 2, gw, length)
            qkv_g = jnp.concatenate([qkv_s[:, (j * N_GROUPS + g) * gw:(j * N_GROUPS + g + 1) * gw]
                                     for j in range(3)], axis=1)
            cols = jnp.pad(qkv_g, ((0, LANES - db), (0, 0))).T
            new_t, o_t, l_t = _sample_cache(cols, cache_t, g, block_bytes=4 * 1024 * 1024)
            os_.append(o_t.T[:db])
            ls_.append(l_t.T[:db])
            new = jnp.transpose(new_t.reshape(db, 2, HEADS, HEAD_DIM, length), (0, 4, 1, 2, 3))
            kv_s[g].append(new)
        state = state_conv[l]
        ys, u_s = _block_sample(ys, rest_s, state.reshape(db, (CONV_WIDTH - 1) * d), os_, ls_,
                                weights, ff_chunk=1024)
        conv_s.append(jnp.stack([state[:, 1, :], u_s], axis=1))

    stack = lambda xs: jnp.stack(xs, axis=0)
    return (yp, ys.reshape(db, 1, d),
            stack(kv_p[0]), stack(kv_p[1]), stack(kv_p[2]), stack(conv_p),
            stack(kv_s[0]), stack(kv_s[1]), stack(kv_s[2]), stack(conv_s))
```

```python
import functools

import jax
import jax.numpy as jnp
from jax import lax
from jax.experimental import pallas as pl
from jax.experimental.pallas import tpu as pltpu

HEAD_DIM = 64
HEADS = 8
GROUP_WIDTH = HEADS * HEAD_DIM
N_GROUPS = 3
WINDOWS = (128, 512, 2048)
DILATIONS = (1, 4, 16)
BAND = 128
CONV_WIDTH = 3
EPS = 1e-6
NEG = -1e30
LOG2E = 1.4426950408889634
LN2 = 0.6931471805599453
LANES = 128
SUBLANES = 8
VMEM_LIMIT = 56 * 1024 * 1024

F32 = jnp.float32
BF16 = jnp.bfloat16


def _rms(x, g):
    return (x * lax.rsqrt(jnp.mean(x * x, axis=-1, keepdims=True) + EPS)) * g


def _resident(shape):
    nd = len(shape)
    return pl.BlockSpec(shape, lambda *_: (0,) * nd, pipeline_mode=pl.Buffered(1))


def _unit_index(units_per_seq):
    u = pl.program_id(0) * pl.num_programs(1) + pl.program_id(1)
    return u // units_per_seq, u % units_per_seq


def _proj_qkv_kernel(x_ref, g_ref, w_ref, cols_ref, cache_ref,
                     q0_ref, q1_ref, q2_ref, new_ref, so_ref, sl_ref, h_scr, hb_scr,
                     *, tn, dil, units_per_seq):
    tm = x_ref.shape[0]
    h = _rms(x_ref[...], g_ref[...])
    hb_scr[0] = h.astype(BF16)
    for j in range(h_scr.shape[0]):
        h_scr[j] = h[:, j * LANES:(j + 1) * LANES]
    for g in range(1, N_GROUPS):
        r = DILATIONS[g]
        for c in range(r):
            for j in range(h_scr.shape[0]):
                hb_scr[g, c * (tm // r):(c + 1) * (tm // r), j * LANES:(j + 1) * LANES] = (
                    h_scr[j, pl.ds(c, tm // r, stride=r), :].astype(BF16))
    for g, out in enumerate((q0_ref, q1_ref, q2_ref)):
        r = DILATIONS[g]
        for j in range(3):
            c0 = (j * N_GROUPS + g) * tn
            res = jnp.dot(hb_scr[g], w_ref[:, c0:c0 + tn], preferred_element_type=F32)
            if j == 0:
                res = res * (HEAD_DIM ** -0.5 * LOG2E)
            out[:, :, j * tn:(j + 1) * tn] = res.reshape(r, tm // r, tn).astype(out.dtype)
    seq, part = _unit_index(units_per_seq)
    _cache_unit(cols_ref, cache_ref, new_ref, so_ref, sl_ref, seq, part, dil)


def _proj_rest_kernel(x_ref, g_ref, w_ref, cols0_ref, cache0_ref, cols1_ref, cache1_ref,
                      rest_ref, new0_ref, so0_ref, sl0_ref, new1_ref, so1_ref, sl1_ref,
                      *, tn, dils, units_per_seq):
    h = _rms(x_ref[...], g_ref[...]).astype(BF16)
    for c0 in range(0, w_ref.shape[-1], tn):
        res = jnp.dot(h, w_ref[:, c0:c0 + tn], preferred_element_type=F32)
        rest_ref[:, c0:c0 + tn] = res.astype(rest_ref.dtype)
    seq, part = _unit_index(units_per_seq)
    _cache_unit(cols0_ref, cache0_ref, new0_ref, so0_ref, sl0_ref, seq, part, dils[0])
    _cache_unit(cols1_ref, cache1_ref, new1_ref, so1_ref, sl1_ref, seq, part, dils[1])


def _cache_specs(cache_t, steps, n_i):
    db, _, gw, length = cache_t.shape
    ups = steps // db
    assert steps == db * ups and gw % ups == 0 and (gw // ups) % HEAD_DIM == 0, (steps, db)
    rows = gw // ups
    unit = lambda bb, i: bb * n_i + i
    blk = pl.BlockSpec((None, 2, rows, length),
                       lambda bb, i: (unit(bb, i) // ups, 0, unit(bb, i) % ups, 0))
    stat = pl.BlockSpec((None, rows, LANES), lambda bb, i: (unit(bb, i), 0, 0))
    stat_shape = jax.ShapeDtypeStruct((steps, rows, LANES), F32)
    return ups, blk, stat, stat_shape


def _project_qkv(x3, g, w_qkv, cols, cache_t, *, tm, dil):
    b, s, d = x3.shape
    tn = GROUP_WIDTH
    n_i = s // tm
    ups, blk, stat, stat_shape = _cache_specs(cache_t, b * n_i, n_i)
    grp_shape = lambda r: jax.ShapeDtypeStruct((b, r, s // r, 3 * tn), BF16)
    grp_spec = lambda r: pl.BlockSpec((None, r, tm // r, 3 * tn), lambda bb, i: (bb, 0, i, 0))
    return pl.pallas_call(
        functools.partial(_proj_qkv_kernel, tn=tn, dil=dil, units_per_seq=ups),
        out_shape=tuple(grp_shape(r) for r in DILATIONS)
        + (jax.ShapeDtypeStruct(cache_t.shape, cache_t.dtype), stat_shape, stat_shape),
        grid=(b, n_i),
        in_specs=[pl.BlockSpec((None, tm, d), lambda bb, i: (bb, i, 0)),
                  _resident((1, d)), _resident(w_qkv.shape), _resident(cols.shape), blk],
        out_specs=tuple(grp_spec(r) for r in DILATIONS) + (blk, stat, stat),
        scratch_shapes=[pltpu.VMEM((d // LANES, tm, LANES), F32),
                        pltpu.VMEM((N_GROUPS, tm, d), BF16)],
        compiler_params=pltpu.CompilerParams(dimension_semantics=("arbitrary", "arbitrary"),
                                             vmem_limit_bytes=VMEM_LIMIT),
        name="proj_qkv",
    )(x3, g, w_qkv, cols, cache_t)


def _project_rest(x3, g, w_rest, cols01, caches01, *, tm, dils):
    b, s, d = x3.shape
    n = w_rest.shape[1]
    n_i = s // tm
    in_specs = [pl.BlockSpec((None, tm, d), lambda bb, i: (bb, i, 0)),
                _resident((1, d)), _resident(w_rest.shape)]
    out_shape = [jax.ShapeDtypeStruct((b, s, n), BF16)]
    out_specs = [pl.BlockSpec((None, tm, n), lambda bb, i: (bb, i, 0))]
    args = [x3, g, w_rest]
    for cols, cache_t in zip(cols01, caches01):
        ups, blk, stat, stat_shape = _cache_specs(cache_t, b * n_i, n_i)
        in_specs += [_resident(cols.shape), blk]
        args += [cols, cache_t]
        out_shape += [jax.ShapeDtypeStruct(cache_t.shape, cache_t.dtype), stat_shape, stat_shape]
        out_specs += [blk, stat, stat]
    return pl.pallas_call(
        functools.partial(_proj_rest_kernel, tn=GROUP_WIDTH, dils=dils, units_per_seq=ups),
        out_shape=tuple(out_shape),
        grid=(b, n_i),
        in_specs=in_specs,
        out_specs=tuple(out_specs),
        compiler_params=pltpu.CompilerParams(dimension_semantics=("arbitrary", "arbitrary"),
                                             vmem_limit_bytes=VMEM_LIMIT),
        name="proj_rest",
    )(*args)


def _proj_rows_kernel(x_ref, g_ref, w_ref, qkv_ref, rest_ref, *, tn):
    h = _rms(x_ref[...], g_ref[...]).astype(BF16)
    nq = qkv_ref.shape[-1]
    for c0 in range(0, w_ref.shape[-1], tn):
        res = jnp.dot(h, w_ref[:, c0:c0 + tn], preferred_element_type=F32)
        if c0 < nq:
            qkv_ref[:, c0:c0 + tn] = res.astype(qkv_ref.dtype)
        else:
            rest_ref[:, c0 - nq:c0 - nq + tn] = res.astype(rest_ref.dtype)


def _project_rows(x2d, g, w_bf16, *, out_dtype):
    m, d = x2d.shape
    n = w_bf16.shape[1]
    tn = GROUP_WIDTH
    nq = 3 * N_GROUPS * tn
    return pl.pallas_call(
        functools.partial(_proj_rows_kernel, tn=tn),
        out_shape=(jax.ShapeDtypeStruct((m, nq), out_dtype),
                   jax.ShapeDtypeStruct((m, n - nq), out_dtype)),
        grid=(1,),
        in_specs=[pl.BlockSpec((m, d), lambda i: (0, 0)), _resident((1, d)), _resident((d, n))],
        out_specs=(pl.BlockSpec((m, nq), lambda i: (0, 0)),
                   pl.BlockSpec((m, n - nq), lambda i: (0, 0))),
        compiler_params=pltpu.CompilerParams(dimension_semantics=("arbitrary",),
                                             vmem_limit_bytes=VMEM_LIMIT),
        name="proj_sample",
    )(x2d, g, w_bf16)


def _attn_kernel(q_ref, kc_ref, vc_ref, kp_ref, vp_ref, o_ref, l_ref, kcat, vcat, *, tq):
    i = pl.program_id(2)
    kcat[0:BAND, :] = kp_ref[...]
    kcat[BAND:, :] = kc_ref[...]
    vcat[0:BAND, :] = vp_ref[...]
    vcat[BAND:, :] = vc_ref[...]

    qi = lax.broadcasted_iota(jnp.int32, (BAND, 2 * BAND), 0)
    ki = lax.broadcasted_iota(jnp.int32, (BAND, 2 * BAND), 1)
    rel = qi + BAND - ki
    band_ok = (rel >= 0) & (rel <= BAND)
    lane = lax.broadcasted_iota(jnp.int32, (BAND, LANES), 1)
    low = lane < HEAD_DIM

    for qb in range(tq // BAND):
        r0 = qb * BAND
        if qb == 0:
            valid = band_ok & ((ki >= BAND) | (i > 0))
        else:
            valid = band_ok
        valid2 = jnp.concatenate([valid, valid], axis=0)
        for hp in range(GROUP_WIDTH // LANES):
            c0 = hp * LANES
            q2 = q_ref[r0:r0 + BAND, c0:c0 + LANES]
            k2 = kcat[r0:r0 + 2 * BAND, c0:c0 + LANES]
            v2 = vcat[r0:r0 + 2 * BAND, c0:c0 + LANES]
            zero = jnp.zeros_like(q2)
            qm = jnp.concatenate([jnp.where(low, q2, zero), jnp.where(low, zero, q2)], axis=0)
            s = lax.dot_general(qm, k2, (((1,), (1,)), ((), ())), preferred_element_type=F32)
            s = jnp.where(valid2, s, NEG)
            m = jnp.max(s, axis=-1, keepdims=True)
            p = jnp.exp2(s - m)
            den = jnp.sum(p, axis=-1, keepdims=True)
            pv = jnp.dot(p.astype(BF16), v2, preferred_element_type=F32)
            o = pv / den
            lse = (m + jnp.log2(den)) * LN2
            o_ref[r0:r0 + BAND, c0:c0 + LANES] = jnp.where(low, o[0:BAND], o[BAND:]).astype(o_ref.dtype)
            l_ref[r0:r0 + BAND, c0:c0 + LANES] = jnp.where(low, lse[0:BAND], lse[BAND:])


def _prompt_attention(qkv_g, g, *, tq):
    b, r, sr, _ = qkv_g.shape
    tq = min(tq, sr)
    per = tq // BAND

    def cur(j):
        return pl.BlockSpec((None, None, tq, GROUP_WIDTH), lambda bb, c, i: (bb, c, i, j))

    def prev(j):
        return pl.BlockSpec((None, None, BAND, GROUP_WIDTH),
                            lambda bb, c, i: (bb, c, jnp.maximum(i * per - 1, 0), j))

    out_spec = pl.BlockSpec((None, None, tq, GROUP_WIDTH), lambda bb, c, i: (bb, c, i, 0))
    return pl.pallas_call(
        functools.partial(_attn_kernel, tq=tq),
        out_shape=(jax.ShapeDtypeStruct((b, r, sr, GROUP_WIDTH), F32),
                   jax.ShapeDtypeStruct((b, r, sr, GROUP_WIDTH), F32)),
        grid=(b, r, sr // tq),
        in_specs=[cur(0), cur(1), cur(2), prev(1), prev(2)],
        out_specs=(out_spec, out_spec),
        scratch_shapes=[pltpu.VMEM((BAND + tq, GROUP_WIDTH), BF16),
                        pltpu.VMEM((BAND + tq, GROUP_WIDTH), BF16)],
        compiler_params=pltpu.CompilerParams(
            dimension_semantics=("arbitrary", "arbitrary", "arbitrary"),
            vmem_limit_bytes=VMEM_LIMIT),
        name=f"attn_g{g}",
    )(qkv_g, qkv_g, qkv_g, qkv_g, qkv_g)


def _merge_groups(os_, ls):
    m = jnp.maximum(jnp.maximum(ls[0], ls[1]), ls[2])
    es = [jnp.exp(l - m) for l in ls]
    num = es[0] * os_[0] + es[1] * os_[1] + es[2] * os_[2]
    return num / (es[0] + es[1] + es[2])


def _block_core(x, attn, c_b, conv_v, g_a, g_c, wao_ref, wco_ref, wo_ref, w1_ref, w2_ref,
                gpost, gpre, gfpost, *, ff_chunk):
    a = jnp.dot(attn.astype(BF16), wao_ref[...], preferred_element_type=F32)
    c = jnp.dot((c_b * conv_v).astype(BF16), wco_ref[...], preferred_element_type=F32)
    mix_in = jax.nn.sigmoid(g_a) * a + jax.nn.sigmoid(g_c) * c
    mix = jnp.dot(mix_in.astype(BF16), wo_ref[...], preferred_element_type=F32)
    x1 = x + _rms(mix, gpost)
    h2 = _rms(x1, gpre).astype(BF16)
    f = None
    for k0 in range(0, w1_ref.shape[1], ff_chunk):
        t = jnp.dot(h2, w1_ref[:, k0:k0 + ff_chunk], preferred_element_type=F32)
        t = jnp.square(jnp.maximum(t, 0.0)).astype(BF16)
        part = jnp.dot(t, w2_ref[k0:k0 + ff_chunk, :], preferred_element_type=F32)
        f = part if f is None else f + part
    return x1 + _rms(f, gfpost)


def _block_prompt_kernel(x_ref, rest_ref, hcc_ref, hch_ref, o0, l0, o1, l1, o2, l2,
                         convw_ref, wao_ref, wco_ref, wo_ref, w1_ref, w2_ref,
                         gpost_ref, gpre_ref, gfpost_ref, y_ref, tail_ref, u_scr, nat_scr,
                         *, ff_chunk):
    i = pl.program_id(1)
    tm, d = x_ref.shape
    vals = []
    for k, ref in enumerate((o0, l0, o1, l1, o2, l2)):
        r = ref.shape[0]
        if r == 1:
            vals.append(ref[0])
        else:
            nl = GROUP_WIDTH // LANES
            for c in range(r):
                for j in range(nl):
                    nat_scr[k * nl + j, pl.ds(c, tm // r, stride=r), :] = (
                        ref[c, :, j * LANES:(j + 1) * LANES])
            vals.append(jnp.concatenate([nat_scr[k * nl + j] for j in range(nl)], axis=1))
    attn = _merge_groups(vals[0::2], vals[1::2])
    c_b = rest_ref[:, 0:d].astype(F32)
    u = rest_ref[:, d:2 * d].astype(F32) * rest_ref[:, 2 * d:3 * d].astype(F32)
    g_a = rest_ref[:, 3 * d:4 * d].astype(F32)
    g_c = rest_ref[:, 4 * d:5 * d].astype(F32)
    hu = hcc_ref[...].astype(F32) * hch_ref[...].astype(F32)
    u_scr[0:SUBLANES, :] = jnp.where(i > 0, hu, jnp.zeros_like(hu))
    u_scr[SUBLANES:, :] = u
    w = convw_ref[...]
    conv_v = (w[0:1, :] * u_scr[SUBLANES - 2:SUBLANES - 2 + tm, :]
              + w[1:2, :] * u_scr[SUBLANES - 1:SUBLANES - 1 + tm, :]
              + w[2:3, :] * u)
    y_ref[...] = _block_core(x_ref[...], attn, c_b, conv_v, g_a, g_c,
                             wao_ref, wco_ref, wo_ref, w1_ref, w2_ref,
                             gpost_ref[...], gpre_ref[...], gfpost_ref[...], ff_chunk=ff_chunk)
    tail_ref[...] = u[tm - SUBLANES:tm, :]


def _block_sample_kernel(x_ref, rest_ref, st_ref, o0, l0, o1, l1, o2, l2,
                         convw_ref, wao_ref, wco_ref, wo_ref, w1_ref, w2_ref,
                         gpost_ref, gpre_ref, gfpost_ref, y_ref, u_ref, *, ff_chunk):
    d = x_ref.shape[1]
    attn = _merge_groups((o0[...], o1[...], o2[...]), (l0[...], l1[...], l2[...]))
    c_b = rest_ref[:, 0:d].astype(F32)
    u = rest_ref[:, d:2 * d].astype(F32) * rest_ref[:, 2 * d:3 * d].astype(F32)
    g_a = rest_ref[:, 3 * d:4 * d].astype(F32)
    g_c = rest_ref[:, 4 * d:5 * d].astype(F32)
    w = convw_ref[...]
    conv_v = w[0:1, :] * st_ref[:, 0:d] + w[1:2, :] * st_ref[:, d:2 * d] + w[2:3, :] * u
    y_ref[...] = _block_core(x_ref[...], attn, c_b, conv_v, g_a, g_c,
                             wao_ref, wco_ref, wo_ref, w1_ref, w2_ref,
                             gpost_ref[...], gpre_ref[...], gfpost_ref[...], ff_chunk=ff_chunk)
    u_ref[...] = u


def _weight_specs(ws):
    return [_resident(w.shape) for w in ws]


def _block_prompt(x3, rest3, os_, ls_, weights, *, tm, ff_chunk):
    b, s, d = x3.shape
    nr = rest3.shape[-1]
    per = tm // SUBLANES
    row = lambda width: pl.BlockSpec((None, tm, width), lambda bb, i: (bb, i, 0))
    halo = lambda col: pl.BlockSpec((None, SUBLANES, d),
                                    lambda bb, i: (bb, jnp.maximum(i * per - 1, 0), col))
    in_specs = [row(d), row(nr), halo(1), halo(2)]
    args = [x3, rest3, rest3, rest3]
    for o, l in zip(os_, ls_):
        r = o.shape[1]
        res = pl.BlockSpec((None, r, tm // r, GROUP_WIDTH), lambda bb, i: (bb, 0, i, 0))
        in_specs += [res, res]
        args += [o, l]
    in_specs += _weight_specs(weights)
    args += list(weights)
    y, tail = pl.pallas_call(
        functools.partial(_block_prompt_kernel, ff_chunk=ff_chunk),
        out_shape=(jax.ShapeDtypeStruct((b, s, d), F32),
                   jax.ShapeDtypeStruct((b, SUBLANES, d), F32)),
        grid=(b, s // tm),
        in_specs=in_specs,
        out_specs=(row(d), pl.BlockSpec((None, SUBLANES, d), lambda bb, i: (bb, 0, 0))),
        scratch_shapes=[pltpu.VMEM((SUBLANES + tm, d), F32),
                        pltpu.VMEM((2 * N_GROUPS * GROUP_WIDTH // LANES, tm, LANES), F32)],
        compiler_params=pltpu.CompilerParams(dimension_semantics=("arbitrary", "arbitrary"),
                                             vmem_limit_bytes=VMEM_LIMIT),
        name="block_prompt",
    )(*args)
    return y, tail


def _block_sample(x2, rest2, state2, os_, ls_, weights, *, ff_chunk):
    m, d = x2.shape
    full = lambda a: pl.BlockSpec(a.shape, lambda i: (0,) * a.ndim)
    args = [x2, rest2, state2]
    for o, l in zip(os_, ls_):
        args += [o, l]
    in_specs = [full(a) for a in args] + _weight_specs(weights)
    args += list(weights)
    return pl.pallas_call(
        functools.partial(_block_sample_kernel, ff_chunk=ff_chunk),
        out_shape=(jax.ShapeDtypeStruct((m, d), F32), jax.ShapeDtypeStruct((m, d), F32)),
        grid=(1,),
        in_specs=in_specs,
        out_specs=(pl.BlockSpec((m, d), lambda i: (0, 0)), pl.BlockSpec((m, d), lambda i: (0, 0))),
        compiler_params=pltpu.CompilerParams(dimension_semantics=("arbitrary",),
                                             vmem_limit_bytes=VMEM_LIMIT),
        name="block_sample",
    )(*args)


def _cache_unit(cols_ref, cache_ref, out_ref, o_ref, l_ref, seq, part, dil):
    _, nrows, length = cache_ref.shape
    scale = HEAD_DIM ** -0.5
    mine = lax.broadcasted_iota(jnp.int32, (1, LANES), 1) == seq
    base = pl.multiple_of(part * nrows, SUBLANES)

    def column(segment):
        blk = cols_ref[pl.ds(segment * GROUP_WIDTH + base, nrows), :]
        return jnp.sum(jnp.where(mine, blk, 0.0), axis=-1, keepdims=True)

    qc, knc, vnc = column(0), column(1), column(2)
    row = lax.broadcasted_iota(jnp.int32, (1, length), 1)
    dist = length - row
    on_grid = ((dist % dil) == 0) & (dist <= BAND * dil)
    for h in range(nrows // HEAD_DIM):
        rows = slice(h * HEAD_DIM, (h + 1) * HEAD_DIM)
        s = jnp.sum(cache_ref[0, rows, :] * qc[rows], axis=0, keepdims=True) * scale
        s = jnp.where(on_grid, s, NEG)
        s_new = jnp.sum(qc[rows] * knc[rows], axis=0, keepdims=True) * scale
        m = jnp.maximum(jnp.max(s, axis=-1, keepdims=True), s_new)
        p = jnp.exp(s - m)
        p_new = jnp.exp(s_new - m)
        den = jnp.sum(p, axis=-1, keepdims=True) + p_new
        acc = jnp.sum(cache_ref[1, rows, :] * p, axis=-1, keepdims=True) + p_new * vnc[rows]
        o_ref[rows, :] = jnp.broadcast_to(acc / den, (HEAD_DIM, LANES))
        l_ref[rows, :] = jnp.broadcast_to(m + jnp.log(den), (HEAD_DIM, LANES))

    last = row == length - 1
    for kv, newc in ((0, knc), (1, vnc)):
        rolled = pltpu.roll(cache_ref[kv], length - 1, axis=1)
        out_ref[kv] = jnp.where(last, newc, rolled)


def kernel(x_prompt, x_sample, cache_kv_w128, cache_kv_w512, cache_kv_w2048, state_conv, w_in, conv_w, w_attn_o, w_conv_o, w_o, w_ff1, w_ff2, g_mix_pre, g_mix_post, g_ffn_pre, g_ffn_post):
    depth = w_in.shape[0]
    b, s, d = x_prompt.shape
    db, ds_, _ = x_sample.shape
    assert ds_ == 1 and d % LANES == 0 and db <= LANES
    caches = (cache_kv_w128, cache_kv_w512, cache_kv_w2048)
    gw = GROUP_WIDTH

    yp, ys = x_prompt, x_sample.reshape(db, d)
    kv_p = [[] for _ in range(N_GROUPS)]
    kv_s = [[] for _ in range(N_GROUPS)]
    conv_p, conv_s = [], []
    for l in range(depth):
        w_in_b = w_in[l].astype(BF16)
        weights = (conv_w[l], w_attn_o[l].astype(BF16), w_conv_o[l].astype(BF16),
                   w_o[l].astype(BF16), w_ff1[l].astype(BF16), w_ff2[l].astype(BF16),
                   g_mix_post[l][None, :], g_ffn_pre[l][None, :], g_ffn_post[l][None, :])
        g_pre = g_mix_pre[l][None, :]

        nq = 3 * N_GROUPS * gw

        qkv_s, rest_s = _project_rows(ys, g_pre, w_in_b, out_dtype=F32)
        cols, caches_t = [], []
        for g in range(N_GROUPS):
            length = caches[g].shape[2]
            caches_t.append(jnp.transpose(caches[g][l], (0, 2, 3, 4, 1)).reshape(db, 2, gw, length))
            qkv_g = jnp.concatenate([qkv_s[:, (j * N_GROUPS + g) * gw:(j * N_GROUPS + g + 1) * gw]
                                     for j in range(3)], axis=1)
            cols.append(jnp.pad(qkv_g, ((0, LANES - db), (0, 0))).T)

        *qkv_groups, new2, so2, sl2 = _project_qkv(yp, g_pre, w_in_b[:, :nq], cols[2], caches_t[2],
                                                   tm=256, dil=DILATIONS[2])
        rest3, new0, so0, sl0, new1, so1, sl1 = _project_rest(
            yp, g_pre, w_in_b[:, nq:], cols[:2], caches_t[:2], tm=256, dils=DILATIONS[:2])
        sample_parts = ((new0, so0, sl0), (new1, so1, sl1), (new2, so2, sl2))
        os_, ls_ = [], []
        for g in range(N_GROUPS):
            o, lse = _prompt_attention(qkv_groups[g], g, tq=512)
            os_.append(o)
            ls_.append(lse)
            r = DILATIONS[g]
            w = min(WINDOWS[g], s)
            kv_tail = qkv_groups[g][:, :, s // r - w // r:, gw:3 * gw]
            kv_tail = jnp.transpose(kv_tail, (0, 2, 1, 3)).reshape(b, w, 2, HEADS, HEAD_DIM)
            kv_p[g].append(kv_tail.astype(F32))
        yp, tail = _block_prompt(yp, rest3, os_, ls_, weights, tm=256, ff_chunk=1024)
        conv_p.append(tail[:, SUBLANES - (CONV_WIDTH - 1):, :])

        os_, ls_ = [], []
        for g in range(N_GROUPS):
            new_t, so, sl = sample_parts[g]
            length = new_t.shape[-1]
            os_.append(so[:, :, 0].reshape(db, gw))
            ls_.append(sl[:, :, 0].reshape(db, gw))
            new = jnp.transpose(new_t.reshape(db, 2, HEADS, HEAD_DIM, length), (0, 4, 1, 2, 3))
            kv_s[g].append(new)
        state = state_conv[l]
        ys, u_s = _block_sample(ys, rest_s, state.reshape(db, (CONV_WIDTH - 1) * d), os_, ls_,
                                weights, ff_chunk=1024)
        conv_s.append(jnp.stack([state[:, 1, :], u_s], axis=1))

    stack = lambda xs: jnp.stack(xs, axis=0)
    return (yp, ys.reshape(db, 1, d),
            stack(kv_p[0]), stack(kv_p[1]), stack(kv_p[2]), stack(conv_p),
            stack(kv_s[0]), stack(kv_s[1]), stack(kv_s[2]), stack(conv_s))
```

```python
import functools

import jax
import jax.numpy as jnp
from jax import lax
from jax.experimental import pallas as pl
from jax.experimental.pallas import tpu as pltpu

HEAD_DIM = 64
HEADS = 8
GROUP_WIDTH = HEADS * HEAD_DIM
N_GROUPS = 3
WINDOWS = (128, 512, 2048)
DILATIONS = (1, 4, 16)
BAND = 128
CONV_WIDTH = 3
EPS = 1e-6
NEG = -1e30
LOG2E = 1.4426950408889634
LN2 = 0.6931471805599453
LANES = 128
SUBLANES = 8
VMEM_LIMIT = 56 * 1024 * 1024

F32 = jnp.float32
BF16 = jnp.bfloat16


def _rms(x, g):
    return (x * lax.rsqrt(jnp.mean(x * x, axis=-1, keepdims=True) + EPS)) * g


def _resident(shape):
    nd = len(shape)
    return pl.BlockSpec(shape, lambda *_: (0,) * nd, pipeline_mode=pl.Buffered(1))


def _unit_index(units_per_seq):
    u = pl.program_id(0) * pl.num_programs(1) + pl.program_id(1)
    return u // units_per_seq, u % units_per_seq


def _proj_qkv_kernel(x_ref, g_ref, w_ref, cols_ref, cache_ref,
                     q0_ref, q1_ref, q2_ref, ra_ref, new_ref, so_ref, sl_ref, h_scr, hb_scr,
                     *, tn, dil, units_per_seq):
    tm = x_ref.shape[0]
    h = _rms(x_ref[...], g_ref[...])
    hb_scr[0] = h.astype(BF16)
    for j in range(h_scr.shape[0]):
        h_scr[j] = h[:, j * LANES:(j + 1) * LANES]
    for g in range(1, N_GROUPS):
        r = DILATIONS[g]
        for c in range(r):
            for j in range(h_scr.shape[0]):
                hb_scr[g, c * (tm // r):(c + 1) * (tm // r), j * LANES:(j + 1) * LANES] = (
                    h_scr[j, pl.ds(c, tm // r, stride=r), :].astype(BF16))
    for g, out in enumerate((q0_ref, q1_ref, q2_ref)):
        r = DILATIONS[g]
        for j in range(3):
            c0 = (j * N_GROUPS + g) * tn
            res = jnp.dot(hb_scr[g], w_ref[:, c0:c0 + tn], preferred_element_type=F32)
            if j == 0:
                res = res * (HEAD_DIM ** -0.5 * LOG2E)
            out[:, :, j * tn:(j + 1) * tn] = res.reshape(r, tm // r, tn).astype(out.dtype)
    nq = 3 * N_GROUPS * tn
    for c0 in range(nq, w_ref.shape[-1], tn):
        res = jnp.dot(hb_scr[0], w_ref[:, c0:c0 + tn], preferred_element_type=F32)
        ra_ref[:, c0 - nq:c0 - nq + tn] = res.astype(ra_ref.dtype)
    seq, part = _unit_index(units_per_seq)
    _cache_unit(cols_ref, cache_ref, new_ref, so_ref, sl_ref, seq, part, dil)


def _proj_rest_kernel(x_ref, g_ref, w_ref, cols0_ref, cache0_ref, cols1_ref, cache1_ref,
                      rest_ref, new0_ref, so0_ref, sl0_ref, new1_ref, so1_ref, sl1_ref,
                      *, tn, dils, units_per_seq):
    h = _rms(x_ref[...], g_ref[...]).astype(BF16)
    for c0 in range(0, w_ref.shape[-1], tn):
        res = jnp.dot(h, w_ref[:, c0:c0 + tn], preferred_element_type=F32)
        rest_ref[:, c0:c0 + tn] = res.astype(rest_ref.dtype)
    seq, part = _unit_index(units_per_seq)
    _cache_unit(cols0_ref, cache0_ref, new0_ref, so0_ref, sl0_ref, seq, part, dils[0])
    _cache_unit(cols1_ref, cache1_ref, new1_ref, so1_ref, sl1_ref, seq, part, dils[1])


def _cache_specs(cache_t, steps, n_i):
    db, _, gw, length = cache_t.shape
    ups = steps // db
    assert steps == db * ups and gw % ups == 0 and (gw // ups) % HEAD_DIM == 0, (steps, db)
    rows = gw // ups
    unit = lambda bb, i: bb * n_i + i
    blk = pl.BlockSpec((None, 2, rows, length),
                       lambda bb, i: (unit(bb, i) // ups, 0, unit(bb, i) % ups, 0))
    stat = pl.BlockSpec((None, rows, LANES), lambda bb, i: (unit(bb, i), 0, 0))
    stat_shape = jax.ShapeDtypeStruct((steps, rows, LANES), F32)
    return ups, blk, stat, stat_shape


def _project_qkv(x3, g, w_qkv, cols, cache_t, *, tm, dil):
    b, s, d = x3.shape
    tn = GROUP_WIDTH
    n_i = s // tm
    extra = w_qkv.shape[1] - 3 * N_GROUPS * tn
    ups, blk, stat, stat_shape = _cache_specs(cache_t, b * n_i, n_i)
    grp_shape = lambda r: jax.ShapeDtypeStruct((b, r, s // r, 3 * tn), BF16)
    grp_spec = lambda r: pl.BlockSpec((None, r, tm // r, 3 * tn), lambda bb, i: (bb, 0, i, 0))
    return pl.pallas_call(
        functools.partial(_proj_qkv_kernel, tn=tn, dil=dil, units_per_seq=ups),
        out_shape=tuple(grp_shape(r) for r in DILATIONS)
        + (jax.ShapeDtypeStruct((b, s, extra), BF16),
           jax.ShapeDtypeStruct(cache_t.shape, cache_t.dtype), stat_shape, stat_shape),
        grid=(b, n_i),
        in_specs=[pl.BlockSpec((None, tm, d), lambda bb, i: (bb, i, 0)),
                  _resident((1, d)), _resident(w_qkv.shape), _resident(cols.shape), blk],
        out_specs=tuple(grp_spec(r) for r in DILATIONS)
        + (pl.BlockSpec((None, tm, extra), lambda bb, i: (bb, i, 0)), blk, stat, stat),
        scratch_shapes=[pltpu.VMEM((d // LANES, tm, LANES), F32),
                        pltpu.VMEM((N_GROUPS, tm, d), BF16)],
        compiler_params=pltpu.CompilerParams(dimension_semantics=("arbitrary", "arbitrary"),
                                             vmem_limit_bytes=VMEM_LIMIT),
        name="proj_qkv",
    )(x3, g, w_qkv, cols, cache_t)


def _project_rest(x3, g, w_rest, cols01, caches01, *, tm, dils):
    b, s, d = x3.shape
    n = w_rest.shape[1]
    n_i = s // tm
    in_specs = [pl.BlockSpec((None, tm, d), lambda bb, i: (bb, i, 0)),
                _resident((1, d)), _resident(w_rest.shape)]
    out_shape = [jax.ShapeDtypeStruct((b, s, n), BF16)]
    out_specs = [pl.BlockSpec((None, tm, n), lambda bb, i: (bb, i, 0))]
    args = [x3, g, w_rest]
    for cols, cache_t in zip(cols01, caches01):
        ups, blk, stat, stat_shape = _cache_specs(cache_t, b * n_i, n_i)
        in_specs += [_resident(cols.shape), blk]
        args += [cols, cache_t]
        out_shape += [jax.ShapeDtypeStruct(cache_t.shape, cache_t.dtype), stat_shape, stat_shape]
        out_specs += [blk, stat, stat]
    return pl.pallas_call(
        functools.partial(_proj_rest_kernel, tn=GROUP_WIDTH, dils=dils, units_per_seq=ups),
        out_shape=tuple(out_shape),
        grid=(b, n_i),
        in_specs=in_specs,
        out_specs=tuple(out_specs),
        compiler_params=pltpu.CompilerParams(dimension_semantics=("arbitrary", "arbitrary"),
                                             vmem_limit_bytes=VMEM_LIMIT),
        name="proj_rest",
    )(*args)


def _proj_rows_kernel(x_ref, g_ref, wq_ref, wr_ref, qkv_ref, rest_ref, *, tn):
    h = _rms(x_ref[...], g_ref[...]).astype(BF16)
    for w_ref, out in ((wq_ref, qkv_ref), (wr_ref, rest_ref)):
        for c0 in range(0, w_ref.shape[-1], tn):
            res = jnp.dot(h, w_ref[:, c0:c0 + tn], preferred_element_type=F32)
            out[:, c0:c0 + tn] = res.astype(out.dtype)


def _project_rows(x2d, g, w_qkv, w_rest, *, out_dtype):
    m, d = x2d.shape
    whole = lambda a: pl.BlockSpec(a.shape, lambda i: (0,) * a.ndim)
    out_shape = (jax.ShapeDtypeStruct((m, w_qkv.shape[1]), out_dtype),
                 jax.ShapeDtypeStruct((m, w_rest.shape[1]), out_dtype))
    return pl.pallas_call(
        functools.partial(_proj_rows_kernel, tn=GROUP_WIDTH),
        out_shape=out_shape,
        grid=(1,),
        in_specs=[whole(x2d), _resident((1, d)), _resident(w_qkv.shape), _resident(w_rest.shape)],
        out_specs=tuple(whole(o) for o in out_shape),
        compiler_params=pltpu.CompilerParams(dimension_semantics=("arbitrary",),
                                             vmem_limit_bytes=VMEM_LIMIT),
        name="proj_sample",
    )(x2d, g, w_qkv, w_rest)


def _attn_kernel(q_ref, kc_ref, vc_ref, kp_ref, vp_ref, o_ref, l_ref, kcat, vcat, *, tq):
    i = pl.program_id(2)
    kcat[0:BAND, :] = kp_ref[...]
    kcat[BAND:, :] = kc_ref[...]
    vcat[0:BAND, :] = vp_ref[...]
    vcat[BAND:, :] = vc_ref[...]

    qi = lax.broadcasted_iota(jnp.int32, (BAND, 2 * BAND), 0)
    ki = lax.broadcasted_iota(jnp.int32, (BAND, 2 * BAND), 1)
    rel = qi + BAND - ki
    band_ok = (rel >= 0) & (rel <= BAND)
    lane = lax.broadcasted_iota(jnp.int32, (BAND, LANES), 1)
    low = lane < HEAD_DIM

    for qb in range(tq // BAND):
        r0 = qb * BAND
        if qb == 0:
            valid = band_ok & ((ki >= BAND) | (i > 0))
        else:
            valid = band_ok
        valid2 = jnp.concatenate([valid, valid], axis=0)
        for hp in range(GROUP_WIDTH // LANES):
            c0 = hp * LANES
            q2 = q_ref[r0:r0 + BAND, c0:c0 + LANES]
            k2 = kcat[r0:r0 + 2 * BAND, c0:c0 + LANES]
            v2 = vcat[r0:r0 + 2 * BAND, c0:c0 + LANES]
            zero = jnp.zeros_like(q2)
            qm = jnp.concatenate([jnp.where(low, q2, zero), jnp.where(low, zero, q2)], axis=0)
            s = lax.dot_general(qm, k2, (((1,), (1,)), ((), ())), preferred_element_type=F32)
            s = jnp.where(valid2, s, NEG)
            m = jnp.max(s, axis=-1, keepdims=True)
            p = jnp.exp2(s - m)
            den = jnp.sum(p, axis=-1, keepdims=True)
            pv = jnp.dot(p.astype(BF16), v2, preferred_element_type=F32)
            o = pv * (1.0 / den)
            lse = (m + jnp.log2(den)) * LN2
            o_ref[r0:r0 + BAND, c0:c0 + LANES] = jnp.where(low, o[0:BAND], o[BAND:]).astype(o_ref.dtype)
            l_ref[r0:r0 + BAND, c0:c0 + LANES] = jnp.where(low, lse[0:BAND], lse[BAND:])


def _prompt_attention(qkv_g, g, *, tq):
    b, r, sr, _ = qkv_g.shape
    tq = min(tq, sr)
    per = tq // BAND

    def cur(j):
        return pl.BlockSpec((None, None, tq, GROUP_WIDTH), lambda bb, c, i: (bb, c, i, j))

    def prev(j):
        return pl.BlockSpec((None, None, BAND, GROUP_WIDTH),
                            lambda bb, c, i: (bb, c, jnp.maximum(i * per - 1, 0), j))

    out_spec = pl.BlockSpec((None, None, tq, GROUP_WIDTH), lambda bb, c, i: (bb, c, i, 0))
    return pl.pallas_call(
        functools.partial(_attn_kernel, tq=tq),
        out_shape=(jax.ShapeDtypeStruct((b, r, sr, GROUP_WIDTH), F32),
                   jax.ShapeDtypeStruct((b, r, sr, GROUP_WIDTH), F32)),
        grid=(b, r, sr // tq),
        in_specs=[cur(0), cur(1), cur(2), prev(1), prev(2)],
        out_specs=(out_spec, out_spec),
        scratch_shapes=[pltpu.VMEM((BAND + tq, GROUP_WIDTH), BF16),
                        pltpu.VMEM((BAND + tq, GROUP_WIDTH), BF16)],
        compiler_params=pltpu.CompilerParams(
            dimension_semantics=("arbitrary", "arbitrary", "arbitrary"),
            vmem_limit_bytes=VMEM_LIMIT),
        name=f"attn_g{g}",
    )(qkv_g, qkv_g, qkv_g, qkv_g, qkv_g)


def _merge_groups(os_, ls):
    m = jnp.maximum(jnp.maximum(ls[0], ls[1]), ls[2])
    es = [jnp.exp(l - m) for l in ls]
    num = es[0] * os_[0] + es[1] * os_[1] + es[2] * os_[2]
    return num / (es[0] + es[1] + es[2])


def _block_core(x, attn, c_b, conv_v, g_a, g_c, wao_ref, wco_ref, wo_ref, w1_ref, w2_ref,
                gpost, gpre, gfpost, *, ff_chunk):
    a = jnp.dot(attn.astype(BF16), wao_ref[...], preferred_element_type=F32)
    c = jnp.dot((c_b * conv_v).astype(BF16), wco_ref[...], preferred_element_type=F32)
    mix_in = jax.nn.sigmoid(g_a) * a + jax.nn.sigmoid(g_c) * c
    mix = jnp.dot(mix_in.astype(BF16), wo_ref[...], preferred_element_type=F32)
    x1 = x + _rms(mix, gpost)
    h2 = _rms(x1, gpre).astype(BF16)
    f = None
    for k0 in range(0, w1_ref.shape[1], ff_chunk):
        t = jnp.dot(h2, w1_ref[:, k0:k0 + ff_chunk], preferred_element_type=F32)
        t = jnp.square(jnp.maximum(t, 0.0)).astype(BF16)
        part = jnp.dot(t, w2_ref[k0:k0 + ff_chunk, :], preferred_element_type=F32)
        f = part if f is None else f + part
    return x1 + _rms(f, gfpost)


def _rest_columns(ra_ref, rb_ref, d):
    a0 = ra_ref.shape[-1] - 2 * d
    c_b = ra_ref[:, a0:a0 + d].astype(F32)
    u = ra_ref[:, a0 + d:a0 + 2 * d].astype(F32) * rb_ref[:, 0:d].astype(F32)
    g_a = rb_ref[:, d:2 * d].astype(F32)
    g_c = rb_ref[:, 2 * d:3 * d].astype(F32)
    return c_b, u, g_a, g_c


def _block_prompt_kernel(x_ref, ra_ref, rb_ref, hcc_ref, hch_ref, o0, l0, o1, l1, o2, l2,
                         convw_ref, wao_ref, wco_ref, wo_ref, w1_ref, w2_ref,
                         gpost_ref, gpre_ref, gfpost_ref, y_ref, tail_ref,
                         u_scr, nat_scr, *, ff_chunk):
    i = pl.program_id(1)
    tm, d = x_ref.shape
    vals = []
    for k, ref in enumerate((o0, l0, o1, l1, o2, l2)):
        r = ref.shape[0]
        if r == 1:
            vals.append(ref[0])
        else:
            nl = GROUP_WIDTH // LANES
            for c in range(r):
                for j in range(nl):
                    nat_scr[k * nl + j, pl.ds(c, tm // r, stride=r), :] = (
                        ref[c, :, j * LANES:(j + 1) * LANES])
            vals.append(jnp.concatenate([nat_scr[k * nl + j] for j in range(nl)], axis=1))
    attn = _merge_groups(vals[0::2], vals[1::2])
    c_b, u, g_a, g_c = _rest_columns(ra_ref, rb_ref, d)
    hu = hcc_ref[...].astype(F32) * hch_ref[...].astype(F32)
    u_scr[0:SUBLANES, :] = jnp.where(i > 0, hu, jnp.zeros_like(hu))
    u_scr[SUBLANES:, :] = u
    w = convw_ref[...]
    conv_v = (w[0:1, :] * u_scr[SUBLANES - 2:SUBLANES - 2 + tm, :]
              + w[1:2, :] * u_scr[SUBLANES - 1:SUBLANES - 1 + tm, :]
              + w[2:3, :] * u)
    y_ref[...] = _block_core(x_ref[...], attn, c_b, conv_v, g_a, g_c,
                             wao_ref, wco_ref, wo_ref, w1_ref, w2_ref,
                             gpost_ref[...], gpre_ref[...], gfpost_ref[...], ff_chunk=ff_chunk)
    tail_ref[...] = u[tm - SUBLANES:tm, :]


def _block_sample_kernel(x_ref, ra_ref, rb_ref, st_ref, o0, l0, o1, l1, o2, l2,
                         convw_ref, wao_ref, wco_ref, wo_ref, w1_ref, w2_ref,
                         gpost_ref, gpre_ref, gfpost_ref, y_ref, u_ref, *, ff_chunk):
    d = x_ref.shape[1]
    attn = _merge_groups((o0[...], o1[...], o2[...]), (l0[...], l1[...], l2[...]))
    c_b, u, g_a, g_c = _rest_columns(ra_ref, rb_ref, d)
    w = convw_ref[...]
    conv_v = w[0:1, :] * st_ref[:, 0:d] + w[1:2, :] * st_ref[:, d:2 * d] + w[2:3, :] * u
    y_ref[...] = _block_core(x_ref[...], attn, c_b, conv_v, g_a, g_c,
                             wao_ref, wco_ref, wo_ref, w1_ref, w2_ref,
                             gpost_ref[...], gpre_ref[...], gfpost_ref[...], ff_chunk=ff_chunk)
    u_ref[...] = u


def _weight_specs(ws):
    return [_resident(w.shape) for w in ws]


def _block_prompt(x3, ra3, rb3, os_, ls_, weights, *, tm, ff_chunk):
    b, s, d = x3.shape
    assert ra3.shape[-1] == 2 * d and rb3.shape[-1] == 3 * d
    per = tm // SUBLANES
    row = lambda width: pl.BlockSpec((None, tm, width), lambda bb, i: (bb, i, 0))
    halo = lambda col: pl.BlockSpec((None, SUBLANES, d),
                                    lambda bb, i: (bb, jnp.maximum(i * per - 1, 0), col))
    in_specs = [row(d), row(2 * d), row(3 * d), halo(1), halo(0)]
    args = [x3, ra3, rb3, ra3, rb3]
    for o, l in zip(os_, ls_):
        r = o.shape[1]
        res = pl.BlockSpec((None, r, tm // r, GROUP_WIDTH), lambda bb, i: (bb, 0, i, 0))
        in_specs += [res, res]
        args += [o, l]
    in_specs += _weight_specs(weights)
    args += list(weights)
    n_slabs = 2 * N_GROUPS * GROUP_WIDTH // LANES
    y, tail = pl.pallas_call(
        functools.partial(_block_prompt_kernel, ff_chunk=ff_chunk),
        out_shape=(jax.ShapeDtypeStruct((b, s, d), F32),
                   jax.ShapeDtypeStruct((b, SUBLANES, d), F32)),
        grid=(b, s // tm),
        in_specs=in_specs,
        out_specs=(row(d), pl.BlockSpec((None, SUBLANES, d), lambda bb, i: (bb, 0, 0))),
        scratch_shapes=[pltpu.VMEM((SUBLANES + tm, d), F32),
                        pltpu.VMEM((n_slabs, tm, LANES), F32)],
        compiler_params=pltpu.CompilerParams(dimension_semantics=("arbitrary", "arbitrary"),
                                             vmem_limit_bytes=VMEM_LIMIT),
        name="block_prompt",
    )(*args)
    return y, tail


def _block_sample(x2, ra2, rb2, state2, os_, ls_, weights, *, ff_chunk):
    m, d = x2.shape
    full = lambda a: pl.BlockSpec(a.shape, lambda i: (0,) * a.ndim)
    args = [x2, ra2, rb2, state2]
    for o, l in zip(os_, ls_):
        args += [o, l]
    in_specs = [full(a) for a in args] + _weight_specs(weights)
    args += list(weights)
    return pl.pallas_call(
        functools.partial(_block_sample_kernel, ff_chunk=ff_chunk),
        out_shape=(jax.ShapeDtypeStruct((m, d), F32), jax.ShapeDtypeStruct((m, d), F32)),
        grid=(1,),
        in_specs=in_specs,
        out_specs=(pl.BlockSpec((m, d), lambda i: (0, 0)), pl.BlockSpec((m, d), lambda i: (0, 0))),
        compiler_params=pltpu.CompilerParams(dimension_semantics=("arbitrary",),
                                             vmem_limit_bytes=VMEM_LIMIT),
        name="block_sample",
    )(*args)


def _cache_unit(cols_ref, cache_ref, out_ref, o_ref, l_ref, seq, part, dil):
    _, nrows, length = cache_ref.shape
    scale = HEAD_DIM ** -0.5
    mine = lax.broadcasted_iota(jnp.int32, (1, LANES), 1) == seq
    base = pl.multiple_of(part * nrows, SUBLANES)

    def column(segment):
        blk = cols_ref[pl.ds(segment * GROUP_WIDTH + base, nrows), :]
        return jnp.sum(jnp.where(mine, blk, 0.0), axis=-1, keepdims=True)

    qc, knc, vnc = column(0), column(1), column(2)
    row = lax.broadcasted_iota(jnp.int32, (1, length), 1)
    dist = length - row
    on_grid = ((dist % dil) == 0) & (dist <= BAND * dil)
    for h in range(nrows // HEAD_DIM):
        rows = slice(h * HEAD_DIM, (h + 1) * HEAD_DIM)
        s = jnp.sum(cache_ref[0, rows, :] * qc[rows], axis=0, keepdims=True) * scale
        s = jnp.where(on_grid, s, NEG)
        s_new = jnp.sum(qc[rows] * knc[rows], axis=0, keepdims=True) * scale
        m = jnp.maximum(jnp.max(s, axis=-1, keepdims=True), s_new)
        p = jnp.exp(s - m)
        p_new = jnp.exp(s_new - m)
        den = jnp.sum(p, axis=-1, keepdims=True) + p_new
        acc = jnp.sum(cache_ref[1, rows, :] * p, axis=-1, keepdims=True) + p_new * vnc[rows]
        o_ref[rows, :] = jnp.broadcast_to(acc / den, (HEAD_DIM, LANES))
        l_ref[rows, :] = jnp.broadcast_to(m + jnp.log(den), (HEAD_DIM, LANES))

    last = row == length - 1
    for kv, newc in ((0, knc), (1, vnc)):
        rolled = pltpu.roll(cache_ref[kv], length - 1, axis=1)
        out_ref[kv] = jnp.where(last, newc, rolled)


def kernel(x_prompt, x_sample, cache_kv_w128, cache_kv_w512, cache_kv_w2048, state_conv, w_in, conv_w, w_attn_o, w_conv_o, w_o, w_ff1, w_ff2, g_mix_pre, g_mix_post, g_ffn_pre, g_ffn_post):
    depth = w_in.shape[0]
    b, s, d = x_prompt.shape
    db, ds_, _ = x_sample.shape
    assert ds_ == 1 and d % LANES == 0 and db <= LANES
    caches = (cache_kv_w128, cache_kv_w512, cache_kv_w2048)
    gw = GROUP_WIDTH

    yp, ys = x_prompt, x_sample.reshape(db, d)
    kv_p = [[] for _ in range(N_GROUPS)]
    kv_s = [[] for _ in range(N_GROUPS)]
    conv_p, conv_s = [], []
    for l in range(depth):
        na = 3 * N_GROUPS * gw + 2 * d
        w_qkv = w_in[l][:, :na].astype(BF16)
        w_rest = w_in[l][:, na:].astype(BF16)
        weights = (conv_w[l], w_attn_o[l].astype(BF16), w_conv_o[l].astype(BF16),
                   w_o[l].astype(BF16), w_ff1[l].astype(BF16), w_ff2[l].astype(BF16),
                   g_mix_post[l][None, :], g_ffn_pre[l][None, :], g_ffn_post[l][None, :])
        g_pre = g_mix_pre[l][None, :]

        qkv_s, rest_s = _project_rows(ys, g_pre, w_qkv, w_rest, out_dtype=F32)
        cols, caches_t = [], []
        for g in range(N_GROUPS):
            length = caches[g].shape[2]
            caches_t.append(jnp.transpose(caches[g][l], (0, 2, 3, 4, 1)).reshape(db, 2, gw, length))
            qkv_g = jnp.concatenate([qkv_s[:, (j * N_GROUPS + g) * gw:(j * N_GROUPS + g + 1) * gw]
                                     for j in range(3)], axis=1)
            cols.append(jnp.pad(qkv_g, ((0, LANES - db), (0, 0))).T)

        *qkv_groups, ra3, new2, so2, sl2 = _project_qkv(yp, g_pre, w_qkv, cols[2], caches_t[2],
                                                        tm=256, dil=DILATIONS[2])
        rb3, new0, so0, sl0, new1, so1, sl1 = _project_rest(
            yp, g_pre, w_rest, cols[:2], caches_t[:2], tm=256, dils=DILATIONS[:2])
        sample_parts = ((new0, so0, sl0), (new1, so1, sl1), (new2, so2, sl2))
        os_, ls_ = [], []
        for g in range(N_GROUPS):
            o, lse = _prompt_attention(qkv_groups[g], g, tq=512)
            os_.append(o)
            ls_.append(lse)
            r = DILATIONS[g]
            w = min(WINDOWS[g], s)
            kv_tail = qkv_groups[g][:, :, s // r - w // r:, gw:3 * gw]
            kv_tail = jnp.transpose(kv_tail, (0, 2, 1, 3)).reshape(b, w, 2, HEADS, HEAD_DIM)
            kv_p[g].append(kv_tail.astype(F32))
        yp, tail = _block_prompt(yp, ra3, rb3, os_, ls_, weights, tm=256, ff_chunk=1024)
        conv_p.append(tail[:, SUBLANES - (CONV_WIDTH - 1):, :])

        os_, ls_ = [], []
        for g in range(N_GROUPS):
            new_t, so, sl = sample_parts[g]
            length = new_t.shape[-1]
            os_.append(so[:, :, 0].reshape(db, gw))
            ls_.append(sl[:, :, 0].reshape(db, gw))
            new = jnp.transpose(new_t.reshape(db, 2, HEADS, HEAD_DIM, length), (0, 4, 1, 2, 3))
            kv_s[g].append(new)
        state = state_conv[l]
        ys, u_s = _block_sample(ys, qkv_s, rest_s, state.reshape(db, (CONV_WIDTH - 1) * d),
                                os_, ls_, weights, ff_chunk=1024)
        conv_s.append(jnp.stack([state[:, 1, :], u_s], axis=1))

    stack = lambda xs: jnp.stack(xs, axis=0)
    return (yp, ys.reshape(db, 1, d),
            stack(kv_p[0]), stack(kv_p[1]), stack(kv_p[2]), stack(conv_p),
            stack(kv_s[0]), stack(kv_s[1]), stack(kv_s[2]), stack(conv_s))
```

```python
import functools

import jax
import jax.numpy as jnp
from jax import lax
from jax.experimental import pallas as pl
from jax.experimental.pallas import tpu as pltpu

HEAD_DIM = 64
HEADS = 8
GROUP_WIDTH = HEADS * HEAD_DIM
N_GROUPS = 3
WINDOWS = (128, 512, 2048)
DILATIONS = (1, 4, 16)
BAND = 128
CONV_WIDTH = 3
EPS = 1e-6
NEG = -1e30
LOG2E = 1.4426950408889634
LN2 = 0.6931471805599453
LANES = 128
SUBLANES = 8
VMEM_LIMIT = 56 * 1024 * 1024

F32 = jnp.float32
BF16 = jnp.bfloat16


def _rms(x, g):
    return (x * lax.rsqrt(jnp.mean(x * x, axis=-1, keepdims=True) + EPS)) * g


def _resident(shape):
    nd = len(shape)
    return pl.BlockSpec(shape, lambda *_: (0,) * nd, pipeline_mode=pl.Buffered(1))


def _unit_index(units_per_seq):
    u = pl.program_id(0) * pl.num_programs(1) + pl.program_id(1)
    return u // units_per_seq, u % units_per_seq


def _proj_qkv_kernel(x_ref, g_ref, w_ref, cols_ref, cache_ref,
                     q0_ref, q1_ref, q2_ref, ra_ref, t0_ref, t1_ref, t2_ref,
                     new_ref, so_ref, sl_ref, h_scr, hb_scr,
                     *, tn, dil, units_per_seq, tail_tiles):
    tm = x_ref.shape[0]
    h = _rms(x_ref[...], g_ref[...])
    hb_scr[0] = h.astype(BF16)
    for j in range(h_scr.shape[0]):
        h_scr[j] = h[:, j * LANES:(j + 1) * LANES]
    for g in range(1, N_GROUPS):
        r = DILATIONS[g]
        for c in range(r):
            for j in range(h_scr.shape[0]):
                hb_scr[g, c * (tm // r):(c + 1) * (tm // r), j * LANES:(j + 1) * LANES] = (
                    h_scr[j, pl.ds(c, tm // r, stride=r), :].astype(BF16))
    kv_f32 = {}
    for g, out in enumerate((q0_ref, q1_ref, q2_ref)):
        r = DILATIONS[g]
        for j in range(3):
            c0 = (j * N_GROUPS + g) * tn
            res = jnp.dot(hb_scr[g], w_ref[:, c0:c0 + tn], preferred_element_type=F32)
            if j == 0:
                res = res * (HEAD_DIM ** -0.5 * LOG2E)
            else:
                kv_f32[g, j] = res
            out[:, :, j * tn:(j + 1) * tn] = res.reshape(r, tm // r, tn).astype(out.dtype)
    nq = 3 * N_GROUPS * tn
    for c0 in range(nq, w_ref.shape[-1], tn):
        res = jnp.dot(hb_scr[0], w_ref[:, c0:c0 + tn], preferred_element_type=F32)
        ra_ref[:, c0 - nq:c0 - nq + tn] = res.astype(ra_ref.dtype)
    for g, out in enumerate((t0_ref, t1_ref, t2_ref)):
        lanes = out.shape[-1]
        r = DILATIONS[g]

        @pl.when(pl.program_id(1) >= pl.num_programs(1) - tail_tiles[g])
        def _():
            for j in (1, 2):
                res = kv_f32[g, j]
                if r > 1:
                    for c in range(r):
                        for n in range(tn // LANES):
                            h_scr[n, pl.ds(c, tm // r, stride=r), :] = (
                                res[c * (tm // r):(c + 1) * (tm // r), n * LANES:(n + 1) * LANES])
                    res = jnp.concatenate([h_scr[n] for n in range(tn // LANES)], axis=1)
                out[j - 1] = res[tm - lanes:, :].T
    seq, part = _unit_index(units_per_seq)
    _cache_unit(cols_ref, cache_ref, new_ref, so_ref, sl_ref, seq, part, dil)


def _proj_rest_kernel(x_ref, g_ref, *refs, tn, n_w, dils, units_per_seq):
    w_refs = refs[:n_w]
    cols0_ref, cache0_ref, cols1_ref, cache1_ref = refs[n_w:n_w + 4]
    rest_ref, new0_ref, so0_ref, sl0_ref, new1_ref, so1_ref, sl1_ref = refs[n_w + 4:]
    h = _rms(x_ref[...], g_ref[...]).astype(BF16)
    for k, w_ref in enumerate(w_refs):
        res = jnp.dot(h, w_ref[...], preferred_element_type=F32)
        rest_ref[:, k * tn:(k + 1) * tn] = res.astype(rest_ref.dtype)
    seq, part = _unit_index(units_per_seq)
    _cache_unit(cols0_ref, cache0_ref, new0_ref, so0_ref, sl0_ref, seq, part, dils[0])
    _cache_unit(cols1_ref, cache1_ref, new1_ref, so1_ref, sl1_ref, seq, part, dils[1])


def _cache_specs(cache_t, steps, n_i):
    db, _, gw, length = cache_t.shape
    ups = steps // db
    assert steps == db * ups and gw % ups == 0 and (gw // ups) % HEAD_DIM == 0, (steps, db)
    rows = gw // ups
    unit = lambda bb, i: bb * n_i + i
    blk = pl.BlockSpec((None, 2, rows, length),
                       lambda bb, i: (unit(bb, i) // ups, 0, unit(bb, i) % ups, 0))
    stat = pl.BlockSpec((None, SUBLANES, rows), lambda bb, i: (unit(bb, i), 0, 0))
    stat_shape = jax.ShapeDtypeStruct((steps, SUBLANES, rows), F32)
    return ups, blk, stat, stat_shape


def _project_qkv(x3, g, w, n_cols, cols, cache_t, *, tm, dil):
    b, s, d = x3.shape
    tn = GROUP_WIDTH
    n_i = s // tm
    extra = n_cols - 3 * N_GROUPS * tn
    w_spec = pl.BlockSpec((d, n_cols), lambda bb, i: (0, 0), pipeline_mode=pl.Buffered(1))
    ups, blk, stat, stat_shape = _cache_specs(cache_t, b * n_i, n_i)
    grp_shape = lambda r: jax.ShapeDtypeStruct((b, r, s // r, 3 * tn), BF16)
    grp_spec = lambda r: pl.BlockSpec((None, r, tm // r, 3 * tn), lambda bb, i: (bb, 0, i, 0))
    tails = [min(w, s) for w in WINDOWS]
    tail_tiles = tuple(max(w // tm, 1) for w in tails)
    assert all(w % min(tm, w) == 0 and min(tm, w) % LANES == 0 for w in tails)
    tail_shape = lambda w: jax.ShapeDtypeStruct((b, 2, tn, w), F32)
    tail_spec = lambda w, nt: pl.BlockSpec(
        (None, 2, tn, min(tm, w)), lambda bb, i: (bb, 0, 0, jnp.maximum(i - (n_i - nt), 0)))
    return pl.pallas_call(
        functools.partial(_proj_qkv_kernel, tn=tn, dil=dil, units_per_seq=ups,
                          tail_tiles=tail_tiles),
        out_shape=tuple(grp_shape(r) for r in DILATIONS)
        + (jax.ShapeDtypeStruct((b, s, extra), BF16),)
        + tuple(tail_shape(w) for w in tails)
        + (jax.ShapeDtypeStruct(cache_t.shape, cache_t.dtype), stat_shape, stat_shape),
        grid=(b, n_i),
        in_specs=[pl.BlockSpec((None, tm, d), lambda bb, i: (bb, i, 0)),
                  _resident((1, d)), w_spec, _resident(cols.shape), blk],
        out_specs=tuple(grp_spec(r) for r in DILATIONS)
        + (pl.BlockSpec((None, tm, extra), lambda bb, i: (bb, i, 0)),)
        + tuple(tail_spec(w, nt) for w, nt in zip(tails, tail_tiles))
        + (blk, stat, stat),
        scratch_shapes=[pltpu.VMEM((d // LANES, tm, LANES), F32),
                        pltpu.VMEM((N_GROUPS, tm, d), BF16)],
        compiler_params=pltpu.CompilerParams(dimension_semantics=("arbitrary", "arbitrary"),
                                             vmem_limit_bytes=VMEM_LIMIT),
        name="proj_qkv",
    )(x3, g, w, cols, cache_t)


def _project_rest(x3, g, w, first_col, cols01, caches01, *, tm, dils):
    b, s, d = x3.shape
    tn = GROUP_WIDTH
    n = w.shape[1] - first_col
    assert first_col % tn == 0 and n % tn == 0
    n_w = n // tn
    n_i = s // tm
    in_specs = [pl.BlockSpec((None, tm, d), lambda bb, i: (bb, i, 0)), _resident((1, d))]
    in_specs += [pl.BlockSpec((d, tn), lambda bb, i, k=k: (0, first_col // tn + k),
                              pipeline_mode=pl.Buffered(1)) for k in range(n_w)]
    out_shape = [jax.ShapeDtypeStruct((b, s, n), BF16)]
    out_specs = [pl.BlockSpec((None, tm, n), lambda bb, i: (bb, i, 0))]
    args = [x3, g] + [w] * n_w
    for cols, cache_t in zip(cols01, caches01):
        ups, blk, stat, stat_shape = _cache_specs(cache_t, b * n_i, n_i)
        in_specs += [_resident(cols.shape), blk]
        args += [cols, cache_t]
        out_shape += [jax.ShapeDtypeStruct(cache_t.shape, cache_t.dtype), stat_shape, stat_shape]
        out_specs += [blk, stat, stat]
    return pl.pallas_call(
        functools.partial(_proj_rest_kernel, tn=tn, n_w=n_w, dils=dils, units_per_seq=ups),
        out_shape=tuple(out_shape),
        grid=(b, n_i),
        in_specs=in_specs,
        out_specs=tuple(out_specs),
        compiler_params=pltpu.CompilerParams(dimension_semantics=("arbitrary", "arbitrary"),
                                             vmem_limit_bytes=VMEM_LIMIT),
        name="proj_rest",
    )(*args)


def _proj_rows_kernel(x_ref, g_ref, w_ref, a_ref, b_ref, *, tn):
    h = _rms(x_ref[...], g_ref[...]).astype(BF16)
    na = a_ref.shape[-1]
    for c0 in range(0, w_ref.shape[-1], tn):
        res = jnp.dot(h, w_ref[:, c0:c0 + tn], preferred_element_type=F32)
        if c0 < na:
            a_ref[:, c0:c0 + tn] = res.astype(a_ref.dtype)
        else:
            b_ref[:, c0 - na:c0 - na + tn] = res.astype(b_ref.dtype)


def _project_rows(x2d, g, w, n_cols, *, out_dtype):
    m, d = x2d.shape
    whole = lambda a: pl.BlockSpec(a.shape, lambda i: (0,) * a.ndim)
    out_shape = (jax.ShapeDtypeStruct((m, n_cols), out_dtype),
                 jax.ShapeDtypeStruct((m, w.shape[1] - n_cols), out_dtype))
    return pl.pallas_call(
        functools.partial(_proj_rows_kernel, tn=GROUP_WIDTH),
        out_shape=out_shape,
        grid=(1,),
        in_specs=[whole(x2d), _resident((1, d)), _resident(w.shape)],
        out_specs=tuple(whole(o) for o in out_shape),
        compiler_params=pltpu.CompilerParams(dimension_semantics=("arbitrary",),
                                             vmem_limit_bytes=VMEM_LIMIT),
        name="proj_sample",
    )(x2d, g, w)


def _attn_kernel(q_ref, kc_ref, vc_ref, kp_ref, vp_ref, o_ref, l_ref, kcat, vcat, *, tq):
    i = pl.program_id(2)
    kcat[0:BAND, :] = kp_ref[...]
    kcat[BAND:, :] = kc_ref[...]
    vcat[0:BAND, :] = vp_ref[...]
    vcat[BAND:, :] = vc_ref[...]

    qi = lax.broadcasted_iota(jnp.int32, (BAND, 2 * BAND), 0)
    ki = lax.broadcasted_iota(jnp.int32, (BAND, 2 * BAND), 1)
    rel = qi + BAND - ki
    band_ok = (rel >= 0) & (rel <= BAND)
    lane = lax.broadcasted_iota(jnp.int32, (BAND, LANES), 1)
    low = lane < HEAD_DIM

    for qb in range(tq // BAND):
        r0 = qb * BAND
        if qb == 0:
            valid = band_ok & ((ki >= BAND) | (i > 0))
        else:
            valid = band_ok
        valid2 = jnp.concatenate([valid, valid], axis=0)
        for hp in range(GROUP_WIDTH // LANES):
            c0 = hp * LANES
            q2 = q_ref[r0:r0 + BAND, c0:c0 + LANES]
            k2 = kcat[r0:r0 + 2 * BAND, c0:c0 + LANES]
            v2 = vcat[r0:r0 + 2 * BAND, c0:c0 + LANES]
            zero = jnp.zeros_like(q2)
            qm = jnp.concatenate([jnp.where(low, q2, zero), jnp.where(low, zero, q2)], axis=0)
            s = lax.dot_general(qm, k2, (((1,), (1,)), ((), ())), preferred_element_type=F32)
            s = jnp.where(valid2, s, NEG)
            m = jnp.max(s, axis=-1, keepdims=True)
            p = jnp.exp2(s - m)
            den = jnp.sum(p, axis=-1, keepdims=True)
            pv = jnp.dot(p.astype(BF16), v2, preferred_element_type=F32)
            o = pv * (1.0 / den)
            lse = (m + jnp.log2(den)) * LN2
            o_ref[r0:r0 + BAND, c0:c0 + LANES] = jnp.where(low, o[0:BAND], o[BAND:]).astype(o_ref.dtype)
            l_ref[r0:r0 + BAND, c0:c0 + LANES] = jnp.where(low, lse[0:BAND], lse[BAND:])


def _prompt_attention(qkv_g, g, *, tq):
    b, r, sr, _ = qkv_g.shape
    tq = min(tq, sr)
    per = tq // BAND

    def cur(j):
        return pl.BlockSpec((None, None, tq, GROUP_WIDTH), lambda bb, c, i: (bb, c, i, j))

    def prev(j):
        return pl.BlockSpec((None, None, BAND, GROUP_WIDTH),
                            lambda bb, c, i: (bb, c, jnp.maximum(i * per - 1, 0), j))

    out_spec = pl.BlockSpec((None, None, tq, GROUP_WIDTH), lambda bb, c, i: (bb, c, i, 0))
    return pl.pallas_call(
        functools.partial(_attn_kernel, tq=tq),
        out_shape=(jax.ShapeDtypeStruct((b, r, sr, GROUP_WIDTH), F32),
                   jax.ShapeDtypeStruct((b, r, sr, GROUP_WIDTH), F32)),
        grid=(b, r, sr // tq),
        in_specs=[cur(0), cur(1), cur(2), prev(1), prev(2)],
        out_specs=(out_spec, out_spec),
        scratch_shapes=[pltpu.VMEM((BAND + tq, GROUP_WIDTH), BF16),
                        pltpu.VMEM((BAND + tq, GROUP_WIDTH), BF16)],
        compiler_params=pltpu.CompilerParams(
            dimension_semantics=("arbitrary", "arbitrary", "arbitrary"),
            vmem_limit_bytes=VMEM_LIMIT),
        name=f"attn_g{g}",
    )(qkv_g, qkv_g, qkv_g, qkv_g, qkv_g)


def _merge_groups(os_, ls):
    m = jnp.maximum(jnp.maximum(ls[0], ls[1]), ls[2])
    es = [jnp.exp(l - m) for l in ls]
    num = es[0] * os_[0] + es[1] * os_[1] + es[2] * os_[2]
    return num / (es[0] + es[1] + es[2])


def _block_core(x, attn, c_b, conv_v, g_a, g_c, wao_ref, wco_ref, wo_ref, w1_ref, w2_ref,
                gpost, gpre, gfpost, *, ff_chunk):
    a = jnp.dot(attn.astype(BF16), wao_ref[...], preferred_element_type=F32)
    c = jnp.dot((c_b * conv_v).astype(BF16), wco_ref[...], preferred_element_type=F32)
    mix_in = jax.nn.sigmoid(g_a) * a + jax.nn.sigmoid(g_c) * c
    mix = jnp.dot(mix_in.astype(BF16), wo_ref[...], preferred_element_type=F32)
    x1 = x + _rms(mix, gpost)
    h2 = _rms(x1, gpre).astype(BF16)
    f = None
    for k0 in range(0, w1_ref.shape[1], ff_chunk):
        t = jnp.dot(h2, w1_ref[:, k0:k0 + ff_chunk], preferred_element_type=F32)
        t = jnp.square(jnp.maximum(t, 0.0)).astype(BF16)
        part = jnp.dot(t, w2_ref[k0:k0 + ff_chunk, :], preferred_element_type=F32)
        f = part if f is None else f + part
    return x1 + _rms(f, gfpost)


def _rest_columns(ra_ref, rb_ref, d):
    a0 = ra_ref.shape[-1] - 2 * d
    c_b = ra_ref[:, a0:a0 + d].astype(F32)
    u = ra_ref[:, a0 + d:a0 + 2 * d].astype(F32) * rb_ref[:, 0:d].astype(F32)
    g_a = rb_ref[:, d:2 * d].astype(F32)
    g_c = rb_ref[:, 2 * d:3 * d].astype(F32)
    return c_b, u, g_a, g_c


def _block_prompt_kernel(x_ref, ra_ref, rb_ref, hcc_ref, hch_ref, o0, l0, o1, l1, o2, l2,
                         convw_ref, wao_ref, wco_ref, wo_ref, w1_ref, w2_ref,
                         gpost_ref, gpre_ref, gfpost_ref, y_ref, tail_ref,
                         u_scr, nat_scr, *, ff_chunk):
    i = pl.program_id(1)
    tm, d = x_ref.shape
    vals = []
    for k, ref in enumerate((o0, l0, o1, l1, o2, l2)):
        r = ref.shape[0]
        if r == 1:
            vals.append(ref[0])
        else:
            nl = GROUP_WIDTH // LANES
            for c in range(r):
                for j in range(nl):
                    nat_scr[k * nl + j, pl.ds(c, tm // r, stride=r), :] = (
                        ref[c, :, j * LANES:(j + 1) * LANES])
            vals.append(jnp.concatenate([nat_scr[k * nl + j] for j in range(nl)], axis=1))
    attn = _merge_groups(vals[0::2], vals[1::2])
    c_b, u, g_a, g_c = _rest_columns(ra_ref, rb_ref, d)
    hu = hcc_ref[...].astype(F32) * hch_ref[...].astype(F32)
    u_scr[0:SUBLANES, :] = jnp.where(i > 0, hu, jnp.zeros_like(hu))
    u_scr[SUBLANES:, :] = u
    w = convw_ref[...]
    conv_v = (w[0:1, :] * u_scr[SUBLANES - 2:SUBLANES - 2 + tm, :]
              + w[1:2, :] * u_scr[SUBLANES - 1:SUBLANES - 1 + tm, :]
              + w[2:3, :] * u)
    y_ref[...] = _block_core(x_ref[...], attn, c_b, conv_v, g_a, g_c,
                             wao_ref, wco_ref, wo_ref, w1_ref, w2_ref,
                             gpost_ref[...], gpre_ref[...], gfpost_ref[...], ff_chunk=ff_chunk)
    tail_ref[...] = u[tm - SUBLANES:tm, :]


def _block_sample_kernel(x_ref, ra_ref, rb_ref, st_ref, o0, l0, o1, l1, o2, l2,
                         convw_ref, wao_ref, wco_ref, wo_ref, w1_ref, w2_ref,
                         gpost_ref, gpre_ref, gfpost_ref, y_ref, u_ref, *, ff_chunk):
    d = x_ref.shape[1]
    attn = _merge_groups((o0[...], o1[...], o2[...]), (l0[...], l1[...], l2[...]))
    c_b, u, g_a, g_c = _rest_columns(ra_ref, rb_ref, d)
    w = convw_ref[...]
    conv_v = w[0:1, :] * st_ref[:, 0:d] + w[1:2, :] * st_ref[:, d:2 * d] + w[2:3, :] * u
    y_ref[...] = _block_core(x_ref[...], attn, c_b, conv_v, g_a, g_c,
                             wao_ref, wco_ref, wo_ref, w1_ref, w2_ref,
                             gpost_ref[...], gpre_ref[...], gfpost_ref[...], ff_chunk=ff_chunk)
    u_ref[...] = u


def _weight_specs(ws):
    return [_resident(w.shape) for w in ws]


def _block_prompt(x3, ra3, rb3, os_, ls_, weights, *, tm, ff_chunk):
    b, s, d = x3.shape
    assert ra3.shape[-1] == 2 * d and rb3.shape[-1] == 3 * d
    per = tm // SUBLANES
    row = lambda width: pl.BlockSpec((None, tm, width), lambda bb, i: (bb, i, 0))
    halo = lambda col: pl.BlockSpec((None, SUBLANES, d),
                                    lambda bb, i: (bb, jnp.maximum(i * per - 1, 0), col))
    in_specs = [row(d), row(2 * d), row(3 * d), halo(1), halo(0)]
    args = [x3, ra3, rb3, ra3, rb3]
    for o, l in zip(os_, ls_):
        r = o.shape[1]
        res = pl.BlockSpec((None, r, tm // r, GROUP_WIDTH), lambda bb, i: (bb, 0, i, 0))
        in_specs += [res, res]
        args += [o, l]
    in_specs += _weight_specs(weights)
    args += list(weights)
    n_slabs = 2 * N_GROUPS * GROUP_WIDTH // LANES
    y, tail = pl.pallas_call(
        functools.partial(_block_prompt_kernel, ff_chunk=ff_chunk),
        out_shape=(jax.ShapeDtypeStruct((b, s, d), F32),
                   jax.ShapeDtypeStruct((b, SUBLANES, d), F32)),
        grid=(b, s // tm),
        in_specs=in_specs,
        out_specs=(row(d), pl.BlockSpec((None, SUBLANES, d), lambda bb, i: (bb, 0, 0))),
        scratch_shapes=[pltpu.VMEM((SUBLANES + tm, d), F32),
                        pltpu.VMEM((n_slabs, tm, LANES), F32)],
        compiler_params=pltpu.CompilerParams(dimension_semantics=("arbitrary", "arbitrary"),
                                             vmem_limit_bytes=VMEM_LIMIT),
        name="block_prompt",
    )(*args)
    return y, tail


def _block_sample(x2, ra2, rb2, state2, os_, ls_, weights, *, ff_chunk):
    m, d = x2.shape
    full = lambda a: pl.BlockSpec(a.shape, lambda i: (0,) * a.ndim)
    args = [x2, ra2, rb2, state2]
    for o, l in zip(os_, ls_):
        args += [o, l]
    in_specs = [full(a) for a in args] + _weight_specs(weights)
    args += list(weights)
    return pl.pallas_call(
        functools.partial(_block_sample_kernel, ff_chunk=ff_chunk),
        out_shape=(jax.ShapeDtypeStruct((m, d), F32), jax.ShapeDtypeStruct((m, d), F32)),
        grid=(1,),
        in_specs=in_specs,
        out_specs=(pl.BlockSpec((m, d), lambda i: (0, 0)), pl.BlockSpec((m, d), lambda i: (0, 0))),
        compiler_params=pltpu.CompilerParams(dimension_semantics=("arbitrary",),
                                             vmem_limit_bytes=VMEM_LIMIT),
        name="block_sample",
    )(*args)


def _cache_unit(cols_ref, cache_ref, out_ref, o_ref, l_ref, seq, part, dil):
    _, nrows, length = cache_ref.shape
    scale = HEAD_DIM ** -0.5
    mine = lax.broadcasted_iota(jnp.int32, (1, LANES), 1) == seq
    base = pl.multiple_of(part * nrows, SUBLANES)

    def column(segment):
        blk = cols_ref[pl.ds(segment * GROUP_WIDTH + base, nrows), :]
        return jnp.sum(jnp.where(mine, blk, 0.0), axis=-1, keepdims=True)

    qc, knc, vnc = column(0), column(1), column(2)
    row = lax.broadcasted_iota(jnp.int32, (1, length), 1)
    dist = length - row
    on_grid = ((dist % dil) == 0) & (dist <= BAND * dil)
    o_cols, l_cols = [], []
    for h in range(nrows // HEAD_DIM):
        rows = slice(h * HEAD_DIM, (h + 1) * HEAD_DIM)
        s = jnp.sum(cache_ref[0, rows, :] * qc[rows], axis=0, keepdims=True) * scale
        s = jnp.where(on_grid, s, NEG)
        s_new = jnp.sum(qc[rows] * knc[rows], axis=0, keepdims=True) * scale
        m = jnp.maximum(jnp.max(s, axis=-1, keepdims=True), s_new)
        p = jnp.exp(s - m)
        p_new = jnp.exp(s_new - m)
        den = jnp.sum(p, axis=-1, keepdims=True) + p_new
        acc = jnp.sum(cache_ref[1, rows, :] * p, axis=-1, keepdims=True) + p_new * vnc[rows]
        o_cols.append(jnp.broadcast_to(acc / den, (HEAD_DIM, LANES)))
        l_cols.append(jnp.broadcast_to(m + jnp.log(den), (HEAD_DIM, LANES)))
    o_ref[...] = jnp.concatenate(o_cols, axis=0).T[0:SUBLANES]
    l_ref[...] = jnp.concatenate(l_cols, axis=0).T[0:SUBLANES]

    last = row == length - 1
    for kv, newc in ((0, knc), (1, vnc)):
        rolled = pltpu.roll(cache_ref[kv], length - 1, axis=1)
        out_ref[kv] = jnp.where(last, newc, rolled)


def kernel(x_prompt, x_sample, cache_kv_w128, cache_kv_w512, cache_kv_w2048, state_conv, w_in, conv_w, w_attn_o, w_conv_o, w_o, w_ff1, w_ff2, g_mix_pre, g_mix_post, g_ffn_pre, g_ffn_post):
    depth = w_in.shape[0]
    b, s, d = x_prompt.shape
    db, ds_, _ = x_sample.shape
    assert ds_ == 1 and d % LANES == 0 and db <= LANES
    caches = (cache_kv_w128, cache_kv_w512, cache_kv_w2048)
    gw = GROUP_WIDTH

    yp, ys = x_prompt, x_sample.reshape(db, d)
    kv_p = [[] for _ in range(N_GROUPS)]
    kv_s = [[] for _ in range(N_GROUPS)]
    conv_p, conv_s = [], []
    for l in range(depth):
        na = 3 * N_GROUPS * gw + 2 * d
        w_in_b = w_in[l].astype(BF16)
        weights = (conv_w[l], w_attn_o[l].astype(BF16), w_conv_o[l].astype(BF16),
                   w_o[l].astype(BF16), w_ff1[l].astype(BF16), w_ff2[l].astype(BF16),
                   g_mix_post[l][None, :], g_ffn_pre[l][None, :], g_ffn_post[l][None, :])
        g_pre = g_mix_pre[l][None, :]

        qkv_s, rest_s = _project_rows(ys, g_pre, w_in_b, na, out_dtype=F32)
        cols, caches_t = [], []
        for g in range(N_GROUPS):
            length = caches[g].shape[2]
            caches_t.append(jnp.transpose(caches[g][l], (0, 2, 3, 4, 1)).reshape(db, 2, gw, length))
            qkv_g = jnp.concatenate([qkv_s[:, (j * N_GROUPS + g) * gw:(j * N_GROUPS + g + 1) * gw]
                                     for j in range(3)], axis=1)
            cols.append(jnp.pad(qkv_g, ((0, LANES - db), (0, 0))).T)

        q0, q1, q2, ra3, t0, t1, t2, new2, so2, sl2 = _project_qkv(
            yp, g_pre, w_in_b, na, cols[2], caches_t[2], tm=256, dil=DILATIONS[2])
        qkv_groups = (q0, q1, q2)
        rb3, new0, so0, sl0, new1, so1, sl1 = _project_rest(
            yp, g_pre, w_in_b, na, cols[:2], caches_t[:2], tm=256, dils=DILATIONS[:2])
        sample_parts = ((new0, so0, sl0), (new1, so1, sl1), (new2, so2, sl2))
        os_, ls_ = [], []
        for g in range(N_GROUPS):
            o, lse = _prompt_attention(qkv_groups[g], g, tq=512)
            os_.append(o)
            ls_.append(lse)
            tail_t = (t0, t1, t2)[g]
            w = tail_t.shape[-1]
            kv_p[g].append(jnp.transpose(tail_t.reshape(b, 2, HEADS, HEAD_DIM, w), (0, 4, 1, 2, 3)))
        yp, tail = _block_prompt(yp, ra3, rb3, os_, ls_, weights, tm=256, ff_chunk=1024)
        conv_p.append(tail[:, SUBLANES - (CONV_WIDTH - 1):, :])

        os_, ls_ = [], []
        for g in range(N_GROUPS):
            new_t, so, sl = sample_parts[g]
            length = new_t.shape[-1]
            os_.append(so[:, 0, :].reshape(db, gw))
            ls_.append(sl[:, 0, :].reshape(db, gw))
            new = jnp.transpose(new_t.reshape(db, 2, HEADS, HEAD_DIM, length), (0, 4, 1, 2, 3))
            kv_s[g].append(new)
        state = state_conv[l]
        ys, u_s = _block_sample(ys, qkv_s, rest_s, state.reshape(db, (CONV_WIDTH - 1) * d),
                                os_, ls_, weights, ff_chunk=1024)
        conv_s.append(jnp.stack([state[:, 1, :], u_s], axis=1))

    stack = lambda xs: jnp.stack(xs, axis=0)
    return (yp, ys.reshape(db, 1, d),
            stack(kv_p[0]), stack(kv_p[1]), stack(kv_p[2]), stack(conv_p),
            stack(kv_s[0]), stack(kv_s[1]), stack(kv_s[2]), stack(conv_s))
```

```python
import functools

import jax
import jax.numpy as jnp
from jax import lax
from jax.experimental import pallas as pl
from jax.experimental.pallas import tpu as pltpu

HEAD_DIM = 64
HEADS = 8
GROUP_WIDTH = HEADS * HEAD_DIM
N_GROUPS = 3
WINDOWS = (128, 512, 2048)
DILATIONS = (1, 4, 16)
BAND = 128
CONV_WIDTH = 3
EPS = 1e-6
NEG = -1e30
LOG2E = 1.4426950408889634
LN2 = 0.6931471805599453
LANES = 128
SUBLANES = 8
VMEM_LIMIT = 56 * 1024 * 1024

F32 = jnp.float32
BF16 = jnp.bfloat16


def _rms(x, g):
    return (x * lax.rsqrt(jnp.mean(x * x, axis=-1, keepdims=True) + EPS)) * g


def _resident(shape):
    nd = len(shape)
    return pl.BlockSpec(shape, lambda *_: (0,) * nd, pipeline_mode=pl.Buffered(1))


def _unit_index(units_per_seq):
    u = pl.program_id(0) * pl.num_programs(1) + pl.program_id(1)
    return u // units_per_seq, u % units_per_seq


def _proj_qkv_kernel(x_ref, g_ref, w_ref, cols_ref, cache_ref,
                     q0_ref, q1_ref, q2_ref, ra_ref, t0_ref, t1_ref, t2_ref,
                     new_ref, so_ref, sl_ref, h_scr, hb_scr,
                     *, tn, dil, units_per_seq, tail_tiles):
    tm = x_ref.shape[0]
    h = _rms(x_ref[...], g_ref[...])
    hb_scr[0] = h.astype(BF16)
    for j in range(h_scr.shape[0]):
        h_scr[j] = h[:, j * LANES:(j + 1) * LANES]
    for g in range(1, N_GROUPS):
        r = DILATIONS[g]
        for c in range(r):
            for j in range(h_scr.shape[0]):
                hb_scr[g, c * (tm // r):(c + 1) * (tm // r), j * LANES:(j + 1) * LANES] = (
                    h_scr[j, pl.ds(c, tm // r, stride=r), :].astype(BF16))
    for g, out in enumerate((q0_ref, q1_ref, q2_ref)):
        r = DILATIONS[g]
        for j in range(3):
            c0 = (j * N_GROUPS + g) * tn
            res = jnp.dot(hb_scr[g], w_ref[:, c0:c0 + tn], preferred_element_type=F32)
            if j == 0:
                res = res * (HEAD_DIM ** -0.5 * LOG2E)
            out[:, :, j * tn:(j + 1) * tn] = res.reshape(r, tm // r, tn).astype(out.dtype)
    nq = 3 * N_GROUPS * tn
    for c0 in range(nq, w_ref.shape[-1], tn):
        res = jnp.dot(hb_scr[0], w_ref[:, c0:c0 + tn], preferred_element_type=F32)
        ra_ref[:, c0 - nq:c0 - nq + tn] = res.astype(ra_ref.dtype)
    seq, part = _unit_index(units_per_seq)
    _cache_unit(cols_ref, cache_ref, new_ref, so_ref, sl_ref, seq, part, dil)
    for g, (src, out) in enumerate(zip((q0_ref, q1_ref, q2_ref), (t0_ref, t1_ref, t2_ref))):
        lanes = out.shape[-1]
        r = DILATIONS[g]

        @pl.when(pl.program_id(1) >= pl.num_programs(1) - tail_tiles[g])
        def _():
            for j in (1, 2):
                for c in range(r):
                    for n in range(tn // LANES):
                        h_scr[n, pl.ds(c, tm // r, stride=r), :] = (
                            src[c, :, j * tn + n * LANES:j * tn + (n + 1) * LANES].astype(F32))
                res = jnp.concatenate([h_scr[n] for n in range(tn // LANES)], axis=1)
                out[j - 1] = res[tm - lanes:, :].T


def _proj_rest_kernel(x_ref, g_ref, *refs, tn, n_w, dils, units_per_seq):
    w_refs = refs[:n_w]
    cols0_ref, cache0_ref, cols1_ref, cache1_ref = refs[n_w:n_w + 4]
    rest_ref, new0_ref, so0_ref, sl0_ref, new1_ref, so1_ref, sl1_ref = refs[n_w + 4:]
    h = _rms(x_ref[...], g_ref[...]).astype(BF16)
    for k, w_ref in enumerate(w_refs):
        res = jnp.dot(h, w_ref[...], preferred_element_type=F32)
        rest_ref[:, k * tn:(k + 1) * tn] = res.astype(rest_ref.dtype)
    seq, part = _unit_index(units_per_seq)
    _cache_unit(cols0_ref, cache0_ref, new0_ref, so0_ref, sl0_ref, seq, part, dils[0])
    _cache_unit(cols1_ref, cache1_ref, new1_ref, so1_ref, sl1_ref, seq, part, dils[1])


def _cache_specs(cache_t, steps, n_i):
    db, _, gw, length = cache_t.shape
    ups = steps // db
    assert steps == db * ups and gw % ups == 0 and (gw // ups) % HEAD_DIM == 0, (steps, db)
    rows = gw // ups
    unit = lambda bb, i: bb * n_i + i
    blk = pl.BlockSpec((None, 2, rows, length),
                       lambda bb, i: (unit(bb, i) // ups, 0, unit(bb, i) % ups, 0))
    stat = pl.BlockSpec((None, SUBLANES, rows), lambda bb, i: (unit(bb, i), 0, 0))
    stat_shape = jax.ShapeDtypeStruct((steps, SUBLANES, rows), F32)
    return ups, blk, stat, stat_shape


def _project_qkv(x3, g, w, n_cols, cols, cache_t, *, tm, dil):
    b, s, d = x3.shape
    tn = GROUP_WIDTH
    n_i = s // tm
    extra = n_cols - 3 * N_GROUPS * tn
    w_spec = pl.BlockSpec((d, n_cols), lambda bb, i: (0, 0), pipeline_mode=pl.Buffered(1))
    ups, blk, stat, stat_shape = _cache_specs(cache_t, b * n_i, n_i)
    grp_shape = lambda r: jax.ShapeDtypeStruct((b, r, s // r, 3 * tn), BF16)
    grp_spec = lambda r: pl.BlockSpec((None, r, tm // r, 3 * tn), lambda bb, i: (bb, 0, i, 0))
    tails = [min(w, s) for w in WINDOWS]
    tail_tiles = tuple(max(w // tm, 1) for w in tails)
    assert all(w % min(tm, w) == 0 and min(tm, w) % LANES == 0 for w in tails)
    tail_shape = lambda w: jax.ShapeDtypeStruct((b, 2, tn, w), F32)
    tail_spec = lambda w, nt: pl.BlockSpec(
        (None, 2, tn, min(tm, w)), lambda bb, i: (bb, 0, 0, jnp.maximum(i - (n_i - nt), 0)))
    return pl.pallas_call(
        functools.partial(_proj_qkv_kernel, tn=tn, dil=dil, units_per_seq=ups,
                          tail_tiles=tail_tiles),
        out_shape=tuple(grp_shape(r) for r in DILATIONS)
        + (jax.ShapeDtypeStruct((b, s, extra), BF16),)
        + tuple(tail_shape(w) for w in tails)
        + (jax.ShapeDtypeStruct(cache_t.shape, cache_t.dtype), stat_shape, stat_shape),
        grid=(b, n_i),
        in_specs=[pl.BlockSpec((None, tm, d), lambda bb, i: (bb, i, 0)),
                  _resident((1, d)), w_spec, _resident(cols.shape), blk],
        out_specs=tuple(grp_spec(r) for r in DILATIONS)
        + (pl.BlockSpec((None, tm, extra), lambda bb, i: (bb, i, 0)),)
        + tuple(tail_spec(w, nt) for w, nt in zip(tails, tail_tiles))
        + (blk, stat, stat),
        scratch_shapes=[pltpu.VMEM((d // LANES, tm, LANES), F32),
                        pltpu.VMEM((N_GROUPS, tm, d), BF16)],
        compiler_params=pltpu.CompilerParams(dimension_semantics=("arbitrary", "arbitrary"),
                                             vmem_limit_bytes=VMEM_LIMIT),
        name="proj_qkv",
    )(x3, g, w, cols, cache_t)


def _project_rest(x3, g, w, first_col, cols01, caches01, *, tm, dils):
    b, s, d = x3.shape
    tn = GROUP_WIDTH
    n = w.shape[1] - first_col
    assert first_col % tn == 0 and n % tn == 0
    n_w = n // tn
    n_i = s // tm
    in_specs = [pl.BlockSpec((None, tm, d), lambda bb, i: (bb, i, 0)), _resident((1, d))]
    in_specs += [pl.BlockSpec((d, tn), lambda bb, i, k=k: (0, first_col // tn + k),
                              pipeline_mode=pl.Buffered(1)) for k in range(n_w)]
    out_shape = [jax.ShapeDtypeStruct((b, s, n), BF16)]
    out_specs = [pl.BlockSpec((None, tm, n), lambda bb, i: (bb, i, 0))]
    args = [x3, g] + [w] * n_w
    for cols, cache_t in zip(cols01, caches01):
        ups, blk, stat, stat_shape = _cache_specs(cache_t, b * n_i, n_i)
        in_specs += [_resident(cols.shape), blk]
        args += [cols, cache_t]
        out_shape += [jax.ShapeDtypeStruct(cache_t.shape, cache_t.dtype), stat_shape, stat_shape]
        out_specs += [blk, stat, stat]
    return pl.pallas_call(
        functools.partial(_proj_rest_kernel, tn=tn, n_w=n_w, dils=dils, units_per_seq=ups),
        out_shape=tuple(out_shape),
        grid=(b, n_i),
        in_specs=in_specs,
        out_specs=tuple(out_specs),
        compiler_params=pltpu.CompilerParams(dimension_semantics=("arbitrary", "arbitrary"),
                                             vmem_limit_bytes=VMEM_LIMIT),
        name="proj_rest",
    )(*args)


def _proj_rows_kernel(x_ref, g_ref, w_ref, a_ref, b_ref, *, tn):
    h = _rms(x_ref[...], g_ref[...]).astype(BF16)
    na = a_ref.shape[-1]
    for c0 in range(0, w_ref.shape[-1], tn):
        res = jnp.dot(h, w_ref[:, c0:c0 + tn], preferred_element_type=F32)
        if c0 < na:
            a_ref[:, c0:c0 + tn] = res.astype(a_ref.dtype)
        else:
            b_ref[:, c0 - na:c0 - na + tn] = res.astype(b_ref.dtype)


def _project_rows(x2d, g, w, n_cols, *, out_dtype):
    m, d = x2d.shape
    whole = lambda a: pl.BlockSpec(a.shape, lambda i: (0,) * a.ndim)
    out_shape = (jax.ShapeDtypeStruct((m, n_cols), out_dtype),
                 jax.ShapeDtypeStruct((m, w.shape[1] - n_cols), out_dtype))
    return pl.pallas_call(
        functools.partial(_proj_rows_kernel, tn=GROUP_WIDTH),
        out_shape=out_shape,
        grid=(1,),
        in_specs=[whole(x2d), _resident((1, d)), _resident(w.shape)],
        out_specs=tuple(whole(o) for o in out_shape),
        compiler_params=pltpu.CompilerParams(dimension_semantics=("arbitrary",),
                                             vmem_limit_bytes=VMEM_LIMIT),
        name="proj_sample",
    )(x2d, g, w)


def _attn_kernel(q_ref, kc_ref, vc_ref, kp_ref, vp_ref, o_ref, l_ref, kcat, vcat, *, tq):
    i = pl.program_id(2)
    kcat[0:BAND, :] = kp_ref[...]
    kcat[BAND:, :] = kc_ref[...]
    vcat[0:BAND, :] = vp_ref[...]
    vcat[BAND:, :] = vc_ref[...]

    qi = lax.broadcasted_iota(jnp.int32, (BAND, 2 * BAND), 0)
    ki = lax.broadcasted_iota(jnp.int32, (BAND, 2 * BAND), 1)
    rel = qi + BAND - ki
    band_ok = (rel >= 0) & (rel <= BAND)
    lane = lax.broadcasted_iota(jnp.int32, (BAND, LANES), 1)
    low = lane < HEAD_DIM

    for qb in range(tq // BAND):
        r0 = qb * BAND
        if qb == 0:
            valid = band_ok & ((ki >= BAND) | (i > 0))
        else:
            valid = band_ok
        valid2 = jnp.concatenate([valid, valid], axis=0)
        for hp in range(GROUP_WIDTH // LANES):
            c0 = hp * LANES
            q2 = q_ref[r0:r0 + BAND, c0:c0 + LANES]
            k2 = kcat[r0:r0 + 2 * BAND, c0:c0 + LANES]
            v2 = vcat[r0:r0 + 2 * BAND, c0:c0 + LANES]
            zero = jnp.zeros_like(q2)
            qm = jnp.concatenate([jnp.where(low, q2, zero), jnp.where(low, zero, q2)], axis=0)
            s = lax.dot_general(qm, k2, (((1,), (1,)), ((), ())), preferred_element_type=F32)
            s = jnp.where(valid2, s, NEG)
            m = jnp.max(s, axis=-1, keepdims=True)
            p = jnp.exp2(s - m)
            den = jnp.sum(p, axis=-1, keepdims=True)
            pv = jnp.dot(p.astype(BF16), v2, preferred_element_type=F32)
            o = pv * (1.0 / den)
            lse = (m + jnp.log2(den)) * LN2
            o_ref[r0:r0 + BAND, c0:c0 + LANES] = jnp.where(low, o[0:BAND], o[BAND:]).astype(o_ref.dtype)
            l_ref[r0:r0 + BAND, c0:c0 + LANES] = jnp.where(low, lse[0:BAND], lse[BAND:])


def _prompt_attention(qkv_g, g, *, tq):
    b, r, sr, _ = qkv_g.shape
    tq = min(tq, sr)
    per = tq // BAND

    def cur(j):
        return pl.BlockSpec((None, None, tq, GROUP_WIDTH), lambda bb, c, i: (bb, c, i, j))

    def prev(j):
        return pl.BlockSpec((None, None, BAND, GROUP_WIDTH),
                            lambda bb, c, i: (bb, c, jnp.maximum(i * per - 1, 0), j))

    out_spec = pl.BlockSpec((None, None, tq, GROUP_WIDTH), lambda bb, c, i: (bb, c, i, 0))
    return pl.pallas_call(
        functools.partial(_attn_kernel, tq=tq),
        out_shape=(jax.ShapeDtypeStruct((b, r, sr, GROUP_WIDTH), F32),
                   jax.ShapeDtypeStruct((b, r, sr, GROUP_WIDTH), F32)),
        grid=(b, r, sr // tq),
        in_specs=[cur(0), cur(1), cur(2), prev(1), prev(2)],
        out_specs=(out_spec, out_spec),
        scratch_shapes=[pltpu.VMEM((BAND + tq, GROUP_WIDTH), BF16),
                        pltpu.VMEM((BAND + tq, GROUP_WIDTH), BF16)],
        compiler_params=pltpu.CompilerParams(
            dimension_semantics=("arbitrary", "arbitrary", "arbitrary"),
            vmem_limit_bytes=VMEM_LIMIT),
        name=f"attn_g{g}",
    )(qkv_g, qkv_g, qkv_g, qkv_g, qkv_g)


def _merge_groups(os_, ls):
    m = jnp.maximum(jnp.maximum(ls[0], ls[1]), ls[2])
    es = [jnp.exp(l - m) for l in ls]
    num = es[0] * os_[0] + es[1] * os_[1] + es[2] * os_[2]
    return num / (es[0] + es[1] + es[2])


def _block_core(x, attn, c_b, conv_v, g_a, g_c, wao_ref, wco_ref, wo_ref, w1_ref, w2_ref,
                gpost, gpre, gfpost, *, ff_chunk):
    a = jnp.dot(attn.astype(BF16), wao_ref[...], preferred_element_type=F32)
    c = jnp.dot((c_b * conv_v).astype(BF16), wco_ref[...], preferred_element_type=F32)
    mix_in = jax.nn.sigmoid(g_a) * a + jax.nn.sigmoid(g_c) * c
    mix = jnp.dot(mix_in.astype(BF16), wo_ref[...], preferred_element_type=F32)
    x1 = x + _rms(mix, gpost)
    h2 = _rms(x1, gpre).astype(BF16)
    f = None
    for k0 in range(0, w1_ref.shape[1], ff_chunk):
        t = jnp.dot(h2, w1_ref[:, k0:k0 + ff_chunk], preferred_element_type=F32)
        t = jnp.square(jnp.maximum(t, 0.0)).astype(BF16)
        part = jnp.dot(t, w2_ref[k0:k0 + ff_chunk, :], preferred_element_type=F32)
        f = part if f is None else f + part
    return x1 + _rms(f, gfpost)


def _rest_columns(ra_ref, rb_ref, d):
    a0 = ra_ref.shape[-1] - 2 * d
    c_b = ra_ref[:, a0:a0 + d].astype(F32)
    u = ra_ref[:, a0 + d:a0 + 2 * d].astype(F32) * rb_ref[:, 0:d].astype(F32)
    g_a = rb_ref[:, d:2 * d].astype(F32)
    g_c = rb_ref[:, 2 * d:3 * d].astype(F32)
    return c_b, u, g_a, g_c


def _block_prompt_kernel(x_ref, ra_ref, rb_ref, hcc_ref, hch_ref, o0, l0, o1, l1, o2, l2,
                         convw_ref, wao_ref, wco_ref, wo_ref, w1_ref, w2_ref,
                         gpost_ref, gpre_ref, gfpost_ref, y_ref, tail_ref,
                         u_scr, nat_scr, *, ff_chunk):
    i = pl.program_id(1)
    tm, d = x_ref.shape
    vals = []
    for k, ref in enumerate((o0, l0, o1, l1, o2, l2)):
        r = ref.shape[0]
        if r == 1:
            vals.append(ref[0])
        else:
            nl = GROUP_WIDTH // LANES
            for c in range(r):
                for j in range(nl):
                    nat_scr[k * nl + j, pl.ds(c, tm // r, stride=r), :] = (
                        ref[c, :, j * LANES:(j + 1) * LANES])
            vals.append(jnp.concatenate([nat_scr[k * nl + j] for j in range(nl)], axis=1))
    attn = _merge_groups(vals[0::2], vals[1::2])
    c_b, u, g_a, g_c = _rest_columns(ra_ref, rb_ref, d)
    hu = hcc_ref[...].astype(F32) * hch_ref[...].astype(F32)
    u_scr[0:SUBLANES, :] = jnp.where(i > 0, hu, jnp.zeros_like(hu))
    u_scr[SUBLANES:, :] = u
    w = convw_ref[...]
    conv_v = (w[0:1, :] * u_scr[SUBLANES - 2:SUBLANES - 2 + tm, :]
              + w[1:2, :] * u_scr[SUBLANES - 1:SUBLANES - 1 + tm, :]
              + w[2:3, :] * u)
    y_ref[...] = _block_core(x_ref[...], attn, c_b, conv_v, g_a, g_c,
                             wao_ref, wco_ref, wo_ref, w1_ref, w2_ref,
                             gpost_ref[...], gpre_ref[...], gfpost_ref[...], ff_chunk=ff_chunk)
    tail_ref[...] = u[tm - SUBLANES:tm, :]


def _block_sample_kernel(x_ref, ra_ref, rb_ref, st_ref, o0, l0, o1, l1, o2, l2,
                         convw_ref, wao_ref, wco_ref, wo_ref, w1_ref, w2_ref,
                         gpost_ref, gpre_ref, gfpost_ref, y_ref, u_ref, *, ff_chunk):
    d = x_ref.shape[1]
    attn = _merge_groups((o0[...], o1[...], o2[...]), (l0[...], l1[...], l2[...]))
    c_b, u, g_a, g_c = _rest_columns(ra_ref, rb_ref, d)
    w = convw_ref[...]
    conv_v = w[0:1, :] * st_ref[:, 0:d] + w[1:2, :] * st_ref[:, d:2 * d] + w[2:3, :] * u
    y_ref[...] = _block_core(x_ref[...], attn, c_b, conv_v, g_a, g_c,
                             wao_ref, wco_ref, wo_ref, w1_ref, w2_ref,
                             gpost_ref[...], gpre_ref[...], gfpost_ref[...], ff_chunk=ff_chunk)
    u_ref[...] = u


def _weight_specs(ws):
    return [_resident(w.shape) for w in ws]


def _block_prompt(x3, ra3, rb3, os_, ls_, weights, *, tm, ff_chunk):
    b, s, d = x3.shape
    assert ra3.shape[-1] == 2 * d and rb3.shape[-1] == 3 * d
    per = tm // SUBLANES
    row = lambda width: pl.BlockSpec((None, tm, width), lambda bb, i: (bb, i, 0))
    halo = lambda col: pl.BlockSpec((None, SUBLANES, d),
                                    lambda bb, i: (bb, jnp.maximum(i * per - 1, 0), col))
    in_specs = [row(d), row(2 * d), row(3 * d), halo(1), halo(0)]
    args = [x3, ra3, rb3, ra3, rb3]
    for o, l in zip(os_, ls_):
        r = o.shape[1]
        res = pl.BlockSpec((None, r, tm // r, GROUP_WIDTH), lambda bb, i: (bb, 0, i, 0))
        in_specs += [res, res]
        args += [o, l]
    in_specs += _weight_specs(weights)
    args += list(weights)
    n_slabs = 2 * N_GROUPS * GROUP_WIDTH // LANES
    y, tail = pl.pallas_call(
        functools.partial(_block_prompt_kernel, ff_chunk=ff_chunk),
        out_shape=(jax.ShapeDtypeStruct((b, s, d), F32),
                   jax.ShapeDtypeStruct((b, SUBLANES, d), F32)),
        grid=(b, s // tm),
        in_specs=in_specs,
        out_specs=(row(d), pl.BlockSpec((None, SUBLANES, d), lambda bb, i: (bb, 0, 0))),
        scratch_shapes=[pltpu.VMEM((SUBLANES + tm, d), F32),
                        pltpu.VMEM((n_slabs, tm, LANES), F32)],
        compiler_params=pltpu.CompilerParams(dimension_semantics=("arbitrary", "arbitrary"),
                                             vmem_limit_bytes=VMEM_LIMIT),
        name="block_prompt",
    )(*args)
    return y, tail


def _block_sample(x2, ra2, rb2, state2, os_, ls_, weights, *, ff_chunk):
    m, d = x2.shape
    full = lambda a: pl.BlockSpec(a.shape, lambda i: (0,) * a.ndim)
    args = [x2, ra2, rb2, state2]
    for o, l in zip(os_, ls_):
        args += [o, l]
    in_specs = [full(a) for a in args] + _weight_specs(weights)
    args += list(weights)
    return pl.pallas_call(
        functools.partial(_block_sample_kernel, ff_chunk=ff_chunk),
        out_shape=(jax.ShapeDtypeStruct((m, d), F32), jax.ShapeDtypeStruct((m, d), F32)),
        grid=(1,),
        in_specs=in_specs,
        out_specs=(pl.BlockSpec((m, d), lambda i: (0, 0)), pl.BlockSpec((m, d), lambda i: (0, 0))),
        compiler_params=pltpu.CompilerParams(dimension_semantics=("arbitrary",),
                                             vmem_limit_bytes=VMEM_LIMIT),
        name="block_sample",
    )(*args)


def _cache_unit(cols_ref, cache_ref, out_ref, o_ref, l_ref, seq, part, dil):
    _, nrows, length = cache_ref.shape
    scale = HEAD_DIM ** -0.5
    mine = lax.broadcasted_iota(jnp.int32, (1, LANES), 1) == seq
    base = pl.multiple_of(part * nrows, SUBLANES)

    def column(segment):
        blk = cols_ref[pl.ds(segment * GROUP_WIDTH + base, nrows), :]
        return jnp.sum(jnp.where(mine, blk, 0.0), axis=-1, keepdims=True)

    qc, knc, vnc = column(0), column(1), column(2)
    row = lax.broadcasted_iota(jnp.int32, (1, length), 1)
    dist = length - row
    on_grid = ((dist % dil) == 0) & (dist <= BAND * dil)
    o_cols, l_cols = [], []
    for h in range(nrows // HEAD_DIM):
        rows = slice(h * HEAD_DIM, (h + 1) * HEAD_DIM)
        s = jnp.sum(cache_ref[0, rows, :] * qc[rows], axis=0, keepdims=True) * scale
        s = jnp.where(on_grid, s, NEG)
        s_new = jnp.sum(qc[rows] * knc[rows], axis=0, keepdims=True) * scale
        m = jnp.maximum(jnp.max(s, axis=-1, keepdims=True), s_new)
        p = jnp.exp(s - m)
        p_new = jnp.exp(s_new - m)
        den = jnp.sum(p, axis=-1, keepdims=True) + p_new
        acc = jnp.sum(cache_ref[1, rows, :] * p, axis=-1, keepdims=True) + p_new * vnc[rows]
        o_cols.append(jnp.broadcast_to(acc / den, (HEAD_DIM, LANES)))
        l_cols.append(jnp.broadcast_to(m + jnp.log(den), (HEAD_DIM, LANES)))
    o_ref[...] = jnp.concatenate(o_cols, axis=0).T[0:SUBLANES]
    l_ref[...] = jnp.concatenate(l_cols, axis=0).T[0:SUBLANES]

    last = row == length - 1
    for kv, newc in ((0, knc), (1, vnc)):
        rolled = pltpu.roll(cache_ref[kv], length - 1, axis=1)
        out_ref[kv] = jnp.where(last, newc, rolled)


def kernel(x_prompt, x_sample, cache_kv_w128, cache_kv_w512, cache_kv_w2048, state_conv, w_in, conv_w, w_attn_o, w_conv_o, w_o, w_ff1, w_ff2, g_mix_pre, g_mix_post, g_ffn_pre, g_ffn_post):
    depth = w_in.shape[0]
    b, s, d = x_prompt.shape
    db, ds_, _ = x_sample.shape
    assert ds_ == 1 and d % LANES == 0 and db <= LANES
    caches = (cache_kv_w128, cache_kv_w512, cache_kv_w2048)
    gw = GROUP_WIDTH

    yp, ys = x_prompt, x_sample.reshape(db, d)
    kv_p = [[] for _ in range(N_GROUPS)]
    kv_s = [[] for _ in range(N_GROUPS)]
    conv_p, conv_s = [], []
    for l in range(depth):
        na = 3 * N_GROUPS * gw + 2 * d
        w_in_b = w_in[l].astype(BF16)
        weights = (conv_w[l], w_attn_o[l].astype(BF16), w_conv_o[l].astype(BF16),
                   w_o[l].astype(BF16), w_ff1[l].astype(BF16), w_ff2[l].astype(BF16),
                   g_mix_post[l][None, :], g_ffn_pre[l][None, :], g_ffn_post[l][None, :])
        g_pre = g_mix_pre[l][None, :]

        qkv_s, rest_s = _project_rows(ys, g_pre, w_in_b, na, out_dtype=F32)
        cols, caches_t = [], []
        for g in range(N_GROUPS):
            length = caches[g].shape[2]
            caches_t.append(jnp.transpose(caches[g][l], (0, 2, 3, 4, 1)).reshape(db, 2, gw, length))
            qkv_g = jnp.concatenate([qkv_s[:, (j * N_GROUPS + g) * gw:(j * N_GROUPS + g + 1) * gw]
                                     for j in range(3)], axis=1)
            cols.append(jnp.pad(qkv_g, ((0, LANES - db), (0, 0))).T)

        q0, q1, q2, ra3, t0, t1, t2, new2, so2, sl2 = _project_qkv(
            yp, g_pre, w_in_b, na, cols[2], caches_t[2], tm=256, dil=DILATIONS[2])
        qkv_groups = (q0, q1, q2)
        rb3, new0, so0, sl0, new1, so1, sl1 = _project_rest(
            yp, g_pre, w_in_b, na, cols[:2], caches_t[:2], tm=256, dils=DILATIONS[:2])
        sample_parts = ((new0, so0, sl0), (new1, so1, sl1), (new2, so2, sl2))
        os_, ls_ = [], []
        for g in range(N_GROUPS):
            o, lse = _prompt_attention(qkv_groups[g], g, tq=512)
            os_.append(o)
            ls_.append(lse)
            tail_t = (t0, t1, t2)[g]
            w = tail_t.shape[-1]
            kv_p[g].append(jnp.transpose(tail_t.reshape(b, 2, HEADS, HEAD_DIM, w), (0, 4, 1, 2, 3)))
        yp, tail = _block_prompt(yp, ra3, rb3, os_, ls_, weights, tm=256, ff_chunk=1024)
        conv_p.append(tail[:, SUBLANES - (CONV_WIDTH - 1):, :])

        os_, ls_ = [], []
        for g in range(N_GROUPS):
            new_t, so, sl = sample_parts[g]
            length = new_t.shape[-1]
            os_.append(so[:, 0, :].reshape(db, gw))
            ls_.append(sl[:, 0, :].reshape(db, gw))
            new = jnp.transpose(new_t.reshape(db, 2, HEADS, HEAD_DIM, length), (0, 4, 1, 2, 3))
            kv_s[g].append(new)
        state = state_conv[l]
        ys, u_s = _block_sample(ys, qkv_s, rest_s, state.reshape(db, (CONV_WIDTH - 1) * d),
                                os_, ls_, weights, ff_chunk=1024)
        conv_s.append(jnp.stack([state[:, 1, :], u_s], axis=1))

    stack = lambda xs: jnp.stack(xs, axis=0)
    return (yp, ys.reshape(db, 1, d),
            stack(kv_p[0]), stack(kv_p[1]), stack(kv_p[2]), stack(conv_p),
            stack(kv_s[0]), stack(kv_s[1]), stack(kv_s[2]), stack(conv_s))
```

```python
import functools

import jax
import jax.numpy as jnp
from jax import lax
from jax.experimental import pallas as pl
from jax.experimental.pallas import tpu as pltpu

HEAD_DIM = 64
HEADS = 8
GROUP_WIDTH = HEADS * HEAD_DIM
N_GROUPS = 3
WINDOWS = (128, 512, 2048)
DILATIONS = (1, 4, 16)
BAND = 128
CONV_WIDTH = 3
EPS = 1e-6
NEG = -1e30
LOG2E = 1.4426950408889634
LANES = 128
SUBLANES = 8
VMEM_LIMIT = 56 * 1024 * 1024

F32 = jnp.float32
BF16 = jnp.bfloat16


def _rms(x, g):
    return (x * lax.rsqrt(jnp.mean(x * x, axis=-1, keepdims=True) + EPS)) * g


def _resident(shape):
    nd = len(shape)
    return pl.BlockSpec(shape, lambda *_: (0,) * nd, pipeline_mode=pl.Buffered(1))


def _unit_index(units_per_seq):
    u = pl.program_id(0) * pl.num_programs(1) + pl.program_id(1)
    return u // units_per_seq, u % units_per_seq


def _proj_qkv_kernel(x_ref, g_ref, w_ref, cols_ref, cache_ref,
                     q0_ref, q1_ref, q2_ref, ra_ref, t0_ref, t1_ref, t2_ref,
                     new_ref, so_ref, sl_ref, h_scr, hb_scr,
                     *, tn, dil, units_per_seq, tail_tiles):
    tm = x_ref.shape[0]
    h = _rms(x_ref[...], g_ref[...])
    hb_scr[0] = h.astype(BF16)
    for j in range(h_scr.shape[0]):
        h_scr[j] = h[:, j * LANES:(j + 1) * LANES]
    for g in range(1, N_GROUPS):
        r = DILATIONS[g]
        for c in range(r):
            for j in range(h_scr.shape[0]):
                hb_scr[g, c * (tm // r):(c + 1) * (tm // r), j * LANES:(j + 1) * LANES] = (
                    h_scr[j, pl.ds(c, tm // r, stride=r), :].astype(BF16))
    for g, out in enumerate((q0_ref, q1_ref, q2_ref)):
        r = DILATIONS[g]
        for j in range(3):
            c0 = (j * N_GROUPS + g) * tn
            res = jnp.dot(hb_scr[g], w_ref[:, c0:c0 + tn], preferred_element_type=F32)
            if j == 0:
                res = res * (HEAD_DIM ** -0.5 * LOG2E)
            out[:, :, j * tn:(j + 1) * tn] = res.reshape(r, tm // r, tn).astype(out.dtype)
    nq = 3 * N_GROUPS * tn
    for c0 in range(nq, w_ref.shape[-1], tn):
        res = jnp.dot(hb_scr[0], w_ref[:, c0:c0 + tn], preferred_element_type=F32)
        ra_ref[:, c0 - nq:c0 - nq + tn] = res.astype(ra_ref.dtype)
    seq, part = _unit_index(units_per_seq)
    _cache_unit(cols_ref, cache_ref, new_ref, so_ref, sl_ref, seq, part, dil)
    for g, (src, out) in enumerate(zip((q0_ref, q1_ref, q2_ref), (t0_ref, t1_ref, t2_ref))):
        lanes = out.shape[-1]
        r = DILATIONS[g]

        @pl.when(pl.program_id(1) >= pl.num_programs(1) - tail_tiles[g])
        def _():
            for j in (1, 2):
                for c in range(r):
                    for n in range(tn // LANES):
                        h_scr[n, pl.ds(c, tm // r, stride=r), :] = (
                            src[c, :, j * tn + n * LANES:j * tn + (n + 1) * LANES].astype(F32))
                res = jnp.concatenate([h_scr[n] for n in range(tn // LANES)], axis=1)
                out[j - 1] = res[tm - lanes:, :].T


def _proj_rest_kernel(x_ref, g_ref, *refs, tn, n_w, dils, units_per_seq):
    w_refs = refs[:n_w]
    cols0_ref, cache0_ref, cols1_ref, cache1_ref = refs[n_w:n_w + 4]
    rest_ref, new0_ref, so0_ref, sl0_ref, new1_ref, so1_ref, sl1_ref = refs[n_w + 4:]
    h = _rms(x_ref[...], g_ref[...]).astype(BF16)
    for k, w_ref in enumerate(w_refs):
        res = jnp.dot(h, w_ref[...], preferred_element_type=F32)
        rest_ref[:, k * tn:(k + 1) * tn] = res.astype(rest_ref.dtype)
    seq, part = _unit_index(units_per_seq)
    _cache_unit(cols0_ref, cache0_ref, new0_ref, so0_ref, sl0_ref, seq, part, dils[0])
    _cache_unit(cols1_ref, cache1_ref, new1_ref, so1_ref, sl1_ref, seq, part, dils[1])


def _cache_specs(cache_t, steps, n_i):
    db, _, gw, length = cache_t.shape
    ups = steps // db
    assert steps == db * ups and gw % ups == 0 and (gw // ups) % HEAD_DIM == 0, (steps, db)
    rows = gw // ups
    unit = lambda bb, i: bb * n_i + i
    blk = pl.BlockSpec((None, 2, rows, length),
                       lambda bb, i: (unit(bb, i) // ups, 0, unit(bb, i) % ups, 0))
    stat = pl.BlockSpec((None, SUBLANES, rows), lambda bb, i: (unit(bb, i), 0, 0))
    stat_shape = jax.ShapeDtypeStruct((steps, SUBLANES, rows), F32)
    return ups, blk, stat, stat_shape


def _project_qkv(x3, g, w, n_cols, cols, cache_t, *, tm, dil):
    b, s, d = x3.shape
    tn = GROUP_WIDTH
    n_i = s // tm
    extra = n_cols - 3 * N_GROUPS * tn
    w_spec = pl.BlockSpec((d, n_cols), lambda bb, i: (0, 0), pipeline_mode=pl.Buffered(1))
    ups, blk, stat, stat_shape = _cache_specs(cache_t, b * n_i, n_i)
    grp_shape = lambda r: jax.ShapeDtypeStruct((b, r, s // r, 3 * tn), BF16)
    grp_spec = lambda r: pl.BlockSpec((None, r, tm // r, 3 * tn), lambda bb, i: (bb, 0, i, 0))
    tails = [min(w, s) for w in WINDOWS]
    tail_tiles = tuple(max(w // tm, 1) for w in tails)
    assert all(w % min(tm, w) == 0 and min(tm, w) % LANES == 0 for w in tails)
    tail_shape = lambda w: jax.ShapeDtypeStruct((b, 2, tn, w), F32)
    tail_spec = lambda w, nt: pl.BlockSpec(
        (None, 2, tn, min(tm, w)), lambda bb, i: (bb, 0, 0, jnp.maximum(i - (n_i - nt), 0)))
    return pl.pallas_call(
        functools.partial(_proj_qkv_kernel, tn=tn, dil=dil, units_per_seq=ups,
                          tail_tiles=tail_tiles),
        out_shape=tuple(grp_shape(r) for r in DILATIONS)
        + (jax.ShapeDtypeStruct((b, s, extra), BF16),)
        + tuple(tail_shape(w) for w in tails)
        + (jax.ShapeDtypeStruct(cache_t.shape, cache_t.dtype), stat_shape, stat_shape),
        grid=(b, n_i),
        in_specs=[pl.BlockSpec((None, tm, d), lambda bb, i: (bb, i, 0)),
                  _resident((1, d)), w_spec, _resident(cols.shape), blk],
        out_specs=tuple(grp_spec(r) for r in DILATIONS)
        + (pl.BlockSpec((None, tm, extra), lambda bb, i: (bb, i, 0)),)
        + tuple(tail_spec(w, nt) for w, nt in zip(tails, tail_tiles))
        + (blk, stat, stat),
        scratch_shapes=[pltpu.VMEM((d // LANES, tm, LANES), F32),
                        pltpu.VMEM((N_GROUPS, tm, d), BF16)],
        compiler_params=pltpu.CompilerParams(dimension_semantics=("arbitrary", "arbitrary"),
                                             vmem_limit_bytes=VMEM_LIMIT),
        name="proj_qkv",
    )(x3, g, w, cols, cache_t)


def _project_rest(x3, g, w, first_col, cols01, caches01, *, tm, dils):
    b, s, d = x3.shape
    tn = GROUP_WIDTH
    n = w.shape[1] - first_col
    assert first_col % tn == 0 and n % tn == 0
    n_w = n // tn
    n_i = s // tm
    in_specs = [pl.BlockSpec((None, tm, d), lambda bb, i: (bb, i, 0)), _resident((1, d))]
    in_specs += [pl.BlockSpec((d, tn), lambda bb, i, k=k: (0, first_col // tn + k),
                              pipeline_mode=pl.Buffered(1)) for k in range(n_w)]
    out_shape = [jax.ShapeDtypeStruct((b, s, n), BF16)]
    out_specs = [pl.BlockSpec((None, tm, n), lambda bb, i: (bb, i, 0))]
    args = [x3, g] + [w] * n_w
    for cols, cache_t in zip(cols01, caches01):
        ups, blk, stat, stat_shape = _cache_specs(cache_t, b * n_i, n_i)
        in_specs += [_resident(cols.shape), blk]
        args += [cols, cache_t]
        out_shape += [jax.ShapeDtypeStruct(cache_t.shape, cache_t.dtype), stat_shape, stat_shape]
        out_specs += [blk, stat, stat]
    return pl.pallas_call(
        functools.partial(_proj_rest_kernel, tn=tn, n_w=n_w, dils=dils, units_per_seq=ups),
        out_shape=tuple(out_shape),
        grid=(b, n_i),
        in_specs=in_specs,
        out_specs=tuple(out_specs),
        compiler_params=pltpu.CompilerParams(dimension_semantics=("arbitrary", "arbitrary"),
                                             vmem_limit_bytes=VMEM_LIMIT),
        name="proj_rest",
    )(*args)


def _proj_rows_kernel(x_ref, g_ref, w_ref, a_ref, b_ref, *, tn):
    h = _rms(x_ref[...], g_ref[...]).astype(BF16)
    na = a_ref.shape[-1]
    for c0 in range(0, w_ref.shape[-1], tn):
        res = jnp.dot(h, w_ref[:, c0:c0 + tn], preferred_element_type=F32)
        if c0 < na:
            a_ref[:, c0:c0 + tn] = res.astype(a_ref.dtype)
        else:
            b_ref[:, c0 - na:c0 - na + tn] = res.astype(b_ref.dtype)


def _project_rows(x2d, g, w, n_cols, *, out_dtype):
    m, d = x2d.shape
    whole = lambda a: pl.BlockSpec(a.shape, lambda i: (0,) * a.ndim)
    out_shape = (jax.ShapeDtypeStruct((m, n_cols), out_dtype),
                 jax.ShapeDtypeStruct((m, w.shape[1] - n_cols), out_dtype))
    return pl.pallas_call(
        functools.partial(_proj_rows_kernel, tn=GROUP_WIDTH),
        out_shape=out_shape,
        grid=(1,),
        in_specs=[whole(x2d), _resident((1, d)), _resident(w.shape)],
        out_specs=tuple(whole(o) for o in out_shape),
        compiler_params=pltpu.CompilerParams(dimension_semantics=("arbitrary",),
                                             vmem_limit_bytes=VMEM_LIMIT),
        name="proj_sample",
    )(x2d, g, w)


def _attn_kernel(q_ref, kc_ref, vc_ref, kp_ref, vp_ref, o_ref, l_ref, kcat, vcat, *, tq):
    i = pl.program_id(2)
    kcat[0:BAND, :] = kp_ref[...]
    kcat[BAND:, :] = kc_ref[...]
    vcat[0:BAND, :] = vp_ref[...]
    vcat[BAND:, :] = vc_ref[...]

    qi = lax.broadcasted_iota(jnp.int32, (BAND, 2 * BAND), 0)
    ki = lax.broadcasted_iota(jnp.int32, (BAND, 2 * BAND), 1)
    rel = qi + BAND - ki
    band_ok = (rel >= 0) & (rel <= BAND)
    lane = lax.broadcasted_iota(jnp.int32, (BAND, LANES), 1)
    low = lane < HEAD_DIM

    for qb in range(tq // BAND):
        r0 = qb * BAND
        if qb == 0:
            valid = band_ok & ((ki >= BAND) | (i > 0))
        else:
            valid = band_ok
        valid2 = jnp.concatenate([valid, valid], axis=0)
        for hp in range(GROUP_WIDTH // LANES):
            c0 = hp * LANES
            q2 = q_ref[r0:r0 + BAND, c0:c0 + LANES]
            k2 = kcat[r0:r0 + 2 * BAND, c0:c0 + LANES]
            v2 = vcat[r0:r0 + 2 * BAND, c0:c0 + LANES]
            zero = jnp.zeros_like(q2)
            qm = jnp.concatenate([jnp.where(low, q2, zero), jnp.where(low, zero, q2)], axis=0)
            s = lax.dot_general(qm, k2, (((1,), (1,)), ((), ())), preferred_element_type=F32)
            s = jnp.where(valid2, s, NEG)
            m = jnp.max(s, axis=-1, keepdims=True)
            p = jnp.exp2(s - m)
            den = jnp.sum(p, axis=-1, keepdims=True)
            pv = jnp.dot(p.astype(BF16), v2, preferred_element_type=F32)
            o = pv * (1.0 / den)
            lse = m + jnp.log2(den)
            o_ref[r0:r0 + BAND, c0:c0 + LANES] = jnp.where(low, o[0:BAND], o[BAND:]).astype(o_ref.dtype)
            l_ref[r0:r0 + BAND, c0:c0 + LANES] = jnp.where(low, lse[0:BAND], lse[BAND:])


def _prompt_attention(qkv_g, g, *, tq):
    b, r, sr, _ = qkv_g.shape
    tq = min(tq, sr)
    per = tq // BAND

    def cur(j):
        return pl.BlockSpec((None, None, tq, GROUP_WIDTH), lambda bb, c, i: (bb, c, i, j))

    def prev(j):
        return pl.BlockSpec((None, None, BAND, GROUP_WIDTH),
                            lambda bb, c, i: (bb, c, jnp.maximum(i * per - 1, 0), j))

    out_spec = pl.BlockSpec((None, None, tq, GROUP_WIDTH), lambda bb, c, i: (bb, c, i, 0))
    return pl.pallas_call(
        functools.partial(_attn_kernel, tq=tq),
        out_shape=(jax.ShapeDtypeStruct((b, r, sr, GROUP_WIDTH), F32),
                   jax.ShapeDtypeStruct((b, r, sr, GROUP_WIDTH), F32)),
        grid=(b, r, sr // tq),
        in_specs=[cur(0), cur(1), cur(2), prev(1), prev(2)],
        out_specs=(out_spec, out_spec),
        scratch_shapes=[pltpu.VMEM((BAND + tq, GROUP_WIDTH), BF16),
                        pltpu.VMEM((BAND + tq, GROUP_WIDTH), BF16)],
        compiler_params=pltpu.CompilerParams(
            dimension_semantics=("arbitrary", "arbitrary", "arbitrary"),
            vmem_limit_bytes=VMEM_LIMIT),
        name=f"attn_g{g}",
    )(qkv_g, qkv_g, qkv_g, qkv_g, qkv_g)


def _merge_groups(os_, ls, exp=jnp.exp):
    m = jnp.maximum(jnp.maximum(ls[0], ls[1]), ls[2])
    es = [exp(l - m) for l in ls]
    num = es[0] * os_[0] + es[1] * os_[1] + es[2] * os_[2]
    return num / (es[0] + es[1] + es[2])


def _block_core(x, attn, c_b, conv_v, g_a, g_c, wao_ref, wco_ref, wo_ref, w1_ref, w2_ref,
                gpost, gpre, gfpost, *, ff_chunk):
    a = jnp.dot(attn.astype(BF16), wao_ref[...], preferred_element_type=F32)
    c = jnp.dot((c_b * conv_v).astype(BF16), wco_ref[...], preferred_element_type=F32)
    mix_in = jax.nn.sigmoid(g_a) * a + jax.nn.sigmoid(g_c) * c
    mix = jnp.dot(mix_in.astype(BF16), wo_ref[...], preferred_element_type=F32)
    x1 = x + _rms(mix, gpost)
    h2 = _rms(x1, gpre).astype(BF16)
    f = None
    for k0 in range(0, w1_ref.shape[1], ff_chunk):
        t = jnp.dot(h2, w1_ref[:, k0:k0 + ff_chunk], preferred_element_type=F32)
        t = jnp.square(jnp.maximum(t, 0.0)).astype(BF16)
        part = jnp.dot(t, w2_ref[k0:k0 + ff_chunk, :], preferred_element_type=F32)
        f = part if f is None else f + part
    return x1 + _rms(f, gfpost)


def _rest_columns(ra_ref, rb_ref, d):
    a0 = ra_ref.shape[-1] - 2 * d
    c_b = ra_ref[:, a0:a0 + d].astype(F32)
    u = ra_ref[:, a0 + d:a0 + 2 * d].astype(F32) * rb_ref[:, 0:d].astype(F32)
    g_a = rb_ref[:, d:2 * d].astype(F32)
    g_c = rb_ref[:, 2 * d:3 * d].astype(F32)
    return c_b, u, g_a, g_c


def _block_prompt_kernel(x_ref, ra_ref, rb_ref, hcc_ref, hch_ref, o0, l0, o1, l1, o2, l2,
                         convw_ref, wao_ref, wco_ref, wo_ref, w1_ref, w2_ref,
                         gpost_ref, gpre_ref, gfpost_ref, y_ref, tail_ref,
                         u_scr, nat_scr, *, ff_chunk):
    i = pl.program_id(1)
    tm, d = x_ref.shape
    vals = []
    for k, ref in enumerate((o0, l0, o1, l1, o2, l2)):
        r = ref.shape[0]
        if r == 1:
            vals.append(ref[0])
        else:
            nl = GROUP_WIDTH // LANES
            for c in range(r):
                for j in range(nl):
                    nat_scr[k * nl + j, pl.ds(c, tm // r, stride=r), :] = (
                        ref[c, :, j * LANES:(j + 1) * LANES])
            vals.append(jnp.concatenate([nat_scr[k * nl + j] for j in range(nl)], axis=1))
    attn = _merge_groups(vals[0::2], vals[1::2], exp=jnp.exp2)
    c_b, u, g_a, g_c = _rest_columns(ra_ref, rb_ref, d)
    hu = hcc_ref[...].astype(F32) * hch_ref[...].astype(F32)
    u_scr[0:SUBLANES, :] = jnp.where(i > 0, hu, jnp.zeros_like(hu))
    u_scr[SUBLANES:, :] = u
    w = convw_ref[...]
    conv_v = (w[0:1, :] * u_scr[SUBLANES - 2:SUBLANES - 2 + tm, :]
              + w[1:2, :] * u_scr[SUBLANES - 1:SUBLANES - 1 + tm, :]
              + w[2:3, :] * u)
    y_ref[...] = _block_core(x_ref[...], attn, c_b, conv_v, g_a, g_c,
                             wao_ref, wco_ref, wo_ref, w1_ref, w2_ref,
                             gpost_ref[...], gpre_ref[...], gfpost_ref[...], ff_chunk=ff_chunk)
    tail_ref[...] = u[tm - SUBLANES:tm, :]


def _block_sample_kernel(x_ref, ra_ref, rb_ref, st_ref, o0, l0, o1, l1, o2, l2,
                         convw_ref, wao_ref, wco_ref, wo_ref, w1_ref, w2_ref,
                         gpost_ref, gpre_ref, gfpost_ref, y_ref, u_ref, *, ff_chunk):
    d = x_ref.shape[1]
    attn = _merge_groups((o0[...], o1[...], o2[...]), (l0[...], l1[...], l2[...]))
    c_b, u, g_a, g_c = _rest_columns(ra_ref, rb_ref, d)
    w = convw_ref[...]
    conv_v = w[0:1, :] * st_ref[:, 0:d] + w[1:2, :] * st_ref[:, d:2 * d] + w[2:3, :] * u
    y_ref[...] = _block_core(x_ref[...], attn, c_b, conv_v, g_a, g_c,
                             wao_ref, wco_ref, wo_ref, w1_ref, w2_ref,
                             gpost_ref[...], gpre_ref[...], gfpost_ref[...], ff_chunk=ff_chunk)
    u_ref[...] = u


def _weight_specs(ws):
    return [_resident(w.shape) for w in ws]


def _block_prompt(x3, ra3, rb3, os_, ls_, weights, *, tm, ff_chunk):
    b, s, d = x3.shape
    assert ra3.shape[-1] == 2 * d and rb3.shape[-1] == 3 * d
    per = tm // SUBLANES
    row = lambda width: pl.BlockSpec((None, tm, width), lambda bb, i: (bb, i, 0))
    halo = lambda col: pl.BlockSpec((None, SUBLANES, d),
                                    lambda bb, i: (bb, jnp.maximum(i * per - 1, 0), col))
    in_specs = [row(d), row(2 * d), row(3 * d), halo(1), halo(0)]
    args = [x3, ra3, rb3, ra3, rb3]
    for o, l in zip(os_, ls_):
        r = o.shape[1]
        res = pl.BlockSpec((None, r, tm // r, GROUP_WIDTH), lambda bb, i: (bb, 0, i, 0))
        in_specs += [res, res]
        args += [o, l]
    in_specs += _weight_specs(weights)
    args += list(weights)
    n_slabs = 2 * N_GROUPS * GROUP_WIDTH // LANES
    y, tail = pl.pallas_call(
        functools.partial(_block_prompt_kernel, ff_chunk=ff_chunk),
        out_shape=(jax.ShapeDtypeStruct((b, s, d), F32),
                   jax.ShapeDtypeStruct((b, SUBLANES, d), F32)),
        grid=(b, s // tm),
        in_specs=in_specs,
        out_specs=(row(d), pl.BlockSpec((None, SUBLANES, d), lambda bb, i: (bb, 0, 0))),
        scratch_shapes=[pltpu.VMEM((SUBLANES + tm, d), F32),
                        pltpu.VMEM((n_slabs, tm, LANES), F32)],
        compiler_params=pltpu.CompilerParams(dimension_semantics=("arbitrary", "arbitrary"),
                                             vmem_limit_bytes=VMEM_LIMIT),
        name="block_prompt",
    )(*args)
    return y, tail


def _block_sample(x2, ra2, rb2, state2, os_, ls_, weights, *, ff_chunk):
    m, d = x2.shape
    full = lambda a: pl.BlockSpec(a.shape, lambda i: (0,) * a.ndim)
    args = [x2, ra2, rb2, state2]
    for o, l in zip(os_, ls_):
        args += [o, l]
    in_specs = [full(a) for a in args] + _weight_specs(weights)
    args += list(weights)
    return pl.pallas_call(
        functools.partial(_block_sample_kernel, ff_chunk=ff_chunk),
        out_shape=(jax.ShapeDtypeStruct((m, d), F32), jax.ShapeDtypeStruct((m, d), F32)),
        grid=(1,),
        in_specs=in_specs,
        out_specs=(pl.BlockSpec((m, d), lambda i: (0, 0)), pl.BlockSpec((m, d), lambda i: (0, 0))),
        compiler_params=pltpu.CompilerParams(dimension_semantics=("arbitrary",),
                                             vmem_limit_bytes=VMEM_LIMIT),
        name="block_sample",
    )(*args)


def _cache_unit(cols_ref, cache_ref, out_ref, o_ref, l_ref, seq, part, dil):
    _, nrows, length = cache_ref.shape
    scale = HEAD_DIM ** -0.5
    mine = lax.broadcasted_iota(jnp.int32, (1, LANES), 1) == seq
    base = pl.multiple_of(part * nrows, SUBLANES)

    def column(segment):
        blk = cols_ref[pl.ds(segment * GROUP_WIDTH + base, nrows), :]
        return jnp.sum(jnp.where(mine, blk, 0.0), axis=-1, keepdims=True)

    qc, knc, vnc = column(0), column(1), column(2)
    row = lax.broadcasted_iota(jnp.int32, (1, length), 1)
    dist = length - row
    on_grid = ((dist % dil) == 0) & (dist <= BAND * dil)
    o_cols, l_cols = [], []
    for h in range(nrows // HEAD_DIM):
        rows = slice(h * HEAD_DIM, (h + 1) * HEAD_DIM)
        s = jnp.sum(cache_ref[0, rows, :] * qc[rows], axis=0, keepdims=True) * scale
        s = jnp.where(on_grid, s, NEG)
        s_new = jnp.sum(qc[rows] * knc[rows], axis=0, keepdims=True) * scale
        m = jnp.maximum(jnp.max(s, axis=-1, keepdims=True), s_new)
        p = jnp.exp(s - m)
        p_new = jnp.exp(s_new - m)
        den = jnp.sum(p, axis=-1, keepdims=True) + p_new
        acc = jnp.sum(cache_ref[1, rows, :] * p, axis=-1, keepdims=True) + p_new * vnc[rows]
        o_cols.append(jnp.broadcast_to(acc / den, (HEAD_DIM, LANES)))
        l_cols.append(jnp.broadcast_to(m + jnp.log(den), (HEAD_DIM, LANES)))
    o_ref[...] = jnp.concatenate(o_cols, axis=0).T[0:SUBLANES]
    l_ref[...] = jnp.concatenate(l_cols, axis=0).T[0:SUBLANES]

    last = row == length - 1
    for kv, newc in ((0, knc), (1, vnc)):
        rolled = pltpu.roll(cache_ref[kv], length - 1, axis=1)
        out_ref[kv] = jnp.where(last, newc, rolled)


def kernel(x_prompt, x_sample, cache_kv_w128, cache_kv_w512, cache_kv_w2048, state_conv, w_in, conv_w, w_attn_o, w_conv_o, w_o, w_ff1, w_ff2, g_mix_pre, g_mix_post, g_ffn_pre, g_ffn_post):
    depth = w_in.shape[0]
    b, s, d = x_prompt.shape
    db, ds_, _ = x_sample.shape
    assert ds_ == 1 and d % LANES == 0 and db <= LANES
    caches = (cache_kv_w128, cache_kv_w512, cache_kv_w2048)
    gw = GROUP_WIDTH

    yp, ys = x_prompt, x_sample.reshape(db, d)
    kv_p = [[] for _ in range(N_GROUPS)]
    kv_s = [[] for _ in range(N_GROUPS)]
    conv_p, conv_s = [], []
    for l in range(depth):
        na = 3 * N_GROUPS * gw + 2 * d
        w_in_b = w_in[l].astype(BF16)
        weights = (conv_w[l], w_attn_o[l].astype(BF16), w_conv_o[l].astype(BF16),
                   w_o[l].astype(BF16), w_ff1[l].astype(BF16), w_ff2[l].astype(BF16),
                   g_mix_post[l][None, :], g_ffn_pre[l][None, :], g_ffn_post[l][None, :])
        g_pre = g_mix_pre[l][None, :]

        qkv_s, rest_s = _project_rows(ys, g_pre, w_in_b, na, out_dtype=F32)
        cols, caches_t = [], []
        for g in range(N_GROUPS):
            length = caches[g].shape[2]
            caches_t.append(jnp.transpose(caches[g][l], (0, 2, 3, 4, 1)).reshape(db, 2, gw, length))
            qkv_g = jnp.concatenate([qkv_s[:, (j * N_GROUPS + g) * gw:(j * N_GROUPS + g + 1) * gw]
                                     for j in range(3)], axis=1)
            cols.append(jnp.pad(qkv_g, ((0, LANES - db), (0, 0))).T)

        q0, q1, q2, ra3, t0, t1, t2, new2, so2, sl2 = _project_qkv(
            yp, g_pre, w_in_b, na, cols[2], caches_t[2], tm=256, dil=DILATIONS[2])
        qkv_groups = (q0, q1, q2)
        rb3, new0, so0, sl0, new1, so1, sl1 = _project_rest(
            yp, g_pre, w_in_b, na, cols[:2], caches_t[:2], tm=512, dils=DILATIONS[:2])
        sample_parts = ((new0, so0, sl0), (new1, so1, sl1), (new2, so2, sl2))
        os_, ls_ = [], []
        for g in range(N_GROUPS):
            o, lse = _prompt_attention(qkv_groups[g], g, tq=1024)
            os_.append(o)
            ls_.append(lse)
            tail_t = (t0, t1, t2)[g]
            w = tail_t.shape[-1]
            kv_p[g].append(jnp.transpose(tail_t.reshape(b, 2, HEADS, HEAD_DIM, w), (0, 4, 1, 2, 3)))
        yp, tail = _block_prompt(yp, ra3, rb3, os_, ls_, weights, tm=256, ff_chunk=1024)
        conv_p.append(tail[:, SUBLANES - (CONV_WIDTH - 1):, :])

        os_, ls_ = [], []
        for g in range(N_GROUPS):
            new_t, so, sl = sample_parts[g]
            length = new_t.shape[-1]
            os_.append(so[:, 0, :].reshape(db, gw))
            ls_.append(sl[:, 0, :].reshape(db, gw))
            new = jnp.transpose(new_t.reshape(db, 2, HEADS, HEAD_DIM, length), (0, 4, 1, 2, 3))
            kv_s[g].append(new)
        state = state_conv[l]
        ys, u_s = _block_sample(ys, qkv_s, rest_s, state.reshape(db, (CONV_WIDTH - 1) * d),
                                os_, ls_, weights, ff_chunk=1024)
        conv_s.append(jnp.stack([state[:, 1, :], u_s], axis=1))

    stack = lambda xs: jnp.stack(xs, axis=0)
    return (yp, ys.reshape(db, 1, d),
            stack(kv_p[0]), stack(kv_p[1]), stack(kv_p[2]), stack(conv_p),
            stack(kv_s[0]), stack(kv_s[1]), stack(kv_s[2]), stack(conv_s))
```

```python
import functools

import jax
import jax.numpy as jnp
from jax import lax
from jax.experimental import pallas as pl
from jax.experimental.pallas import tpu as pltpu

HEAD_DIM = 64
HEADS = 8
GROUP_WIDTH = HEADS * HEAD_DIM
N_GROUPS = 3
WINDOWS = (128, 512, 2048)
DILATIONS = (1, 4, 16)
BAND = 128
CONV_WIDTH = 3
EPS = 1e-6
NEG = -1e30
LOG2E = 1.4426950408889634
LANES = 128
SUBLANES = 8
VMEM_LIMIT = 56 * 1024 * 1024
VMEM_LIMIT_BLOCK = 62 * 1024 * 1024

F32 = jnp.float32
BF16 = jnp.bfloat16


def _rms(x, g):
    return (x * lax.rsqrt(jnp.mean(x * x, axis=-1, keepdims=True) + EPS)) * g


def _resident(shape):
    nd = len(shape)
    return pl.BlockSpec(shape, lambda *_: (0,) * nd, pipeline_mode=pl.Buffered(1))


def _unit_index(units_per_seq):
    u = pl.program_id(0) * pl.num_programs(1) + pl.program_id(1)
    return u // units_per_seq, u % units_per_seq


def _proj_qkv_kernel(x_ref, g_ref, w_ref,
                     q0_ref, q1_ref, q2_ref, ra_ref, t0_ref, t1_ref, t2_ref, h_scr, hb_scr,
                     *, tn, tail_tiles):
    tm = x_ref.shape[0]
    h = _rms(x_ref[...], g_ref[...])
    hb_scr[0] = h.astype(BF16)
    for j in range(h_scr.shape[0]):
        h_scr[j] = h[:, j * LANES:(j + 1) * LANES]
    for g in range(1, N_GROUPS):
        r = DILATIONS[g]
        for c in range(r):
            for j in range(h_scr.shape[0]):
                hb_scr[g, c * (tm // r):(c + 1) * (tm // r), j * LANES:(j + 1) * LANES] = (
                    h_scr[j, pl.ds(c, tm // r, stride=r), :].astype(BF16))
    for g, out in enumerate((q0_ref, q1_ref, q2_ref)):
        r = DILATIONS[g]
        for j in range(3):
            c0 = (j * N_GROUPS + g) * tn
            res = jnp.dot(hb_scr[g], w_ref[:, c0:c0 + tn], preferred_element_type=F32)
            if j == 0:
                res = res * (HEAD_DIM ** -0.5 * LOG2E)
            out[:, :, j * tn:(j + 1) * tn] = res.reshape(r, tm // r, tn).astype(out.dtype)
    nq = 3 * N_GROUPS * tn
    for c0 in range(nq, w_ref.shape[-1], tn):
        res = jnp.dot(hb_scr[0], w_ref[:, c0:c0 + tn], preferred_element_type=F32)
        ra_ref[:, c0 - nq:c0 - nq + tn] = res.astype(ra_ref.dtype)
    for g, (src, out) in enumerate(zip((q0_ref, q1_ref, q2_ref), (t0_ref, t1_ref, t2_ref))):
        lanes = out.shape[-1]
        r = DILATIONS[g]

        @pl.when(pl.program_id(1) >= pl.num_programs(1) - tail_tiles[g])
        def _():
            for j in (1, 2):
                for c in range(r):
                    for n in range(tn // LANES):
                        h_scr[n, pl.ds(c, tm // r, stride=r), :] = (
                            src[c, :, j * tn + n * LANES:j * tn + (n + 1) * LANES].astype(F32))
                res = jnp.concatenate([h_scr[n] for n in range(tn // LANES)], axis=1)
                out[j - 1] = res[tm - lanes:, :].T


def _proj_rest_kernel(x_ref, g_ref, *refs, tn, n_w, dils, units_per_seq):
    w_refs = refs[:n_w]
    cols0_ref, cache0_ref, cols1_ref, cache1_ref = refs[n_w:n_w + 4]
    rest_ref, new0_ref, so0_ref, sl0_ref, new1_ref, so1_ref, sl1_ref = refs[n_w + 4:]
    h = _rms(x_ref[...], g_ref[...]).astype(BF16)
    for k, w_ref in enumerate(w_refs):
        res = jnp.dot(h, w_ref[...], preferred_element_type=F32)
        rest_ref[:, k * tn:(k + 1) * tn] = res.astype(rest_ref.dtype)
    seq, part = _unit_index(units_per_seq)
    _cache_unit(cols0_ref, cache0_ref, new0_ref, so0_ref, sl0_ref, seq, part, dils[0])
    _cache_unit(cols1_ref, cache1_ref, new1_ref, so1_ref, sl1_ref, seq, part, dils[1])


def _cache_specs(cache_t, steps, n_i):
    db, _, gw, length = cache_t.shape
    ups = steps // db
    assert steps == db * ups and gw % ups == 0 and (gw // ups) % HEAD_DIM == 0, (steps, db)
    rows = gw // ups
    unit = lambda bb, i: bb * n_i + i
    blk = pl.BlockSpec((None, 2, rows, length),
                       lambda bb, i: (unit(bb, i) // ups, 0, unit(bb, i) % ups, 0))
    stat = pl.BlockSpec((None, SUBLANES, rows), lambda bb, i: (unit(bb, i), 0, 0))
    stat_shape = jax.ShapeDtypeStruct((steps, SUBLANES, rows), F32)
    return ups, blk, stat, stat_shape


def _project_qkv(x3, g, w, n_cols, *, tm):
    b, s, d = x3.shape
    tn = GROUP_WIDTH
    n_i = s // tm
    extra = n_cols - 3 * N_GROUPS * tn
    w_spec = pl.BlockSpec((d, n_cols), lambda bb, i: (0, 0), pipeline_mode=pl.Buffered(1))
    grp_shape = lambda r: jax.ShapeDtypeStruct((b, r, s // r, 3 * tn), BF16)
    grp_spec = lambda r: pl.BlockSpec((None, r, tm // r, 3 * tn), lambda bb, i: (bb, 0, i, 0))
    tails = [min(w, s) for w in WINDOWS]
    tail_tiles = tuple(max(w // tm, 1) for w in tails)
    assert all(w % min(tm, w) == 0 and min(tm, w) % LANES == 0 for w in tails)
    tail_shape = lambda w: jax.ShapeDtypeStruct((b, 2, tn, w), F32)
    tail_spec = lambda w, nt: pl.BlockSpec(
        (None, 2, tn, min(tm, w)), lambda bb, i: (bb, 0, 0, jnp.maximum(i - (n_i - nt), 0)))
    return pl.pallas_call(
        functools.partial(_proj_qkv_kernel, tn=tn, tail_tiles=tail_tiles),
        out_shape=tuple(grp_shape(r) for r in DILATIONS)
        + (jax.ShapeDtypeStruct((b, s, extra), BF16),)
        + tuple(tail_shape(w) for w in tails),
        grid=(b, n_i),
        in_specs=[pl.BlockSpec((None, tm, d), lambda bb, i: (bb, i, 0)),
                  _resident((1, d)), w_spec],
        out_specs=tuple(grp_spec(r) for r in DILATIONS)
        + (pl.BlockSpec((None, tm, extra), lambda bb, i: (bb, i, 0)),)
        + tuple(tail_spec(w, nt) for w, nt in zip(tails, tail_tiles)),
        scratch_shapes=[pltpu.VMEM((d // LANES, tm, LANES), F32),
                        pltpu.VMEM((N_GROUPS, tm, d), BF16)],
        compiler_params=pltpu.CompilerParams(dimension_semantics=("arbitrary", "arbitrary"),
                                             vmem_limit_bytes=VMEM_LIMIT),
        name="proj_qkv",
    )(x3, g, w)


def _project_rest(x3, g, w, first_col, cols01, caches01, *, tm, dils):
    b, s, d = x3.shape
    tn = GROUP_WIDTH
    n = w.shape[1] - first_col
    assert first_col % tn == 0 and n % tn == 0
    n_w = n // tn
    n_i = s // tm
    in_specs = [pl.BlockSpec((None, tm, d), lambda bb, i: (bb, i, 0)), _resident((1, d))]
    in_specs += [pl.BlockSpec((d, tn), lambda bb, i, k=k: (0, first_col // tn + k),
                              pipeline_mode=pl.Buffered(1)) for k in range(n_w)]
    out_shape = [jax.ShapeDtypeStruct((b, s, n), BF16)]
    out_specs = [pl.BlockSpec((None, tm, n), lambda bb, i: (bb, i, 0))]
    args = [x3, g] + [w] * n_w
    for cols, cache_t in zip(cols01, caches01):
        ups, blk, stat, stat_shape = _cache_specs(cache_t, b * n_i, n_i)
        in_specs += [_resident(cols.shape), blk]
        args += [cols, cache_t]
        out_shape += [jax.ShapeDtypeStruct(cache_t.shape, cache_t.dtype), stat_shape, stat_shape]
        out_specs += [blk, stat, stat]
    return pl.pallas_call(
        functools.partial(_proj_rest_kernel, tn=tn, n_w=n_w, dils=dils, units_per_seq=ups),
        out_shape=tuple(out_shape),
        grid=(b, n_i),
        in_specs=in_specs,
        out_specs=tuple(out_specs),
        compiler_params=pltpu.CompilerParams(dimension_semantics=("arbitrary", "arbitrary"),
                                             vmem_limit_bytes=VMEM_LIMIT),
        name="proj_rest",
    )(*args)


def _proj_rows_kernel(x_ref, g_ref, w_ref, a_ref, b_ref, *, tn):
    h = _rms(x_ref[...], g_ref[...]).astype(BF16)
    na = a_ref.shape[-1]
    for c0 in range(0, w_ref.shape[-1], tn):
        res = jnp.dot(h, w_ref[:, c0:c0 + tn], preferred_element_type=F32)
        if c0 < na:
            a_ref[:, c0:c0 + tn] = res.astype(a_ref.dtype)
        else:
            b_ref[:, c0 - na:c0 - na + tn] = res.astype(b_ref.dtype)


def _project_rows(x2d, g, w, n_cols, *, out_dtype):
    m, d = x2d.shape
    whole = lambda a: pl.BlockSpec(a.shape, lambda i: (0,) * a.ndim)
    out_shape = (jax.ShapeDtypeStruct((m, n_cols), out_dtype),
                 jax.ShapeDtypeStruct((m, w.shape[1] - n_cols), out_dtype))
    return pl.pallas_call(
        functools.partial(_proj_rows_kernel, tn=GROUP_WIDTH),
        out_shape=out_shape,
        grid=(1,),
        in_specs=[whole(x2d), _resident((1, d)), _resident(w.shape)],
        out_specs=tuple(whole(o) for o in out_shape),
        compiler_params=pltpu.CompilerParams(dimension_semantics=("arbitrary",),
                                             vmem_limit_bytes=VMEM_LIMIT),
        name="proj_sample",
    )(x2d, g, w)


def _attn_kernel(q_ref, kc_ref, vc_ref, kp_ref, vp_ref, o_ref, l_ref, kcat, vcat, *, tq):
    i = pl.program_id(2)
    kcat[0:BAND, :] = kp_ref[...]
    kcat[BAND:, :] = kc_ref[...]
    vcat[0:BAND, :] = vp_ref[...]
    vcat[BAND:, :] = vc_ref[...]

    qi = lax.broadcasted_iota(jnp.int32, (BAND, 2 * BAND), 0)
    ki = lax.broadcasted_iota(jnp.int32, (BAND, 2 * BAND), 1)
    rel = qi + BAND - ki
    band_ok = (rel >= 0) & (rel <= BAND)
    lane = lax.broadcasted_iota(jnp.int32, (BAND, LANES), 1)
    low = lane < HEAD_DIM

    for qb in range(tq // BAND):
        r0 = qb * BAND
        if qb == 0:
            valid = band_ok & ((ki >= BAND) | (i > 0))
        else:
            valid = band_ok
        valid2 = jnp.concatenate([valid, valid], axis=0)
        for hp in range(GROUP_WIDTH // LANES):
            c0 = hp * LANES
            q2 = q_ref[r0:r0 + BAND, c0:c0 + LANES]
            k2 = kcat[r0:r0 + 2 * BAND, c0:c0 + LANES]
            v2 = vcat[r0:r0 + 2 * BAND, c0:c0 + LANES]
            zero = jnp.zeros_like(q2)
            qm = jnp.concatenate([jnp.where(low, q2, zero), jnp.where(low, zero, q2)], axis=0)
            s = lax.dot_general(qm, k2, (((1,), (1,)), ((), ())), preferred_element_type=F32)
            s = jnp.where(valid2, s, NEG)
            m = jnp.max(s, axis=-1, keepdims=True)
            p = jnp.exp2(s - m)
            den = jnp.sum(p, axis=-1, keepdims=True)
            pv = jnp.dot(p.astype(BF16), v2, preferred_element_type=F32)
            o = pv * (1.0 / den)
            lse = m + jnp.log2(den)
            o_ref[r0:r0 + BAND, c0:c0 + LANES] = jnp.where(low, o[0:BAND], o[BAND:]).astype(o_ref.dtype)
            l_ref[r0:r0 + BAND, c0:c0 + LANES] = jnp.where(low, lse[0:BAND], lse[BAND:])


def _prompt_attention(qkv_g, g, *, tq):
    b, r, sr, _ = qkv_g.shape
    tq = min(tq, sr)
    per = tq // BAND

    def cur(j):
        return pl.BlockSpec((None, None, tq, GROUP_WIDTH), lambda bb, c, i: (bb, c, i, j))

    def prev(j):
        return pl.BlockSpec((None, None, BAND, GROUP_WIDTH),
                            lambda bb, c, i: (bb, c, jnp.maximum(i * per - 1, 0), j))

    out_spec = pl.BlockSpec((None, None, tq, GROUP_WIDTH), lambda bb, c, i: (bb, c, i, 0))
    return pl.pallas_call(
        functools.partial(_attn_kernel, tq=tq),
        out_shape=(jax.ShapeDtypeStruct((b, r, sr, GROUP_WIDTH), F32),
                   jax.ShapeDtypeStruct((b, r, sr, GROUP_WIDTH), F32)),
        grid=(b, r, sr // tq),
        in_specs=[cur(0), cur(1), cur(2), prev(1), prev(2)],
        out_specs=(out_spec, out_spec),
        scratch_shapes=[pltpu.VMEM((BAND + tq, GROUP_WIDTH), BF16),
                        pltpu.VMEM((BAND + tq, GROUP_WIDTH), BF16)],
        compiler_params=pltpu.CompilerParams(
            dimension_semantics=("arbitrary", "arbitrary", "arbitrary"),
            vmem_limit_bytes=VMEM_LIMIT),
        name=f"attn_g{g}",
    )(qkv_g, qkv_g, qkv_g, qkv_g, qkv_g)


def _merge_groups(os_, ls, exp=jnp.exp):
    m = jnp.maximum(jnp.maximum(ls[0], ls[1]), ls[2])
    es = [exp(l - m) for l in ls]
    num = es[0] * os_[0] + es[1] * os_[1] + es[2] * os_[2]
    return num / (es[0] + es[1] + es[2])


def _block_core(x, attn, c_b, conv_v, g_a, g_c, wao_ref, wco_ref, wo_ref, w1_ref, w2_ref,
                gpost, gpre, gfpost, *, ff_chunk, row_parts=1, mid_mlp=None):
    n = x.shape[0] // row_parts
    rows = [slice(p * n, (p + 1) * n) for p in range(row_parts)]
    dot = functools.partial(jnp.dot, preferred_element_type=F32)
    a = [dot(attn[r].astype(BF16), wao_ref[...]) for r in rows]
    c = [dot((c_b[r] * conv_v[r]).astype(BF16), wco_ref[...]) for r in rows]
    mix = [dot((jax.nn.sigmoid(g_a[r]) * a[p] + jax.nn.sigmoid(g_c[r]) * c[p]).astype(BF16),
               wo_ref[...]) for p, r in enumerate(rows)]
    x1 = [x[r] + _rms(mix[p], gpost) for p, r in enumerate(rows)]
    h2 = [_rms(v, gpre).astype(BF16) for v in x1]
    f = [None] * row_parts
    for k0 in range(0, w1_ref.shape[1], ff_chunk):
        if mid_mlp is not None and k0 == ff_chunk:
            mid_mlp()
        t = [dot(h, w1_ref[:, k0:k0 + ff_chunk]) for h in h2]
        t = [jnp.square(jnp.maximum(v, 0.0)).astype(BF16) for v in t]
        part = [dot(v, w2_ref[k0:k0 + ff_chunk, :]) for v in t]
        f = [q if acc is None else acc + q for acc, q in zip(f, part)]
    return jnp.concatenate([x1[p] + _rms(f[p], gfpost) for p in range(row_parts)], axis=0)


def _rest_columns(ra_ref, rb_ref, d):
    a0 = ra_ref.shape[-1] - 2 * d
    c_b = ra_ref[:, a0:a0 + d].astype(F32)
    u = ra_ref[:, a0 + d:a0 + 2 * d].astype(F32) * rb_ref[:, 0:d].astype(F32)
    g_a = rb_ref[:, d:2 * d].astype(F32)
    g_c = rb_ref[:, 2 * d:3 * d].astype(F32)
    return c_b, u, g_a, g_c


def _block_prompt_kernel(x_ref, ra_ref, rb_ref, hcc_ref, hch_ref, o0, l0, o1, l1, o2, l2,
                         cols_ref, cache_ref,
                         convw_ref, wao_ref, wco_ref, wo_ref, w1_ref, w2_ref,
                         gpost_ref, gpre_ref, gfpost_ref, y_ref, tail_ref, new_ref, so_ref, sl_ref,
                         u_scr, nat_scr, *, ff_chunk, dil, units_per_seq):
    def window_unit():
        seq, part = _unit_index(units_per_seq)
        _cache_unit(cols_ref, cache_ref, new_ref, so_ref, sl_ref, seq, part, dil)

    i = pl.program_id(1)
    tm, d = x_ref.shape
    vals = []
    slab = 0
    for ref in (o0, l0, o1, l1, o2, l2):
        r = ref.shape[0]
        if r == 1:
            vals.append(ref[0])
        else:
            nl = GROUP_WIDTH // LANES
            for c in range(r):
                for j in range(nl):
                    nat_scr[slab + j, pl.ds(c, tm // r, stride=r), :] = (
                        ref[c, :, j * LANES:(j + 1) * LANES])
            vals.append(jnp.concatenate([nat_scr[slab + j] for j in range(nl)], axis=1))
            slab += nl
    attn = _merge_groups(vals[0::2], vals[1::2], exp=jnp.exp2)
    c_b, u, g_a, g_c = _rest_columns(ra_ref, rb_ref, d)
    hu = hcc_ref[...].astype(F32) * hch_ref[...].astype(F32)
    u_scr[0:SUBLANES, :] = jnp.where(i > 0, hu, jnp.zeros_like(hu))
    u_scr[SUBLANES:, :] = u
    w = convw_ref[...]
    conv_v = (w[0:1, :] * u_scr[SUBLANES - 2:SUBLANES - 2 + tm, :]
              + w[1:2, :] * u_scr[SUBLANES - 1:SUBLANES - 1 + tm, :]
              + w[2:3, :] * u)
    y_ref[...] = _block_core(x_ref[...], attn, c_b, conv_v, g_a, g_c,
                             wao_ref, wco_ref, wo_ref, w1_ref, w2_ref,
                             gpost_ref[...], gpre_ref[...], gfpost_ref[...], ff_chunk=ff_chunk,
                             row_parts=2, mid_mlp=window_unit)
    tail_ref[...] = u[tm - SUBLANES:tm, :]


def _block_sample_kernel(x_ref, ra_ref, rb_ref, st_ref, o0, l0, o1, l1, o2, l2,
                         convw_ref, wao_ref, wco_ref, wo_ref, w1_ref, w2_ref,
                         gpost_ref, gpre_ref, gfpost_ref, y_ref, u_ref, *, ff_chunk):
    d = x_ref.shape[1]
    attn = _merge_groups((o0[...], o1[...], o2[...]), (l0[...], l1[...], l2[...]))
    c_b, u, g_a, g_c = _rest_columns(ra_ref, rb_ref, d)
    w = convw_ref[...]
    conv_v = w[0:1, :] * st_ref[:, 0:d] + w[1:2, :] * st_ref[:, d:2 * d] + w[2:3, :] * u
    y_ref[...] = _block_core(x_ref[...], attn, c_b, conv_v, g_a, g_c,
                             wao_ref, wco_ref, wo_ref, w1_ref, w2_ref,
                             gpost_ref[...], gpre_ref[...], gfpost_ref[...], ff_chunk=ff_chunk)
    u_ref[...] = u


def _weight_specs(ws):
    return [_resident(w.shape) for w in ws]


def _block_prompt(x3, ra3, rb3, os_, ls_, cols, cache_t, weights, *, tm, ff_chunk, dil):
    b, s, d = x3.shape
    assert ra3.shape[-1] == 2 * d and rb3.shape[-1] == 3 * d
    per = tm // SUBLANES
    row = lambda width: pl.BlockSpec((None, tm, width), lambda bb, i: (bb, i, 0))
    halo = lambda col: pl.BlockSpec((None, SUBLANES, d),
                                    lambda bb, i: (bb, jnp.maximum(i * per - 1, 0), col))
    in_specs = [row(d), row(2 * d), row(3 * d), halo(1), halo(0)]
    args = [x3, ra3, rb3, ra3, rb3]
    for o, l in zip(os_, ls_):
        r = o.shape[1]
        res = pl.BlockSpec((None, r, tm // r, GROUP_WIDTH), lambda bb, i: (bb, 0, i, 0))
        in_specs += [res, res]
        args += [o, l]
    n_i = s // tm
    ups, blk, stat, stat_shape = _cache_specs(cache_t, b * n_i, n_i)
    in_specs += [_resident(cols.shape), blk]
    args += [cols, cache_t]
    in_specs += _weight_specs(weights)
    args += list(weights)
    n_slabs = 2 * sum(o.shape[1] > 1 for o in os_) * GROUP_WIDTH // LANES
    return pl.pallas_call(
        functools.partial(_block_prompt_kernel, ff_chunk=ff_chunk, dil=dil, units_per_seq=ups),
        out_shape=(jax.ShapeDtypeStruct((b, s, d), F32),
                   jax.ShapeDtypeStruct((b, SUBLANES, d), F32),
                   jax.ShapeDtypeStruct(cache_t.shape, cache_t.dtype), stat_shape, stat_shape),
        grid=(b, n_i),
        in_specs=in_specs,
        out_specs=(row(d), pl.BlockSpec((None, SUBLANES, d), lambda bb, i: (bb, 0, 0)),
                   blk, stat, stat),
        scratch_shapes=[pltpu.VMEM((SUBLANES + tm, d), F32),
                        pltpu.VMEM((n_slabs, tm, LANES), F32)],
        compiler_params=pltpu.CompilerParams(dimension_semantics=("arbitrary", "arbitrary"),
                                             vmem_limit_bytes=VMEM_LIMIT_BLOCK),
        name="block_prompt",
    )(*args)


def _block_sample(x2, ra2, rb2, state2, os_, ls_, weights, *, ff_chunk):
    m, d = x2.shape
    full = lambda a: pl.BlockSpec(a.shape, lambda i: (0,) * a.ndim)
    args = [x2, ra2, rb2, state2]
    for o, l in zip(os_, ls_):
        args += [o, l]
    in_specs = [full(a) for a in args] + _weight_specs(weights)
    args += list(weights)
    return pl.pallas_call(
        functools.partial(_block_sample_kernel, ff_chunk=ff_chunk),
        out_shape=(jax.ShapeDtypeStruct((m, d), F32), jax.ShapeDtypeStruct((m, d), F32)),
        grid=(1,),
        in_specs=in_specs,
        out_specs=(pl.BlockSpec((m, d), lambda i: (0, 0)), pl.BlockSpec((m, d), lambda i: (0, 0))),
        compiler_params=pltpu.CompilerParams(dimension_semantics=("arbitrary",),
                                             vmem_limit_bytes=VMEM_LIMIT),
        name="block_sample",
    )(*args)


def _cache_unit(cols_ref, cache_ref, out_ref, o_ref, l_ref, seq, part, dil):
    _, nrows, length = cache_ref.shape
    scale = HEAD_DIM ** -0.5
    mine = lax.broadcasted_iota(jnp.int32, (1, LANES), 1) == seq
    base = pl.multiple_of(part * nrows, SUBLANES)

    def column(segment):
        blk = cols_ref[pl.ds(segment * GROUP_WIDTH + base, nrows), :]
        return jnp.sum(jnp.where(mine, blk, 0.0), axis=-1, keepdims=True)

    qc, knc, vnc = column(0), column(1), column(2)
    row = lax.broadcasted_iota(jnp.int32, (1, length), 1)
    dist = length - row
    on_grid = ((dist % dil) == 0) & (dist <= BAND * dil)
    o_cols, l_cols = [], []
    for h in range(nrows // HEAD_DIM):
        rows = slice(h * HEAD_DIM, (h + 1) * HEAD_DIM)
        s = jnp.sum(cache_ref[0, rows, :] * qc[rows], axis=0, keepdims=True) * scale
        s = jnp.where(on_grid, s, NEG)
        s_new = jnp.sum(qc[rows] * knc[rows], axis=0, keepdims=True) * scale
        m = jnp.maximum(jnp.max(s, axis=-1, keepdims=True), s_new)
        p = jnp.exp(s - m)
        p_new = jnp.exp(s_new - m)
        den = jnp.sum(p, axis=-1, keepdims=True) + p_new
        acc = jnp.sum(cache_ref[1, rows, :] * p, axis=-1, keepdims=True) + p_new * vnc[rows]
        o_cols.append(jnp.broadcast_to(acc / den, (HEAD_DIM, LANES)))
        l_cols.append(jnp.broadcast_to(m + jnp.log(den), (HEAD_DIM, LANES)))
    o_ref[...] = jnp.concatenate(o_cols, axis=0).T[0:SUBLANES]
    l_ref[...] = jnp.concatenate(l_cols, axis=0).T[0:SUBLANES]

    last = row == length - 1
    for kv, newc in ((0, knc), (1, vnc)):
        rolled = pltpu.roll(cache_ref[kv], length - 1, axis=1)
        out_ref[kv] = jnp.where(last, newc, rolled)


def kernel(x_prompt, x_sample, cache_kv_w128, cache_kv_w512, cache_kv_w2048, state_conv, w_in, conv_w, w_attn_o, w_conv_o, w_o, w_ff1, w_ff2, g_mix_pre, g_mix_post, g_ffn_pre, g_ffn_post):
    depth = w_in.shape[0]
    b, s, d = x_prompt.shape
    db, ds_, _ = x_sample.shape
    assert ds_ == 1 and d % LANES == 0 and db <= LANES
    caches = (cache_kv_w128, cache_kv_w512, cache_kv_w2048)
    gw = GROUP_WIDTH

    yp, ys = x_prompt, x_sample.reshape(db, d)
    kv_p = [[] for _ in range(N_GROUPS)]
    kv_s = [[] for _ in range(N_GROUPS)]
    conv_p, conv_s = [], []
    for l in range(depth):
        na = 3 * N_GROUPS * gw + 2 * d
        w_in_b = w_in[l].astype(BF16)
        weights = (conv_w[l], w_attn_o[l].astype(BF16), w_conv_o[l].astype(BF16),
                   w_o[l].astype(BF16), w_ff1[l].astype(BF16), w_ff2[l].astype(BF16),
                   g_mix_post[l][None, :], g_ffn_pre[l][None, :], g_ffn_post[l][None, :])
        g_pre = g_mix_pre[l][None, :]

        qkv_s, rest_s = _project_rows(ys, g_pre, w_in_b, na, out_dtype=F32)
        cols, caches_t = [], []
        for g in range(N_GROUPS):
            length = caches[g].shape[2]
            caches_t.append(jnp.transpose(caches[g][l], (0, 2, 3, 4, 1)).reshape(db, 2, gw, length))
            qkv_g = jnp.concatenate([qkv_s[:, (j * N_GROUPS + g) * gw:(j * N_GROUPS + g + 1) * gw]
                                     for j in range(3)], axis=1)
            cols.append(jnp.pad(qkv_g, ((0, LANES - db), (0, 0))).T)

        q0, q1, q2, ra3, t0, t1, t2 = _project_qkv(yp, g_pre, w_in_b, na, tm=256)
        qkv_groups = (q0, q1, q2)
        rb3, new0, so0, sl0, new1, so1, sl1 = _project_rest(
            yp, g_pre, w_in_b, na, cols[:2], caches_t[:2], tm=512, dils=DILATIONS[:2])
        os_, ls_ = [], []
        for g in range(N_GROUPS):
            o, lse = _prompt_attention(qkv_groups[g], g, tq=1024)
            os_.append(o)
            ls_.append(lse)
            tail_t = (t0, t1, t2)[g]
            w = tail_t.shape[-1]
            kv_p[g].append(jnp.transpose(tail_t.reshape(b, 2, HEADS, HEAD_DIM, w), (0, 4, 1, 2, 3)))
        yp, tail, new2, so2, sl2 = _block_prompt(yp, ra3, rb3, os_, ls_, cols[2], caches_t[2],
                                                 weights, tm=256, ff_chunk=1024, dil=DILATIONS[2])
        conv_p.append(tail[:, SUBLANES - (CONV_WIDTH - 1):, :])
        sample_parts = ((new0, so0, sl0), (new1, so1, sl1), (new2, so2, sl2))

        os_, ls_ = [], []
        for g in range(N_GROUPS):
            new_t, so, sl = sample_parts[g]
            length = new_t.shape[-1]
            os_.append(so[:, 0, :].reshape(db, gw))
            ls_.append(sl[:, 0, :].reshape(db, gw))
            new = jnp.transpose(new_t.reshape(db, 2, HEADS, HEAD_DIM, length), (0, 4, 1, 2, 3))
            kv_s[g].append(new)
        state = state_conv[l]
        ys, u_s = _block_sample(ys, qkv_s, rest_s, state.reshape(db, (CONV_WIDTH - 1) * d),
                                os_, ls_, weights, ff_chunk=1024)
        conv_s.append(jnp.stack([state[:, 1, :], u_s], axis=1))

    stack = lambda xs: jnp.stack(xs, axis=0)
    return (yp, ys.reshape(db, 1, d),
            stack(kv_p[0]), stack(kv_p[1]), stack(kv_p[2]), stack(conv_p),
            stack(kv_s[0]), stack(kv_s[1]), stack(kv_s[2]), stack(conv_s))
```

```python
import functools

import jax
import jax.numpy as jnp
from jax import lax
from jax.experimental import pallas as pl
from jax.experimental.pallas import tpu as pltpu

HEAD_DIM = 64
HEADS = 8
GROUP_WIDTH = HEADS * HEAD_DIM
N_GROUPS = 3
WINDOWS = (128, 512, 2048)
DILATIONS = (1, 4, 16)
BAND = 128
CONV_WIDTH = 3
EPS = 1e-6
NEG = -1e30
LOG2E = 1.4426950408889634
LANES = 128
SUBLANES = 8
VMEM_LIMIT = 56 * 1024 * 1024
VMEM_LIMIT_BLOCK = 62 * 1024 * 1024

F32 = jnp.float32
BF16 = jnp.bfloat16


def _rms(x, g):
    return (x * lax.rsqrt(jnp.mean(x * x, axis=-1, keepdims=True) + EPS)) * g


def _resident(shape):
    nd = len(shape)
    return pl.BlockSpec(shape, lambda *_: (0,) * nd, pipeline_mode=pl.Buffered(1))


def _unit_index(units_per_seq):
    u = pl.program_id(0) * pl.num_programs(1) + pl.program_id(1)
    return u // units_per_seq, u % units_per_seq


def _proj_qkv_kernel(x_ref, g_ref, w_ref,
                     q0_ref, q1_ref, q2_ref, ra_ref, t0_ref, t1_ref, t2_ref, h_scr, hb_scr,
                     *, tn, tail_tiles):
    tm = x_ref.shape[0]
    h = _rms(x_ref[...], g_ref[...])
    hb_scr[0] = h.astype(BF16)
    for j in range(h_scr.shape[0]):
        h_scr[j] = h[:, j * LANES:(j + 1) * LANES]
    for g in range(1, N_GROUPS):
        r = DILATIONS[g]
        for c in range(r):
            for j in range(h_scr.shape[0]):
                hb_scr[g, c * (tm // r):(c + 1) * (tm // r), j * LANES:(j + 1) * LANES] = (
                    h_scr[j, pl.ds(c, tm // r, stride=r), :].astype(BF16))
    for g, out in enumerate((q0_ref, q1_ref, q2_ref)):
        r = DILATIONS[g]
        for j in range(3):
            c0 = (j * N_GROUPS + g) * tn
            res = jnp.dot(hb_scr[g], w_ref[:, c0:c0 + tn], preferred_element_type=F32)
            if j == 0:
                res = res * (HEAD_DIM ** -0.5 * LOG2E)
            out[:, :, j * tn:(j + 1) * tn] = res.reshape(r, tm // r, tn).astype(out.dtype)
    nq = 3 * N_GROUPS * tn
    for c0 in range(nq, w_ref.shape[-1], tn):
        res = jnp.dot(hb_scr[0], w_ref[:, c0:c0 + tn], preferred_element_type=F32)
        ra_ref[:, c0 - nq:c0 - nq + tn] = res.astype(ra_ref.dtype)
    for g, (src, out) in enumerate(zip((q0_ref, q1_ref, q2_ref), (t0_ref, t1_ref, t2_ref))):
        lanes = out.shape[-1]
        r = DILATIONS[g]

        @pl.when(pl.program_id(1) >= pl.num_programs(1) - tail_tiles[g])
        def _():
            for j in (1, 2):
                for c in range(r):
                    for n in range(tn // LANES):
                        h_scr[n, pl.ds(c, tm // r, stride=r), :] = (
                            src[c, :, j * tn + n * LANES:j * tn + (n + 1) * LANES].astype(F32))
                res = jnp.concatenate([h_scr[n] for n in range(tn // LANES)], axis=1)
                out[j - 1] = res[tm - lanes:, :].T


def _proj_rest_kernel(x_ref, g_ref, *refs, tn, n_w, dils, units_per_seq):
    w_refs = refs[:n_w]
    cols0_ref, cache0_ref, cols1_ref, cache1_ref = refs[n_w:n_w + 4]
    rest_ref, new0_ref, so0_ref, sl0_ref, new1_ref, so1_ref, sl1_ref = refs[n_w + 4:]
    h = _rms(x_ref[...], g_ref[...]).astype(BF16)
    for k, w_ref in enumerate(w_refs):
        res = jnp.dot(h, w_ref[...], preferred_element_type=F32)
        rest_ref[:, k * tn:(k + 1) * tn] = res.astype(rest_ref.dtype)
    seq, part = _unit_index(units_per_seq)
    _cache_unit(cols0_ref, cache0_ref, new0_ref, so0_ref, sl0_ref, seq, part, dils[0])
    _cache_unit(cols1_ref, cache1_ref, new1_ref, so1_ref, sl1_ref, seq, part, dils[1])


def _cache_specs(cache_t, steps, n_i):
    db, _, gw, length = cache_t.shape
    ups = steps // db
    assert steps == db * ups and gw % ups == 0 and (gw // ups) % HEAD_DIM == 0, (steps, db)
    rows = gw // ups
    unit = lambda bb, i: bb * n_i + i
    blk = pl.BlockSpec((None, 2, rows, length),
                       lambda bb, i: (unit(bb, i) // ups, 0, unit(bb, i) % ups, 0))
    stat = pl.BlockSpec((None, SUBLANES, rows), lambda bb, i: (unit(bb, i), 0, 0))
    stat_shape = jax.ShapeDtypeStruct((steps, SUBLANES, rows), F32)
    return ups, blk, stat, stat_shape


def _project_qkv(x3, g, w, n_cols, *, tm):
    b, s, d = x3.shape
    tn = GROUP_WIDTH
    n_i = s // tm
    extra = n_cols - 3 * N_GROUPS * tn
    w_spec = pl.BlockSpec((d, n_cols), lambda bb, i: (0, 0), pipeline_mode=pl.Buffered(1))
    grp_shape = lambda r: jax.ShapeDtypeStruct((b, r, s // r, 3 * tn), BF16)
    grp_spec = lambda r: pl.BlockSpec((None, r, tm // r, 3 * tn), lambda bb, i: (bb, 0, i, 0))
    tails = [min(w, s) for w in WINDOWS]
    tail_tiles = tuple(max(w // tm, 1) for w in tails)
    assert all(w % min(tm, w) == 0 and min(tm, w) % LANES == 0 for w in tails)
    tail_shape = lambda w: jax.ShapeDtypeStruct((b, 2, tn, w), F32)
    tail_spec = lambda w, nt: pl.BlockSpec(
        (None, 2, tn, min(tm, w)), lambda bb, i: (bb, 0, 0, jnp.maximum(i - (n_i - nt), 0)))
    return pl.pallas_call(
        functools.partial(_proj_qkv_kernel, tn=tn, tail_tiles=tail_tiles),
        out_shape=tuple(grp_shape(r) for r in DILATIONS)
        + (jax.ShapeDtypeStruct((b, s, extra), BF16),)
        + tuple(tail_shape(w) for w in tails),
        grid=(b, n_i),
        in_specs=[pl.BlockSpec((None, tm, d), lambda bb, i: (bb, i, 0)),
                  _resident((1, d)), w_spec],
        out_specs=tuple(grp_spec(r) for r in DILATIONS)
        + (pl.BlockSpec((None, tm, extra), lambda bb, i: (bb, i, 0)),)
        + tuple(tail_spec(w, nt) for w, nt in zip(tails, tail_tiles)),
        scratch_shapes=[pltpu.VMEM((d // LANES, tm, LANES), F32),
                        pltpu.VMEM((N_GROUPS, tm, d), BF16)],
        compiler_params=pltpu.CompilerParams(dimension_semantics=("arbitrary", "arbitrary"),
                                             vmem_limit_bytes=VMEM_LIMIT),
        name="proj_qkv",
    )(x3, g, w)


def _project_rest(x3, g, w, first_col, cols01, caches01, *, tm, dils):
    b, s, d = x3.shape
    tn = GROUP_WIDTH
    n = w.shape[1] - first_col
    assert first_col % tn == 0 and n % tn == 0
    n_w = n // tn
    n_i = s // tm
    in_specs = [pl.BlockSpec((None, tm, d), lambda bb, i: (bb, i, 0)), _resident((1, d))]
    in_specs += [pl.BlockSpec((d, tn), lambda bb, i, k=k: (0, first_col // tn + k),
                              pipeline_mode=pl.Buffered(1)) for k in range(n_w)]
    out_shape = [jax.ShapeDtypeStruct((b, s, n), BF16)]
    out_specs = [pl.BlockSpec((None, tm, n), lambda bb, i: (bb, i, 0))]
    args = [x3, g] + [w] * n_w
    for cols, cache_t in zip(cols01, caches01):
        ups, blk, stat, stat_shape = _cache_specs(cache_t, b * n_i, n_i)
        in_specs += [_resident(cols.shape), blk]
        args += [cols, cache_t]
        out_shape += [jax.ShapeDtypeStruct(cache_t.shape, cache_t.dtype), stat_shape, stat_shape]
        out_specs += [blk, stat, stat]
    return pl.pallas_call(
        functools.partial(_proj_rest_kernel, tn=tn, n_w=n_w, dils=dils, units_per_seq=ups),
        out_shape=tuple(out_shape),
        grid=(b, n_i),
        in_specs=in_specs,
        out_specs=tuple(out_specs),
        compiler_params=pltpu.CompilerParams(dimension_semantics=("arbitrary", "arbitrary"),
                                             vmem_limit_bytes=VMEM_LIMIT),
        name="proj_rest",
    )(*args)


def _proj_rows_kernel(x_ref, g_ref, w_ref, a_ref, b_ref, *, tn):
    h = _rms(x_ref[...], g_ref[...]).astype(BF16)
    na = a_ref.shape[-1]
    for c0 in range(0, w_ref.shape[-1], tn):
        res = jnp.dot(h, w_ref[:, c0:c0 + tn], preferred_element_type=F32)
        if c0 < na:
            a_ref[:, c0:c0 + tn] = res.astype(a_ref.dtype)
        else:
            b_ref[:, c0 - na:c0 - na + tn] = res.astype(b_ref.dtype)


def _project_rows(x2d, g, w, n_cols, *, out_dtype):
    m, d = x2d.shape
    whole = lambda a: pl.BlockSpec(a.shape, lambda i: (0,) * a.ndim)
    out_shape = (jax.ShapeDtypeStruct((m, n_cols), out_dtype),
                 jax.ShapeDtypeStruct((m, w.shape[1] - n_cols), out_dtype))
    return pl.pallas_call(
        functools.partial(_proj_rows_kernel, tn=GROUP_WIDTH),
        out_shape=out_shape,
        grid=(1,),
        in_specs=[whole(x2d), _resident((1, d)), _resident(w.shape)],
        out_specs=tuple(whole(o) for o in out_shape),
        compiler_params=pltpu.CompilerParams(dimension_semantics=("arbitrary",),
                                             vmem_limit_bytes=VMEM_LIMIT),
        name="proj_sample",
    )(x2d, g, w)


def _attn_kernel(q_ref, kc_ref, vc_ref, kp_ref, vp_ref, o_ref, l_ref, kcat, vcat, *, tq):
    i = pl.program_id(2)
    kcat[0:BAND, :] = kp_ref[...]
    kcat[BAND:, :] = kc_ref[...]
    vcat[0:BAND, :] = vp_ref[...]
    vcat[BAND:, :] = vc_ref[...]

    qi = lax.broadcasted_iota(jnp.int32, (BAND, 2 * BAND), 0)
    ki = lax.broadcasted_iota(jnp.int32, (BAND, 2 * BAND), 1)
    rel = qi + BAND - ki
    band_ok = (rel >= 0) & (rel <= BAND)
    lane = lax.broadcasted_iota(jnp.int32, (BAND, LANES), 1)
    low = lane < HEAD_DIM

    for qb in range(tq // BAND):
        r0 = qb * BAND
        if qb == 0:
            valid = band_ok & ((ki >= BAND) | (i > 0))
        else:
            valid = band_ok
        valid2 = jnp.concatenate([valid, valid], axis=0)
        for hp in range(GROUP_WIDTH // LANES):
            c0 = hp * LANES
            q2 = q_ref[r0:r0 + BAND, c0:c0 + LANES]
            k2 = kcat[r0:r0 + 2 * BAND, c0:c0 + LANES]
            v2 = vcat[r0:r0 + 2 * BAND, c0:c0 + LANES]
            zero = jnp.zeros_like(q2)
            qm = jnp.concatenate([jnp.where(low, q2, zero), jnp.where(low, zero, q2)], axis=0)
            s = lax.dot_general(qm, k2, (((1,), (1,)), ((), ())), preferred_element_type=F32)
            s = jnp.where(valid2, s, NEG)
            m = jnp.max(s, axis=-1, keepdims=True)
            p = jnp.exp2(s - m)
            den = jnp.sum(p, axis=-1, keepdims=True)
            pv = jnp.dot(p.astype(BF16), v2, preferred_element_type=F32)
            o = pv * (1.0 / den)
            lse = m + jnp.log2(den)
            o_ref[r0:r0 + BAND, c0:c0 + LANES] = jnp.where(low, o[0:BAND], o[BAND:]).astype(o_ref.dtype)
            l_ref[r0:r0 + BAND, c0:c0 + LANES] = jnp.where(low, lse[0:BAND], lse[BAND:])


def _prompt_attention(qkv_g, g, *, tq):
    b, r, sr, _ = qkv_g.shape
    tq = min(tq, sr)
    per = tq // BAND

    def cur(j):
        return pl.BlockSpec((None, None, tq, GROUP_WIDTH), lambda bb, c, i: (bb, c, i, j))

    def prev(j):
        return pl.BlockSpec((None, None, BAND, GROUP_WIDTH),
                            lambda bb, c, i: (bb, c, jnp.maximum(i * per - 1, 0), j))

    out_spec = pl.BlockSpec((None, None, tq, GROUP_WIDTH), lambda bb, c, i: (bb, c, i, 0))
    return pl.pallas_call(
        functools.partial(_attn_kernel, tq=tq),
        out_shape=(jax.ShapeDtypeStruct((b, r, sr, GROUP_WIDTH), F32),
                   jax.ShapeDtypeStruct((b, r, sr, GROUP_WIDTH), F32)),
        grid=(b, r, sr // tq),
        in_specs=[cur(0), cur(1), cur(2), prev(1), prev(2)],
        out_specs=(out_spec, out_spec),
        scratch_shapes=[pltpu.VMEM((BAND + tq, GROUP_WIDTH), BF16),
                        pltpu.VMEM((BAND + tq, GROUP_WIDTH), BF16)],
        compiler_params=pltpu.CompilerParams(
            dimension_semantics=("arbitrary", "arbitrary", "arbitrary"),
            vmem_limit_bytes=VMEM_LIMIT),
        name=f"attn_g{g}",
    )(qkv_g, qkv_g, qkv_g, qkv_g, qkv_g)


def _merge_groups(os_, ls, exp=jnp.exp):
    m = jnp.maximum(jnp.maximum(ls[0], ls[1]), ls[2])
    es = [exp(l - m) for l in ls]
    num = es[0] * os_[0] + es[1] * os_[1] + es[2] * os_[2]
    return num / (es[0] + es[1] + es[2])


def _block_core(x, attn, c_b, conv_v, g_a, g_c, wao_ref, wco_ref, wo_ref, w1_ref, w2_ref,
                gpost, gpre, gfpost, *, ff_chunk, row_parts=1, mid_mlp=None):
    n = x.shape[0] // row_parts
    rows = [slice(p * n, (p + 1) * n) for p in range(row_parts)]
    dot = functools.partial(jnp.dot, preferred_element_type=F32)
    a = [dot(attn[r].astype(BF16), wao_ref[...]) for r in rows]
    c = [dot((c_b[r] * conv_v[r]).astype(BF16), wco_ref[...]) for r in rows]
    mix = [dot((jax.nn.sigmoid(g_a[r]) * a[p] + jax.nn.sigmoid(g_c[r]) * c[p]).astype(BF16),
               wo_ref[...]) for p, r in enumerate(rows)]
    x1 = [x[r] + _rms(mix[p], gpost) for p, r in enumerate(rows)]
    h2 = [_rms(v, gpre).astype(BF16) for v in x1]
    f = [None] * row_parts
    for k0 in range(0, w1_ref.shape[1], ff_chunk):
        if mid_mlp is not None and k0 == ff_chunk:
            mid_mlp()
        t = [dot(h, w1_ref[:, k0:k0 + ff_chunk]) for h in h2]
        t = [jnp.square(jnp.maximum(v, 0.0)).astype(BF16) for v in t]
        part = [dot(v, w2_ref[k0:k0 + ff_chunk, :]) for v in t]
        f = [q if acc is None else acc + q for acc, q in zip(f, part)]
    return jnp.concatenate([x1[p] + _rms(f[p], gfpost) for p in range(row_parts)], axis=0)


def _rest_columns(ra_ref, rb_ref, d):
    a0 = ra_ref.shape[-1] - 2 * d
    c_b = ra_ref[:, a0:a0 + d].astype(F32)
    u = ra_ref[:, a0 + d:a0 + 2 * d].astype(F32) * rb_ref[:, 0:d].astype(F32)
    g_a = rb_ref[:, d:2 * d].astype(F32)
    g_c = rb_ref[:, 2 * d:3 * d].astype(F32)
    return c_b, u, g_a, g_c


def _block_prompt_kernel(x_ref, ra_ref, rb_ref, hcc_ref, hch_ref, o0, l0, o1, l1, o2, l2,
                         cols_ref, cache_ref,
                         convw_ref, wao_ref, wco_ref, wo_ref, w1_ref, w2_ref,
                         gpost_ref, gpre_ref, gfpost_ref, y_ref, tail_ref, new_ref, so_ref, sl_ref,
                         u_scr, nat_scr, *, ff_chunk, dil, units_per_seq):
    def window_unit():
        seq, part = _unit_index(units_per_seq)
        _cache_unit(cols_ref, cache_ref, new_ref, so_ref, sl_ref, seq, part, dil)

    i = pl.program_id(1)
    tm, d = x_ref.shape
    vals = []
    slab = 0
    for ref in (o0, l0, o1, l1, o2, l2):
        r = ref.shape[0]
        if r == 1:
            vals.append(ref[0])
        else:
            nl = GROUP_WIDTH // LANES
            for c in range(r):
                for j in range(nl):
                    nat_scr[slab + j, pl.ds(c, tm // r, stride=r), :] = (
                        ref[c, :, j * LANES:(j + 1) * LANES])
            vals.append(jnp.concatenate([nat_scr[slab + j] for j in range(nl)], axis=1))
            slab += nl
    attn = _merge_groups(vals[0::2], vals[1::2], exp=jnp.exp2)
    c_b, u, g_a, g_c = _rest_columns(ra_ref, rb_ref, d)
    hu = hcc_ref[...].astype(F32) * hch_ref[...].astype(F32)
    u_scr[0:SUBLANES, :] = jnp.where(i > 0, hu, jnp.zeros_like(hu))
    u_scr[SUBLANES:, :] = u
    w = convw_ref[...]
    conv_v = (w[0:1, :] * u_scr[SUBLANES - 2:SUBLANES - 2 + tm, :]
              + w[1:2, :] * u_scr[SUBLANES - 1:SUBLANES - 1 + tm, :]
              + w[2:3, :] * u)
    y_ref[...] = _block_core(x_ref[...], attn, c_b, conv_v, g_a, g_c,
                             wao_ref, wco_ref, wo_ref, w1_ref, w2_ref,
                             gpost_ref[...], gpre_ref[...], gfpost_ref[...], ff_chunk=ff_chunk,
                             row_parts=2, mid_mlp=window_unit)
    tail_ref[...] = u[tm - SUBLANES:tm, :]


def _block_sample_kernel(x_ref, ra_ref, rb_ref, st_ref, o0, l0, o1, l1, o2, l2,
                         convw_ref, wao_ref, wco_ref, wo_ref, w1_ref, w2_ref,
                         gpost_ref, gpre_ref, gfpost_ref, y_ref, u_ref, *, ff_chunk):
    d = x_ref.shape[1]
    attn = _merge_groups((o0[...], o1[...], o2[...]), (l0[...], l1[...], l2[...]))
    c_b, u, g_a, g_c = _rest_columns(ra_ref, rb_ref, d)
    w = convw_ref[...]
    conv_v = w[0:1, :] * st_ref[:, 0:d] + w[1:2, :] * st_ref[:, d:2 * d] + w[2:3, :] * u
    y_ref[...] = _block_core(x_ref[...], attn, c_b, conv_v, g_a, g_c,
                             wao_ref, wco_ref, wo_ref, w1_ref, w2_ref,
                             gpost_ref[...], gpre_ref[...], gfpost_ref[...], ff_chunk=ff_chunk)
    u_ref[...] = u


def _weight_specs(ws):
    return [_resident(w.shape) for w in ws]


def _block_prompt(x3, ra3, rb3, os_, ls_, cols, cache_t, weights, *, tm, ff_chunk, dil):
    b, s, d = x3.shape
    assert ra3.shape[-1] == 2 * d and rb3.shape[-1] == 3 * d
    per = tm // SUBLANES
    row = lambda width: pl.BlockSpec((None, tm, width), lambda bb, i: (bb, i, 0))
    halo = lambda col: pl.BlockSpec((None, SUBLANES, d),
                                    lambda bb, i: (bb, jnp.maximum(i * per - 1, 0), col))
    in_specs = [row(d), row(2 * d), row(3 * d), halo(1), halo(0)]
    args = [x3, ra3, rb3, ra3, rb3]
    for o, l in zip(os_, ls_):
        r = o.shape[1]
        res = pl.BlockSpec((None, r, tm // r, GROUP_WIDTH), lambda bb, i: (bb, 0, i, 0))
        in_specs += [res, res]
        args += [o, l]
    n_i = s // tm
    ups, blk, stat, stat_shape = _cache_specs(cache_t, b * n_i, n_i)
    in_specs += [_resident(cols.shape), blk]
    args += [cols, cache_t]
    in_specs += _weight_specs(weights)
    args += list(weights)
    n_slabs = 2 * sum(o.shape[1] > 1 for o in os_) * GROUP_WIDTH // LANES
    return pl.pallas_call(
        functools.partial(_block_prompt_kernel, ff_chunk=ff_chunk, dil=dil, units_per_seq=ups),
        out_shape=(jax.ShapeDtypeStruct((b, s, d), F32),
                   jax.ShapeDtypeStruct((b, SUBLANES, d), F32),
                   jax.ShapeDtypeStruct(cache_t.shape, cache_t.dtype), stat_shape, stat_shape),
        grid=(b, n_i),
        in_specs=in_specs,
        out_specs=(row(d), pl.BlockSpec((None, SUBLANES, d), lambda bb, i: (bb, 0, 0)),
                   blk, stat, stat),
        scratch_shapes=[pltpu.VMEM((SUBLANES + tm, d), F32),
                        pltpu.VMEM((n_slabs, tm, LANES), F32)],
        compiler_params=pltpu.CompilerParams(dimension_semantics=("arbitrary", "arbitrary"),
                                             vmem_limit_bytes=VMEM_LIMIT_BLOCK),
        name="block_prompt",
    )(*args)


def _block_sample(x2, ra2, rb2, state2, os_, ls_, weights, *, ff_chunk):
    m, d = x2.shape
    full = lambda a: pl.BlockSpec(a.shape, lambda i: (0,) * a.ndim)
    args = [x2, ra2, rb2, state2]
    for o, l in zip(os_, ls_):
        args += [o, l]
    in_specs = [full(a) for a in args] + _weight_specs(weights)
    args += list(weights)
    return pl.pallas_call(
        functools.partial(_block_sample_kernel, ff_chunk=ff_chunk),
        out_shape=(jax.ShapeDtypeStruct((m, d), F32), jax.ShapeDtypeStruct((m, d), F32)),
        grid=(1,),
        in_specs=in_specs,
        out_specs=(pl.BlockSpec((m, d), lambda i: (0, 0)), pl.BlockSpec((m, d), lambda i: (0, 0))),
        compiler_params=pltpu.CompilerParams(dimension_semantics=("arbitrary",),
                                             vmem_limit_bytes=VMEM_LIMIT),
        name="block_sample",
    )(*args)


def _cache_unit(cols_ref, cache_ref, out_ref, o_ref, l_ref, seq, part, dil):
    _, nrows, length = cache_ref.shape
    scale = HEAD_DIM ** -0.5
    mine = lax.broadcasted_iota(jnp.int32, (1, LANES), 1) == seq
    base = pl.multiple_of(part * nrows, SUBLANES)

    def column(segment):
        blk = cols_ref[pl.ds(segment * GROUP_WIDTH + base, nrows), :]
        return jnp.sum(jnp.where(mine, blk, 0.0), axis=-1, keepdims=True)

    qc, knc, vnc = column(0), column(1), column(2)
    row = lax.broadcasted_iota(jnp.int32, (1, length), 1)
    dist = length - row
    on_grid = ((dist % dil) == 0) & (dist <= BAND * dil)
    o_cols, l_cols = [], []
    for h in range(nrows // HEAD_DIM):
        rows = slice(h * HEAD_DIM, (h + 1) * HEAD_DIM)
        s = jnp.sum(cache_ref[0, rows, :] * qc[rows], axis=0, keepdims=True) * scale
        s = jnp.where(on_grid, s, NEG)
        s_new = jnp.sum(qc[rows] * knc[rows], axis=0, keepdims=True) * scale
        m = jnp.maximum(jnp.max(s, axis=-1, keepdims=True), s_new)
        p = jnp.exp(s - m)
        p_new = jnp.exp(s_new - m)
        den = jnp.sum(p, axis=-1, keepdims=True) + p_new
        acc = jnp.sum(cache_ref[1, rows, :] * p, axis=-1, keepdims=True) + p_new * vnc[rows]
        o_cols.append(jnp.broadcast_to(acc / den, (HEAD_DIM, LANES)))
        l_cols.append(jnp.broadcast_to(m + jnp.log(den), (HEAD_DIM, LANES)))
    o_ref[...] = jnp.concatenate(o_cols, axis=0).T[0:SUBLANES]
    l_ref[...] = jnp.concatenate(l_cols, axis=0).T[0:SUBLANES]

    last = row == length - 1
    for kv, newc in ((0, knc), (1, vnc)):
        rolled = pltpu.roll(cache_ref[kv], length - 1, axis=1)
        out_ref[kv] = jnp.where(last, newc, rolled)


def kernel(x_prompt, x_sample, cache_kv_w128, cache_kv_w512, cache_kv_w2048, state_conv, w_in, conv_w, w_attn_o, w_conv_o, w_o, w_ff1, w_ff2, g_mix_pre, g_mix_post, g_ffn_pre, g_ffn_post):
    depth = w_in.shape[0]
    b, s, d = x_prompt.shape
    db, ds_, _ = x_sample.shape
    assert ds_ == 1 and d % LANES == 0 and db <= LANES
    caches = (cache_kv_w128, cache_kv_w512, cache_kv_w2048)
    gw = GROUP_WIDTH

    yp, ys = x_prompt, x_sample.reshape(db, d)
    kv_p = [[] for _ in range(N_GROUPS)]
    kv_s = [[] for _ in range(N_GROUPS)]
    conv_p, conv_s = [], []
    for l in range(depth):
        na = 3 * N_GROUPS * gw + 2 * d
        w_in_b = w_in[l].astype(BF16)
        weights = (conv_w[l], w_attn_o[l].astype(BF16), w_conv_o[l].astype(BF16),
                   w_o[l].astype(BF16), w_ff1[l].astype(BF16), w_ff2[l].astype(BF16),
                   g_mix_post[l][None, :], g_ffn_pre[l][None, :], g_ffn_post[l][None, :])
        g_pre = g_mix_pre[l][None, :]

        qkv_s, rest_s = _project_rows(ys, g_pre, w_in_b, na, out_dtype=F32)
        cols, caches_t = [], []
        for g in range(N_GROUPS):
            length = caches[g].shape[2]
            caches_t.append(jnp.transpose(caches[g][l], (0, 2, 3, 4, 1)).reshape(db, 2, gw, length))
            qkv_g = jnp.concatenate([qkv_s[:, (j * N_GROUPS + g) * gw:(j * N_GROUPS + g + 1) * gw]
                                     for j in range(3)], axis=1)
            cols.append(jnp.pad(qkv_g, ((0, LANES - db), (0, 0))).T)

        q0, q1, q2, ra3, t0, t1, t2 = _project_qkv(yp, g_pre, w_in_b, na, tm=512)
        qkv_groups = (q0, q1, q2)
        rb3, new0, so0, sl0, new1, so1, sl1 = _project_rest(
            yp, g_pre, w_in_b, na, cols[:2], caches_t[:2], tm=512, dils=DILATIONS[:2])
        os_, ls_ = [], []
        for g in range(N_GROUPS):
            o, lse = _prompt_attention(qkv_groups[g], g, tq=1024)
            os_.append(o)
            ls_.append(lse)
            tail_t = (t0, t1, t2)[g]
            w = tail_t.shape[-1]
            kv_p[g].append(jnp.transpose(tail_t.reshape(b, 2, HEADS, HEAD_DIM, w), (0, 4, 1, 2, 3)))
        yp, tail, new2, so2, sl2 = _block_prompt(yp, ra3, rb3, os_, ls_, cols[2], caches_t[2],
                                                 weights, tm=256, ff_chunk=1024, dil=DILATIONS[2])
        conv_p.append(tail[:, SUBLANES - (CONV_WIDTH - 1):, :])
        sample_parts = ((new0, so0, sl0), (new1, so1, sl1), (new2, so2, sl2))

        os_, ls_ = [], []
        for g in range(N_GROUPS):
            new_t, so, sl = sample_parts[g]
            length = new_t.shape[-1]
            os_.append(so[:, 0, :].reshape(db, gw))
            ls_.append(sl[:, 0, :].reshape(db, gw))
            new = jnp.transpose(new_t.reshape(db, 2, HEADS, HEAD_DIM, length), (0, 4, 1, 2, 3))
            kv_s[g].append(new)
        state = state_conv[l]
        ys, u_s = _block_sample(ys, qkv_s, rest_s, state.reshape(db, (CONV_WIDTH - 1) * d),
                                os_, ls_, weights, ff_chunk=1024)
        conv_s.append(jnp.stack([state[:, 1, :], u_s], axis=1))

    stack = lambda xs: jnp.stack(xs, axis=0)
    return (yp, ys.reshape(db, 1, d),
            stack(kv_p[0]), stack(kv_p[1]), stack(kv_p[2]), stack(conv_p),
            stack(kv_s[0]), stack(kv_s[1]), stack(kv_s[2]), stack(conv_s))
```

```python
import functools

import jax
import jax.numpy as jnp
from jax import lax
from jax.experimental import pallas as pl
from jax.experimental.pallas import tpu as pltpu

HEAD_DIM = 64
HEADS = 8
GROUP_WIDTH = HEADS * HEAD_DIM
N_GROUPS = 3
WINDOWS = (128, 512, 2048)
DILATIONS = (1, 4, 16)
BAND = 128
CONV_WIDTH = 3
EPS = 1e-6
NEG = -1e30
LOG2E = 1.4426950408889634
LANES = 128
SUBLANES = 8
VMEM_LIMIT = 56 * 1024 * 1024
VMEM_LIMIT_BLOCK = 62 * 1024 * 1024

TM_PROJ = 512
TM_BLOCK = 256
TQ_ATTN = 1024
FF_CHUNK = 1024

F32 = jnp.float32
BF16 = jnp.bfloat16


def _rms(x, g):
    return (x * lax.rsqrt(jnp.mean(x * x, axis=-1, keepdims=True) + EPS)) * g


def _resident(shape):
    nd = len(shape)
    return pl.BlockSpec(shape, lambda *_: (0,) * nd, pipeline_mode=pl.Buffered(1))


def _unit_index(units_per_seq):
    u = pl.program_id(0) * pl.num_programs(1) + pl.program_id(1)
    return u // units_per_seq, u % units_per_seq


def _proj_qkv_kernel(x_ref, g_ref, w_ref,
                     q0_ref, q1_ref, q2_ref, ra_ref, t0_ref, t1_ref, t2_ref, h_scr, hb_scr,
                     *, tn, tail_tiles):
    tm = x_ref.shape[0]
    h = _rms(x_ref[...], g_ref[...])
    hb_scr[0] = h.astype(BF16)
    for j in range(h_scr.shape[0]):
        h_scr[j] = h[:, j * LANES:(j + 1) * LANES]
    for g in range(1, N_GROUPS):
        r = DILATIONS[g]
        for c in range(r):
            for j in range(h_scr.shape[0]):
                hb_scr[g, c * (tm // r):(c + 1) * (tm // r), j * LANES:(j + 1) * LANES] = (
                    h_scr[j, pl.ds(c, tm // r, stride=r), :].astype(BF16))
    for g, out in enumerate((q0_ref, q1_ref, q2_ref)):
        r = DILATIONS[g]
        for j in range(3):
            c0 = (j * N_GROUPS + g) * tn
            res = jnp.dot(hb_scr[g], w_ref[:, c0:c0 + tn], preferred_element_type=F32)
            if j == 0:
                res = res * (HEAD_DIM ** -0.5 * LOG2E)
            out[:, :, j * tn:(j + 1) * tn] = res.reshape(r, tm // r, tn).astype(out.dtype)
    nq = 3 * N_GROUPS * tn
    for c0 in range(nq, w_ref.shape[-1], tn):
        res = jnp.dot(hb_scr[0], w_ref[:, c0:c0 + tn], preferred_element_type=F32)
        ra_ref[:, c0 - nq:c0 - nq + tn] = res.astype(ra_ref.dtype)
    for g, (src, out) in enumerate(zip((q0_ref, q1_ref, q2_ref), (t0_ref, t1_ref, t2_ref))):
        lanes = out.shape[-1]
        r = DILATIONS[g]

        @pl.when(pl.program_id(1) >= pl.num_programs(1) - tail_tiles[g])
        def _():
            for j in (1, 2):
                for c in range(r):
                    for n in range(tn // LANES):
                        h_scr[n, pl.ds(c, tm // r, stride=r), :] = (
                            src[c, :, j * tn + n * LANES:j * tn + (n + 1) * LANES].astype(F32))
                res = jnp.concatenate([h_scr[n] for n in range(tn // LANES)], axis=1)
                out[j - 1] = res[tm - lanes:, :].T


def _proj_rest_kernel(x_ref, g_ref, *refs, tn, n_w, dils, units_per_seq):
    w_refs = refs[:n_w]
    cols0_ref, cache0_ref, cols1_ref, cache1_ref = refs[n_w:n_w + 4]
    rest_ref, new0_ref, so0_ref, sl0_ref, new1_ref, so1_ref, sl1_ref = refs[n_w + 4:]
    h = _rms(x_ref[...], g_ref[...]).astype(BF16)
    for k, w_ref in enumerate(w_refs):
        res = jnp.dot(h, w_ref[...], preferred_element_type=F32)
        rest_ref[:, k * tn:(k + 1) * tn] = res.astype(rest_ref.dtype)
    seq, part = _unit_index(units_per_seq)
    _cache_unit(cols0_ref, cache0_ref, new0_ref, so0_ref, sl0_ref, seq, part, dils[0])
    _cache_unit(cols1_ref, cache1_ref, new1_ref, so1_ref, sl1_ref, seq, part, dils[1])


def _cache_specs(cache_t, steps, n_i):
    db, _, gw, length = cache_t.shape
    ups = steps // db
    assert steps == db * ups and gw % ups == 0 and (gw // ups) % HEAD_DIM == 0, (steps, db)
    rows = gw // ups
    unit = lambda bb, i: bb * n_i + i
    blk = pl.BlockSpec((None, 2, rows, length),
                       lambda bb, i: (unit(bb, i) // ups, 0, unit(bb, i) % ups, 0))
    stat = pl.BlockSpec((None, SUBLANES, rows), lambda bb, i: (unit(bb, i), 0, 0))
    stat_shape = jax.ShapeDtypeStruct((steps, SUBLANES, rows), F32)
    return ups, blk, stat, stat_shape


def _project_qkv(x3, g, w, n_cols, *, tm):
    b, s, d = x3.shape
    tn = GROUP_WIDTH
    n_i = s // tm
    extra = n_cols - 3 * N_GROUPS * tn
    w_spec = pl.BlockSpec((d, n_cols), lambda bb, i: (0, 0), pipeline_mode=pl.Buffered(1))
    grp_shape = lambda r: jax.ShapeDtypeStruct((b, r, s // r, 3 * tn), BF16)
    grp_spec = lambda r: pl.BlockSpec((None, r, tm // r, 3 * tn), lambda bb, i: (bb, 0, i, 0))
    tails = [min(w, s) for w in WINDOWS]
    tail_tiles = tuple(max(w // tm, 1) for w in tails)
    assert all(w % min(tm, w) == 0 and min(tm, w) % LANES == 0 for w in tails)
    tail_shape = lambda w: jax.ShapeDtypeStruct((b, 2, tn, w), F32)
    tail_spec = lambda w, nt: pl.BlockSpec(
        (None, 2, tn, min(tm, w)), lambda bb, i: (bb, 0, 0, jnp.maximum(i - (n_i - nt), 0)))
    return pl.pallas_call(
        functools.partial(_proj_qkv_kernel, tn=tn, tail_tiles=tail_tiles),
        out_shape=tuple(grp_shape(r) for r in DILATIONS)
        + (jax.ShapeDtypeStruct((b, s, extra), BF16),)
        + tuple(tail_shape(w) for w in tails),
        grid=(b, n_i),
        in_specs=[pl.BlockSpec((None, tm, d), lambda bb, i: (bb, i, 0)),
                  _resident((1, d)), w_spec],
        out_specs=tuple(grp_spec(r) for r in DILATIONS)
        + (pl.BlockSpec((None, tm, extra), lambda bb, i: (bb, i, 0)),)
        + tuple(tail_spec(w, nt) for w, nt in zip(tails, tail_tiles)),
        scratch_shapes=[pltpu.VMEM((d // LANES, tm, LANES), F32),
                        pltpu.VMEM((N_GROUPS, tm, d), BF16)],
        compiler_params=pltpu.CompilerParams(dimension_semantics=("arbitrary", "arbitrary"),
                                             vmem_limit_bytes=VMEM_LIMIT),
        name="proj_qkv",
    )(x3, g, w)


def _project_rest(x3, g, w, first_col, cols01, caches01, *, tm, dils):
    b, s, d = x3.shape
    tn = GROUP_WIDTH
    n = w.shape[1] - first_col
    assert first_col % tn == 0 and n % tn == 0
    n_w = n // tn
    n_i = s // tm
    in_specs = [pl.BlockSpec((None, tm, d), lambda bb, i: (bb, i, 0)), _resident((1, d))]
    in_specs += [pl.BlockSpec((d, tn), lambda bb, i, k=k: (0, first_col // tn + k),
                              pipeline_mode=pl.Buffered(1)) for k in range(n_w)]
    out_shape = [jax.ShapeDtypeStruct((b, s, n), BF16)]
    out_specs = [pl.BlockSpec((None, tm, n), lambda bb, i: (bb, i, 0))]
    args = [x3, g] + [w] * n_w
    for cols, cache_t in zip(cols01, caches01):
        ups, blk, stat, stat_shape = _cache_specs(cache_t, b * n_i, n_i)
        in_specs += [_resident(cols.shape), blk]
        args += [cols, cache_t]
        out_shape += [jax.ShapeDtypeStruct(cache_t.shape, cache_t.dtype), stat_shape, stat_shape]
        out_specs += [blk, stat, stat]
    return pl.pallas_call(
        functools.partial(_proj_rest_kernel, tn=tn, n_w=n_w, dils=dils, units_per_seq=ups),
        out_shape=tuple(out_shape),
        grid=(b, n_i),
        in_specs=in_specs,
        out_specs=tuple(out_specs),
        compiler_params=pltpu.CompilerParams(dimension_semantics=("arbitrary", "arbitrary"),
                                             vmem_limit_bytes=VMEM_LIMIT),
        name="proj_rest",
    )(*args)


def _proj_rows_kernel(x_ref, g_ref, w_ref, c0_ref, c1_ref, c2_ref, rest_ref, *, tn):
    m = x_ref.shape[0]
    h = _rms(x_ref[...], g_ref[...]).astype(BF16)
    nq = 3 * N_GROUPS * tn
    for g, out in enumerate((c0_ref, c1_ref, c2_ref)):
        for j in range(3):
            c0 = (j * N_GROUPS + g) * tn
            res = jnp.dot(h, w_ref[:, c0:c0 + tn], preferred_element_type=F32)
            res = jnp.concatenate([res, jnp.zeros((LANES - m, tn), F32)], axis=0)
            out[j * tn:(j + 1) * tn, :] = res.T
    for c0 in range(nq, w_ref.shape[-1], tn):
        res = jnp.dot(h, w_ref[:, c0:c0 + tn], preferred_element_type=F32)
        rest_ref[:, c0 - nq:c0 - nq + tn] = res


def _project_rows(x2d, g, w):
    m, d = x2d.shape
    tn = GROUP_WIDTH
    nq = 3 * N_GROUPS * tn
    assert m <= LANES and m % SUBLANES == 0
    whole = lambda a: pl.BlockSpec(a.shape, lambda i: (0,) * a.ndim)
    out_shape = tuple(jax.ShapeDtypeStruct((3 * tn, LANES), F32) for _ in range(N_GROUPS))
    out_shape += (jax.ShapeDtypeStruct((m, w.shape[1] - nq), F32),)
    return pl.pallas_call(
        functools.partial(_proj_rows_kernel, tn=tn),
        out_shape=out_shape,
        grid=(1,),
        in_specs=[whole(x2d), _resident((1, d)), _resident(w.shape)],
        out_specs=tuple(whole(o) for o in out_shape),
        compiler_params=pltpu.CompilerParams(dimension_semantics=("arbitrary",),
                                             vmem_limit_bytes=VMEM_LIMIT),
        name="proj_sample",
    )(x2d, g, w)


def _attn_kernel(q_ref, kc_ref, vc_ref, kp_ref, vp_ref, o_ref, l_ref, kcat, vcat, *, tq):
    i = pl.program_id(2)
    kcat[0:BAND, :] = kp_ref[...]
    kcat[BAND:, :] = kc_ref[...]
    vcat[0:BAND, :] = vp_ref[...]
    vcat[BAND:, :] = vc_ref[...]

    qi = lax.broadcasted_iota(jnp.int32, (BAND, 2 * BAND), 0)
    ki = lax.broadcasted_iota(jnp.int32, (BAND, 2 * BAND), 1)
    rel = qi + BAND - ki
    band_ok = (rel >= 0) & (rel <= BAND)
    lane = lax.broadcasted_iota(jnp.int32, (BAND, LANES), 1)
    low = lane < HEAD_DIM

    for qb in range(tq // BAND):
        r0 = qb * BAND
        if qb == 0:
            valid = band_ok & ((ki >= BAND) | (i > 0))
        else:
            valid = band_ok
        valid2 = jnp.concatenate([valid, valid], axis=0)
        for hp in range(GROUP_WIDTH // LANES):
            c0 = hp * LANES
            q2 = q_ref[r0:r0 + BAND, c0:c0 + LANES]
            k2 = kcat[r0:r0 + 2 * BAND, c0:c0 + LANES]
            v2 = vcat[r0:r0 + 2 * BAND, c0:c0 + LANES]
            zero = jnp.zeros_like(q2)
            qm = jnp.concatenate([jnp.where(low, q2, zero), jnp.where(low, zero, q2)], axis=0)
            s = lax.dot_general(qm, k2, (((1,), (1,)), ((), ())), preferred_element_type=F32)
            s = jnp.where(valid2, s, NEG)
            m = jnp.max(s, axis=-1, keepdims=True)
            p = jnp.exp2(s - m)
            den = jnp.sum(p, axis=-1, keepdims=True)
            pv = jnp.dot(p.astype(BF16), v2, preferred_element_type=F32)
            o = pv * (1.0 / den)
            lse = m + jnp.log2(den)
            o_ref[r0:r0 + BAND, c0:c0 + LANES] = jnp.where(low, o[0:BAND], o[BAND:]).astype(o_ref.dtype)
            l_ref[r0:r0 + BAND, c0:c0 + LANES] = jnp.where(low, lse[0:BAND], lse[BAND:])


def _prompt_attention(qkv_g, g, *, tq):
    b, r, sr, _ = qkv_g.shape
    tq = min(tq, sr)
    per = tq // BAND

    def cur(j):
        return pl.BlockSpec((None, None, tq, GROUP_WIDTH), lambda bb, c, i: (bb, c, i, j))

    def prev(j):
        return pl.BlockSpec((None, None, BAND, GROUP_WIDTH),
                            lambda bb, c, i: (bb, c, jnp.maximum(i * per - 1, 0), j))

    out_spec = pl.BlockSpec((None, None, tq, GROUP_WIDTH), lambda bb, c, i: (bb, c, i, 0))
    return pl.pallas_call(
        functools.partial(_attn_kernel, tq=tq),
        out_shape=(jax.ShapeDtypeStruct((b, r, sr, GROUP_WIDTH), F32),
                   jax.ShapeDtypeStruct((b, r, sr, GROUP_WIDTH), F32)),
        grid=(b, r, sr // tq),
        in_specs=[cur(0), cur(1), cur(2), prev(1), prev(2)],
        out_specs=(out_spec, out_spec),
        scratch_shapes=[pltpu.VMEM((BAND + tq, GROUP_WIDTH), BF16),
                        pltpu.VMEM((BAND + tq, GROUP_WIDTH), BF16)],
        compiler_params=pltpu.CompilerParams(
            dimension_semantics=("arbitrary", "arbitrary", "arbitrary"),
            vmem_limit_bytes=VMEM_LIMIT),
        name=f"attn_g{g}",
    )(qkv_g, qkv_g, qkv_g, qkv_g, qkv_g)


def _merge_groups(os_, ls, exp=jnp.exp):
    m = jnp.maximum(jnp.maximum(ls[0], ls[1]), ls[2])
    es = [exp(l - m) for l in ls]
    num = es[0] * os_[0] + es[1] * os_[1] + es[2] * os_[2]
    return num / (es[0] + es[1] + es[2])


def _block_core(x, attn, c_b, conv_v, g_a, g_c, wao_ref, wco_ref, wo_ref, w1_ref, w2_ref,
                gpost, gpre, gfpost, *, ff_chunk, row_parts=1, mid_mlp=None):
    n = x.shape[0] // row_parts
    rows = [slice(p * n, (p + 1) * n) for p in range(row_parts)]
    dot = functools.partial(jnp.dot, preferred_element_type=F32)
    a = [dot(attn[r].astype(BF16), wao_ref[...]) for r in rows]
    c = [dot((c_b[r] * conv_v[r]).astype(BF16), wco_ref[...]) for r in rows]
    mix = [dot((jax.nn.sigmoid(g_a[r]) * a[p] + jax.nn.sigmoid(g_c[r]) * c[p]).astype(BF16),
               wo_ref[...]) for p, r in enumerate(rows)]
    x1 = [x[r] + _rms(mix[p], gpost) for p, r in enumerate(rows)]
    h2 = [_rms(v, gpre).astype(BF16) for v in x1]
    f = [None] * row_parts
    for k0 in range(0, w1_ref.shape[1], ff_chunk):
        if mid_mlp is not None and k0 == ff_chunk:
            mid_mlp()
        t = [dot(h, w1_ref[:, k0:k0 + ff_chunk]) for h in h2]
        t = [jnp.square(jnp.maximum(v, 0.0)).astype(BF16) for v in t]
        part = [dot(v, w2_ref[k0:k0 + ff_chunk, :]) for v in t]
        f = [q if acc is None else acc + q for acc, q in zip(f, part)]
    return jnp.concatenate([x1[p] + _rms(f[p], gfpost) for p in range(row_parts)], axis=0)


def _rest_columns(ra_ref, rb_ref, b0, d):
    c_b = ra_ref[:, 0:d].astype(F32)
    u = ra_ref[:, d:2 * d].astype(F32) * rb_ref[:, b0:b0 + d].astype(F32)
    g_a = rb_ref[:, b0 + d:b0 + 2 * d].astype(F32)
    g_c = rb_ref[:, b0 + 2 * d:b0 + 3 * d].astype(F32)
    return c_b, u, g_a, g_c


def _block_prompt_kernel(x_ref, ra_ref, rb_ref, hcc_ref, hch_ref, o0, l0, o1, l1, o2, l2,
                         cols_ref, cache_ref,
                         convw_ref, wao_ref, wco_ref, wo_ref, w1_ref, w2_ref,
                         gpost_ref, gpre_ref, gfpost_ref, y_ref, tail_ref, new_ref, so_ref, sl_ref,
                         u_scr, nat_scr, *, ff_chunk, dil, units_per_seq):
    def window_unit():
        seq, part = _unit_index(units_per_seq)
        _cache_unit(cols_ref, cache_ref, new_ref, so_ref, sl_ref, seq, part, dil)

    i = pl.program_id(1)
    tm, d = x_ref.shape
    vals = []
    slab = 0
    for ref in (o0, l0, o1, l1, o2, l2):
        r = ref.shape[0]
        if r == 1:
            vals.append(ref[0])
        else:
            nl = GROUP_WIDTH // LANES
            for c in range(r):
                for j in range(nl):
                    nat_scr[slab + j, pl.ds(c, tm // r, stride=r), :] = (
                        ref[c, :, j * LANES:(j + 1) * LANES])
            vals.append(jnp.concatenate([nat_scr[slab + j] for j in range(nl)], axis=1))
            slab += nl
    attn = _merge_groups(vals[0::2], vals[1::2], exp=jnp.exp2)
    c_b, u, g_a, g_c = _rest_columns(ra_ref, rb_ref, 0, d)
    hu = hcc_ref[...].astype(F32) * hch_ref[...].astype(F32)
    u_scr[0:SUBLANES, :] = jnp.where(i > 0, hu, jnp.zeros_like(hu))
    u_scr[SUBLANES:, :] = u
    w = convw_ref[...]
    conv_v = (w[0:1, :] * u_scr[SUBLANES - 2:SUBLANES - 2 + tm, :]
              + w[1:2, :] * u_scr[SUBLANES - 1:SUBLANES - 1 + tm, :]
              + w[2:3, :] * u)
    y_ref[...] = _block_core(x_ref[...], attn, c_b, conv_v, g_a, g_c,
                             wao_ref, wco_ref, wo_ref, w1_ref, w2_ref,
                             gpost_ref[...], gpre_ref[...], gfpost_ref[...], ff_chunk=ff_chunk,
                             row_parts=2, mid_mlp=window_unit)
    tail_ref[...] = u[tm - SUBLANES:tm, :]


def _block_sample_kernel(x_ref, rest_ref, st_ref, o0, l0, o1, l1, o2, l2,
                         convw_ref, wao_ref, wco_ref, wo_ref, w1_ref, w2_ref,
                         gpost_ref, gpre_ref, gfpost_ref, y_ref, u_ref, *, ff_chunk):
    d = x_ref.shape[1]
    attn = _merge_groups((o0[...], o1[...], o2[...]), (l0[...], l1[...], l2[...]))
    c_b, u, g_a, g_c = _rest_columns(rest_ref, rest_ref, 2 * d, d)
    w = convw_ref[...]
    conv_v = w[0:1, :] * st_ref[:, 0:d] + w[1:2, :] * st_ref[:, d:2 * d] + w[2:3, :] * u
    y_ref[...] = _block_core(x_ref[...], attn, c_b, conv_v, g_a, g_c,
                             wao_ref, wco_ref, wo_ref, w1_ref, w2_ref,
                             gpost_ref[...], gpre_ref[...], gfpost_ref[...], ff_chunk=ff_chunk)
    u_ref[...] = u


def _weight_specs(ws):
    return [_resident(w.shape) for w in ws]


def _block_prompt(x3, ra3, rb3, os_, ls_, cols, cache_t, weights, *, tm, ff_chunk, dil):
    b, s, d = x3.shape
    assert ra3.shape[-1] == 2 * d and rb3.shape[-1] == 3 * d
    per = tm // SUBLANES
    row = lambda width: pl.BlockSpec((None, tm, width), lambda bb, i: (bb, i, 0))
    halo = lambda col: pl.BlockSpec((None, SUBLANES, d),
                                    lambda bb, i: (bb, jnp.maximum(i * per - 1, 0), col))
    in_specs = [row(d), row(2 * d), row(3 * d), halo(1), halo(0)]
    args = [x3, ra3, rb3, ra3, rb3]
    for o, l in zip(os_, ls_):
        r = o.shape[1]
        res = pl.BlockSpec((None, r, tm // r, GROUP_WIDTH), lambda bb, i: (bb, 0, i, 0))
        in_specs += [res, res]
        args += [o, l]
    n_i = s // tm
    ups, blk, stat, stat_shape = _cache_specs(cache_t, b * n_i, n_i)
    in_specs += [_resident(cols.shape), blk]
    args += [cols, cache_t]
    in_specs += _weight_specs(weights)
    args += list(weights)
    n_slabs = 2 * sum(o.shape[1] > 1 for o in os_) * GROUP_WIDTH // LANES
    return pl.pallas_call(
        functools.partial(_block_prompt_kernel, ff_chunk=ff_chunk, dil=dil, units_per_seq=ups),
        out_shape=(jax.ShapeDtypeStruct((b, s, d), F32),
                   jax.ShapeDtypeStruct((b, SUBLANES, d), F32),
                   jax.ShapeDtypeStruct(cache_t.shape, cache_t.dtype), stat_shape, stat_shape),
        grid=(b, n_i),
        in_specs=in_specs,
        out_specs=(row(d), pl.BlockSpec((None, SUBLANES, d), lambda bb, i: (bb, 0, 0)),
                   blk, stat, stat),
        scratch_shapes=[pltpu.VMEM((SUBLANES + tm, d), F32),
                        pltpu.VMEM((n_slabs, tm, LANES), F32)],
        compiler_params=pltpu.CompilerParams(dimension_semantics=("arbitrary", "arbitrary"),
                                             vmem_limit_bytes=VMEM_LIMIT_BLOCK),
        name="block_prompt",
    )(*args)


def _block_sample(x2, rest2, state2, os_, ls_, weights, *, ff_chunk):
    m, d = x2.shape
    full = lambda a: pl.BlockSpec(a.shape, lambda i: (0,) * a.ndim)
    args = [x2, rest2, state2]
    for o, l in zip(os_, ls_):
        args += [o, l]
    in_specs = [full(a) for a in args] + _weight_specs(weights)
    args += list(weights)
    return pl.pallas_call(
        functools.partial(_block_sample_kernel, ff_chunk=ff_chunk),
        out_shape=(jax.ShapeDtypeStruct((m, d), F32), jax.ShapeDtypeStruct((m, d), F32)),
        grid=(1,),
        in_specs=in_specs,
        out_specs=(pl.BlockSpec((m, d), lambda i: (0, 0)), pl.BlockSpec((m, d), lambda i: (0, 0))),
        compiler_params=pltpu.CompilerParams(dimension_semantics=("arbitrary",),
                                             vmem_limit_bytes=VMEM_LIMIT),
        name="block_sample",
    )(*args)


def _cache_unit(cols_ref, cache_ref, out_ref, o_ref, l_ref, seq, part, dil):
    _, nrows, length = cache_ref.shape
    scale = HEAD_DIM ** -0.5
    mine = lax.broadcasted_iota(jnp.int32, (1, LANES), 1) == seq
    base = pl.multiple_of(part * nrows, SUBLANES)

    def column(segment):
        blk = cols_ref[pl.ds(segment * GROUP_WIDTH + base, nrows), :]
        return jnp.sum(jnp.where(mine, blk, 0.0), axis=-1, keepdims=True)

    qc, knc, vnc = column(0), column(1), column(2)
    row = lax.broadcasted_iota(jnp.int32, (1, length), 1)
    dist = length - row
    on_grid = ((dist % dil) == 0) & (dist <= BAND * dil)
    o_cols, l_cols = [], []
    for h in range(nrows // HEAD_DIM):
        rows = slice(h * HEAD_DIM, (h + 1) * HEAD_DIM)
        s = jnp.sum(cache_ref[0, rows, :] * qc[rows], axis=0, keepdims=True) * scale
        s = jnp.where(on_grid, s, NEG)
        s_new = jnp.sum(qc[rows] * knc[rows], axis=0, keepdims=True) * scale
        m = jnp.maximum(jnp.max(s, axis=-1, keepdims=True), s_new)
        p = jnp.exp(s - m)
        p_new = jnp.exp(s_new - m)
        den = jnp.sum(p, axis=-1, keepdims=True) + p_new
        acc = jnp.sum(cache_ref[1, rows, :] * p, axis=-1, keepdims=True) + p_new * vnc[rows]
        o_cols.append(jnp.broadcast_to(acc / den, (HEAD_DIM, LANES)))
        l_cols.append(jnp.broadcast_to(m + jnp.log(den), (HEAD_DIM, LANES)))
    o_ref[...] = jnp.concatenate(o_cols, axis=0).T[0:SUBLANES]
    l_ref[...] = jnp.concatenate(l_cols, axis=0).T[0:SUBLANES]

    last = row == length - 1
    for kv, newc in ((0, knc), (1, vnc)):
        rolled = pltpu.roll(cache_ref[kv], length - 1, axis=1)
        out_ref[kv] = jnp.where(last, newc, rolled)


def kernel(x_prompt, x_sample, cache_kv_w128, cache_kv_w512, cache_kv_w2048, state_conv, w_in, conv_w, w_attn_o, w_conv_o, w_o, w_ff1, w_ff2, g_mix_pre, g_mix_post, g_ffn_pre, g_ffn_post):
    depth = w_in.shape[0]
    b, s, d = x_prompt.shape
    db, ds_, _ = x_sample.shape
    assert ds_ == 1 and d % LANES == 0 and db <= LANES
    caches = (cache_kv_w128, cache_kv_w512, cache_kv_w2048)
    gw = GROUP_WIDTH

    yp, ys = x_prompt, x_sample.reshape(db, d)
    kv_p = [[] for _ in range(N_GROUPS)]
    kv_s = [[] for _ in range(N_GROUPS)]
    conv_p, conv_s = [], []
    for l in range(depth):
        na = 3 * N_GROUPS * gw + 2 * d
        w_in_b = w_in[l].astype(BF16)
        weights = (conv_w[l], w_attn_o[l].astype(BF16), w_conv_o[l].astype(BF16),
                   w_o[l].astype(BF16), w_ff1[l].astype(BF16), w_ff2[l].astype(BF16),
                   g_mix_post[l][None, :], g_ffn_pre[l][None, :], g_ffn_post[l][None, :])
        g_pre = g_mix_pre[l][None, :]

        *cols, rest_s = _project_rows(ys, g_pre, w_in_b)
        caches_t = [jnp.transpose(c[l], (0, 2, 3, 4, 1)).reshape(db, 2, gw, c.shape[2])
                    for c in caches]

        q0, q1, q2, ra3, t0, t1, t2 = _project_qkv(yp, g_pre, w_in_b, na, tm=TM_PROJ)
        qkv_groups = (q0, q1, q2)
        rb3, new0, so0, sl0, new1, so1, sl1 = _project_rest(
            yp, g_pre, w_in_b, na, cols[:2], caches_t[:2], tm=TM_PROJ, dils=DILATIONS[:2])
        os_, ls_ = [], []
        for g in range(N_GROUPS):
            o, lse = _prompt_attention(qkv_groups[g], g, tq=TQ_ATTN)
            os_.append(o)
            ls_.append(lse)
            tail_t = (t0, t1, t2)[g]
            w = tail_t.shape[-1]
            kv_p[g].append(jnp.transpose(tail_t.reshape(b, 2, HEADS, HEAD_DIM, w), (0, 4, 1, 2, 3)))
        yp, tail, new2, so2, sl2 = _block_prompt(yp, ra3, rb3, os_, ls_, cols[2], caches_t[2],
                                                 weights, tm=TM_BLOCK, ff_chunk=FF_CHUNK,
                                                 dil=DILATIONS[2])
        conv_p.append(tail[:, SUBLANES - (CONV_WIDTH - 1):, :])
        sample_parts = ((new0, so0, sl0), (new1, so1, sl1), (new2, so2, sl2))

        os_, ls_ = [], []
        for g in range(N_GROUPS):
            new_t, so, sl = sample_parts[g]
            length = new_t.shape[-1]
            os_.append(so[:, 0, :].reshape(db, gw))
            ls_.append(sl[:, 0, :].reshape(db, gw))
            new = jnp.transpose(new_t.reshape(db, 2, HEADS, HEAD_DIM, length), (0, 4, 1, 2, 3))
            kv_s[g].append(new)
        state = state_conv[l]
        ys, u_s = _block_sample(ys, rest_s, state.reshape(db, (CONV_WIDTH - 1) * d),
                                os_, ls_, weights, ff_chunk=FF_CHUNK)
        conv_s.append(jnp.stack([state[:, 1, :], u_s], axis=1))

    stack = lambda xs: jnp.stack(xs, axis=0)
    return (yp, ys.reshape(db, 1, d),
            stack(kv_p[0]), stack(kv_p[1]), stack(kv_p[2]), stack(conv_p),
            stack(kv_s[0]), stack(kv_s[1]), stack(kv_s[2]), stack(conv_s))
```

```python
import functools

import jax
import jax.numpy as jnp
from jax import lax
from jax.experimental import pallas as pl
from jax.experimental.pallas import tpu as pltpu

HEAD_DIM = 64
HEADS = 8
GROUP_WIDTH = HEADS * HEAD_DIM
N_GROUPS = 3
WINDOWS = (128, 512, 2048)
DILATIONS = (1, 4, 16)
BAND = 128
CONV_WIDTH = 3
EPS = 1e-6
NEG = -1e30
LOG2E = 1.4426950408889634
LANES = 128
SUBLANES = 8
VMEM_LIMIT = 56 * 1024 * 1024
VMEM_LIMIT_BLOCK = 62 * 1024 * 1024

TM_PROJ = 512
TM_BLOCK = 256
TQ_ATTN = 2048
FF_CHUNK = 1024

F32 = jnp.float32
BF16 = jnp.bfloat16


def _rms(x, g):
    return (x * lax.rsqrt(jnp.mean(x * x, axis=-1, keepdims=True) + EPS)) * g


def _resident(shape):
    nd = len(shape)
    return pl.BlockSpec(shape, lambda *_: (0,) * nd, pipeline_mode=pl.Buffered(1))


def _unit_index(units_per_seq):
    u = pl.program_id(0) * pl.num_programs(1) + pl.program_id(1)
    return u // units_per_seq, u % units_per_seq


def _proj_qkv_kernel(x_ref, g_ref, w_ref,
                     q0_ref, q1_ref, q2_ref, ra_ref, t0_ref, t1_ref, t2_ref, h_scr, hb_scr,
                     *, tn, tail_tiles):
    tm = x_ref.shape[0]
    h = _rms(x_ref[...], g_ref[...])
    hb_scr[0] = h.astype(BF16)
    for j in range(h_scr.shape[0]):
        h_scr[j] = h[:, j * LANES:(j + 1) * LANES]
    for g in range(1, N_GROUPS):
        r = DILATIONS[g]
        for c in range(r):
            for j in range(h_scr.shape[0]):
                hb_scr[g, c * (tm // r):(c + 1) * (tm // r), j * LANES:(j + 1) * LANES] = (
                    h_scr[j, pl.ds(c, tm // r, stride=r), :].astype(BF16))
    for g, out in enumerate((q0_ref, q1_ref, q2_ref)):
        r = DILATIONS[g]
        for j in range(3):
            c0 = (j * N_GROUPS + g) * tn
            res = jnp.dot(hb_scr[g], w_ref[:, c0:c0 + tn], preferred_element_type=F32)
            if j == 0:
                res = res * (HEAD_DIM ** -0.5 * LOG2E)
            out[:, :, j * tn:(j + 1) * tn] = res.reshape(r, tm // r, tn).astype(out.dtype)
    nq = 3 * N_GROUPS * tn
    for c0 in range(nq, w_ref.shape[-1], tn):
        res = jnp.dot(hb_scr[0], w_ref[:, c0:c0 + tn], preferred_element_type=F32)
        ra_ref[:, c0 - nq:c0 - nq + tn] = res.astype(ra_ref.dtype)
    for g, (src, out) in enumerate(zip((q0_ref, q1_ref, q2_ref), (t0_ref, t1_ref, t2_ref))):
        lanes = out.shape[-1]
        r = DILATIONS[g]

        @pl.when(pl.program_id(1) >= pl.num_programs(1) - tail_tiles[g])
        def _():
            for j in (1, 2):
                for c in range(r):
                    for n in range(tn // LANES):
                        h_scr[n, pl.ds(c, tm // r, stride=r), :] = (
                            src[c, :, j * tn + n * LANES:j * tn + (n + 1) * LANES].astype(F32))
                res = jnp.concatenate([h_scr[n] for n in range(tn // LANES)], axis=1)
                out[j - 1] = res[tm - lanes:, :].T


def _proj_rest_kernel(x_ref, g_ref, *refs, tn, n_w, dils, units_per_seq):
    w_refs = refs[:n_w]
    cols0_ref, cache0_ref, cols1_ref, cache1_ref = refs[n_w:n_w + 4]
    rest_ref, new0_ref, so0_ref, sl0_ref, new1_ref, so1_ref, sl1_ref = refs[n_w + 4:]
    h = _rms(x_ref[...], g_ref[...]).astype(BF16)
    for k, w_ref in enumerate(w_refs):
        res = jnp.dot(h, w_ref[...], preferred_element_type=F32)
        rest_ref[:, k * tn:(k + 1) * tn] = res.astype(rest_ref.dtype)
    seq, part = _unit_index(units_per_seq)
    _cache_unit(cols0_ref, cache0_ref, new0_ref, so0_ref, sl0_ref, seq, part, dils[0])
    _cache_unit(cols1_ref, cache1_ref, new1_ref, so1_ref, sl1_ref, seq, part, dils[1])


def _cache_specs(cache_t, steps, n_i):
    db, _, gw, length = cache_t.shape
    ups = steps // db
    assert steps == db * ups and gw % ups == 0 and (gw // ups) % HEAD_DIM == 0, (steps, db)
    rows = gw // ups
    unit = lambda bb, i: bb * n_i + i
    blk = pl.BlockSpec((None, 2, rows, length),
                       lambda bb, i: (unit(bb, i) // ups, 0, unit(bb, i) % ups, 0))
    stat = pl.BlockSpec((None, SUBLANES, rows), lambda bb, i: (unit(bb, i), 0, 0))
    stat_shape = jax.ShapeDtypeStruct((steps, SUBLANES, rows), F32)
    return ups, blk, stat, stat_shape


def _project_qkv(x3, g, w, n_cols, *, tm):
    b, s, d = x3.shape
    tn = GROUP_WIDTH
    n_i = s // tm
    extra = n_cols - 3 * N_GROUPS * tn
    w_spec = pl.BlockSpec((d, n_cols), lambda bb, i: (0, 0), pipeline_mode=pl.Buffered(1))
    grp_shape = lambda r: jax.ShapeDtypeStruct((b, r, s // r, 3 * tn), BF16)
    grp_spec = lambda r: pl.BlockSpec((None, r, tm // r, 3 * tn), lambda bb, i: (bb, 0, i, 0))
    tails = [min(w, s) for w in WINDOWS]
    tail_tiles = tuple(max(w // tm, 1) for w in tails)
    assert all(w % min(tm, w) == 0 and min(tm, w) % LANES == 0 for w in tails)
    tail_shape = lambda w: jax.ShapeDtypeStruct((b, 2, tn, w), F32)
    tail_spec = lambda w, nt: pl.BlockSpec(
        (None, 2, tn, min(tm, w)), lambda bb, i: (bb, 0, 0, jnp.maximum(i - (n_i - nt), 0)))
    return pl.pallas_call(
        functools.partial(_proj_qkv_kernel, tn=tn, tail_tiles=tail_tiles),
        out_shape=tuple(grp_shape(r) for r in DILATIONS)
        + (jax.ShapeDtypeStruct((b, s, extra), BF16),)
        + tuple(tail_shape(w) for w in tails),
        grid=(b, n_i),
        in_specs=[pl.BlockSpec((None, tm, d), lambda bb, i: (bb, i, 0)),
                  _resident((1, d)), w_spec],
        out_specs=tuple(grp_spec(r) for r in DILATIONS)
        + (pl.BlockSpec((None, tm, extra), lambda bb, i: (bb, i, 0)),)
        + tuple(tail_spec(w, nt) for w, nt in zip(tails, tail_tiles)),
        scratch_shapes=[pltpu.VMEM((d // LANES, tm, LANES), F32),
                        pltpu.VMEM((N_GROUPS, tm, d), BF16)],
        compiler_params=pltpu.CompilerParams(dimension_semantics=("arbitrary", "arbitrary"),
                                             vmem_limit_bytes=VMEM_LIMIT),
        name="proj_qkv",
    )(x3, g, w)


def _project_rest(x3, g, w, first_col, cols01, caches01, *, tm, dils):
    b, s, d = x3.shape
    tn = GROUP_WIDTH
    n = w.shape[1] - first_col
    assert first_col % tn == 0 and n % tn == 0
    n_w = n // tn
    n_i = s // tm
    in_specs = [pl.BlockSpec((None, tm, d), lambda bb, i: (bb, i, 0)), _resident((1, d))]
    in_specs += [pl.BlockSpec((d, tn), lambda bb, i, k=k: (0, first_col // tn + k),
                              pipeline_mode=pl.Buffered(1)) for k in range(n_w)]
    out_shape = [jax.ShapeDtypeStruct((b, s, n), BF16)]
    out_specs = [pl.BlockSpec((None, tm, n), lambda bb, i: (bb, i, 0))]
    args = [x3, g] + [w] * n_w
    for cols, cache_t in zip(cols01, caches01):
        ups, blk, stat, stat_shape = _cache_specs(cache_t, b * n_i, n_i)
        in_specs += [_resident(cols.shape), blk]
        args += [cols, cache_t]
        out_shape += [jax.ShapeDtypeStruct(cache_t.shape, cache_t.dtype), stat_shape, stat_shape]
        out_specs += [blk, stat, stat]
    return pl.pallas_call(
        functools.partial(_proj_rest_kernel, tn=tn, n_w=n_w, dils=dils, units_per_seq=ups),
        out_shape=tuple(out_shape),
        grid=(b, n_i),
        in_specs=in_specs,
        out_specs=tuple(out_specs),
        compiler_params=pltpu.CompilerParams(dimension_semantics=("arbitrary", "arbitrary"),
                                             vmem_limit_bytes=VMEM_LIMIT),
        name="proj_rest",
    )(*args)


def _proj_rows_kernel(x_ref, g_ref, w_ref, c0_ref, c1_ref, c2_ref, rest_ref, *, tn):
    m = x_ref.shape[0]
    h = _rms(x_ref[...], g_ref[...]).astype(BF16)
    nq = 3 * N_GROUPS * tn
    for g, out in enumerate((c0_ref, c1_ref, c2_ref)):
        for j in range(3):
            c0 = (j * N_GROUPS + g) * tn
            res = jnp.dot(h, w_ref[:, c0:c0 + tn], preferred_element_type=F32)
            res = jnp.concatenate([res, jnp.zeros((LANES - m, tn), F32)], axis=0)
            out[j * tn:(j + 1) * tn, :] = res.T
    for c0 in range(nq, w_ref.shape[-1], tn):
        res = jnp.dot(h, w_ref[:, c0:c0 + tn], preferred_element_type=F32)
        rest_ref[:, c0 - nq:c0 - nq + tn] = res


def _project_rows(x2d, g, w):
    m, d = x2d.shape
    tn = GROUP_WIDTH
    nq = 3 * N_GROUPS * tn
    assert m <= LANES and m % SUBLANES == 0
    whole = lambda a: pl.BlockSpec(a.shape, lambda i: (0,) * a.ndim)
    out_shape = tuple(jax.ShapeDtypeStruct((3 * tn, LANES), F32) for _ in range(N_GROUPS))
    out_shape += (jax.ShapeDtypeStruct((m, w.shape[1] - nq), F32),)
    return pl.pallas_call(
        functools.partial(_proj_rows_kernel, tn=tn),
        out_shape=out_shape,
        grid=(1,),
        in_specs=[whole(x2d), _resident((1, d)), _resident(w.shape)],
        out_specs=tuple(whole(o) for o in out_shape),
        compiler_params=pltpu.CompilerParams(dimension_semantics=("arbitrary",),
                                             vmem_limit_bytes=VMEM_LIMIT),
        name="proj_sample",
    )(x2d, g, w)


def _attn_kernel(q_ref, kc_ref, vc_ref, kp_ref, vp_ref, o_ref, l_ref, kcat, vcat, *, tq):
    i = pl.program_id(2)
    kcat[0:BAND, :] = kp_ref[...]
    kcat[BAND:, :] = kc_ref[...]
    vcat[0:BAND, :] = vp_ref[...]
    vcat[BAND:, :] = vc_ref[...]

    qi = lax.broadcasted_iota(jnp.int32, (BAND, 2 * BAND), 0)
    ki = lax.broadcasted_iota(jnp.int32, (BAND, 2 * BAND), 1)
    rel = qi + BAND - ki
    band_ok = (rel >= 0) & (rel <= BAND)
    lane = lax.broadcasted_iota(jnp.int32, (BAND, LANES), 1)
    low = lane < HEAD_DIM

    for qb in range(tq // BAND):
        r0 = qb * BAND
        if qb == 0:
            valid = band_ok & ((ki >= BAND) | (i > 0))
        else:
            valid = band_ok
        valid2 = jnp.concatenate([valid, valid], axis=0)
        for hp in range(GROUP_WIDTH // LANES):
            c0 = hp * LANES
            q2 = q_ref[r0:r0 + BAND, c0:c0 + LANES]
            k2 = kcat[r0:r0 + 2 * BAND, c0:c0 + LANES]
            v2 = vcat[r0:r0 + 2 * BAND, c0:c0 + LANES]
            zero = jnp.zeros_like(q2)
            qm = jnp.concatenate([jnp.where(low, q2, zero), jnp.where(low, zero, q2)], axis=0)
            s = lax.dot_general(qm, k2, (((1,), (1,)), ((), ())), preferred_element_type=F32)
            s = jnp.where(valid2, s, NEG)
            m = jnp.max(s, axis=-1, keepdims=True)
            p = jnp.exp2(s - m)
            den = jnp.sum(p, axis=-1, keepdims=True)
            pv = jnp.dot(p.astype(BF16), v2, preferred_element_type=F32)
            o = pv * (1.0 / den)
            lse = m + jnp.log2(den)
            o_ref[r0:r0 + BAND, c0:c0 + LANES] = jnp.where(low, o[0:BAND], o[BAND:]).astype(o_ref.dtype)
            l_ref[r0:r0 + BAND, c0:c0 + LANES] = jnp.where(low, lse[0:BAND], lse[BAND:])


def _prompt_attention(qkv_g, g, *, tq):
    b, r, sr, _ = qkv_g.shape
    tq = min(tq, sr)
    per = tq // BAND

    def cur(j):
        return pl.BlockSpec((None, None, tq, GROUP_WIDTH), lambda bb, c, i: (bb, c, i, j))

    def prev(j):
        return pl.BlockSpec((None, None, BAND, GROUP_WIDTH),
                            lambda bb, c, i: (bb, c, jnp.maximum(i * per - 1, 0), j))

    out_spec = pl.BlockSpec((None, None, tq, GROUP_WIDTH), lambda bb, c, i: (bb, c, i, 0))
    return pl.pallas_call(
        functools.partial(_attn_kernel, tq=tq),
        out_shape=(jax.ShapeDtypeStruct((b, r, sr, GROUP_WIDTH), F32),
                   jax.ShapeDtypeStruct((b, r, sr, GROUP_WIDTH), F32)),
        grid=(b, r, sr // tq),
        in_specs=[cur(0), cur(1), cur(2), prev(1), prev(2)],
        out_specs=(out_spec, out_spec),
        scratch_shapes=[pltpu.VMEM((BAND + tq, GROUP_WIDTH), BF16),
                        pltpu.VMEM((BAND + tq, GROUP_WIDTH), BF16)],
        compiler_params=pltpu.CompilerParams(
            dimension_semantics=("arbitrary", "arbitrary", "arbitrary"),
            vmem_limit_bytes=VMEM_LIMIT),
        name=f"attn_g{g}",
    )(qkv_g, qkv_g, qkv_g, qkv_g, qkv_g)


def _merge_groups(os_, ls, exp=jnp.exp):
    m = jnp.maximum(jnp.maximum(ls[0], ls[1]), ls[2])
    es = [exp(l - m) for l in ls]
    num = es[0] * os_[0] + es[1] * os_[1] + es[2] * os_[2]
    return num / (es[0] + es[1] + es[2])


def _block_core(x, attn, c_b, conv_v, g_a, g_c, wao_ref, wco_ref, wo_ref, w1_ref, w2_ref,
                gpost, gpre, gfpost, *, ff_chunk, row_parts=1, mid_mlp=None):
    n = x.shape[0] // row_parts
    rows = [slice(p * n, (p + 1) * n) for p in range(row_parts)]
    dot = functools.partial(jnp.dot, preferred_element_type=F32)
    a = [dot(attn[r].astype(BF16), wao_ref[...]) for r in rows]
    c = [dot((c_b[r] * conv_v[r]).astype(BF16), wco_ref[...]) for r in rows]
    mix = [dot((jax.nn.sigmoid(g_a[r]) * a[p] + jax.nn.sigmoid(g_c[r]) * c[p]).astype(BF16),
               wo_ref[...]) for p, r in enumerate(rows)]
    x1 = [x[r] + _rms(mix[p], gpost) for p, r in enumerate(rows)]
    h2 = [_rms(v, gpre).astype(BF16) for v in x1]
    f = [None] * row_parts
    for k0 in range(0, w1_ref.shape[1], ff_chunk):
        if mid_mlp is not None and k0 == ff_chunk:
            mid_mlp()
        t = [dot(h, w1_ref[:, k0:k0 + ff_chunk]) for h in h2]
        t = [jnp.square(jnp.maximum(v, 0.0)).astype(BF16) for v in t]
        part = [dot(v, w2_ref[k0:k0 + ff_chunk, :]) for v in t]
        f = [q if acc is None else acc + q for acc, q in zip(f, part)]
    return jnp.concatenate([x1[p] + _rms(f[p], gfpost) for p in range(row_parts)], axis=0)


def _rest_columns(ra_ref, rb_ref, b0, d):
    c_b = ra_ref[:, 0:d].astype(F32)
    u = ra_ref[:, d:2 * d].astype(F32) * rb_ref[:, b0:b0 + d].astype(F32)
    g_a = rb_ref[:, b0 + d:b0 + 2 * d].astype(F32)
    g_c = rb_ref[:, b0 + 2 * d:b0 + 3 * d].astype(F32)
    return c_b, u, g_a, g_c


def _block_prompt_kernel(x_ref, ra_ref, rb_ref, hcc_ref, hch_ref, o0, l0, o1, l1, o2, l2,
                         cols_ref, cache_ref,
                         convw_ref, wao_ref, wco_ref, wo_ref, w1_ref, w2_ref,
                         gpost_ref, gpre_ref, gfpost_ref, y_ref, tail_ref, new_ref, so_ref, sl_ref,
                         u_scr, nat_scr, *, ff_chunk, dil, units_per_seq):
    def window_unit():
        seq, part = _unit_index(units_per_seq)
        _cache_unit(cols_ref, cache_ref, new_ref, so_ref, sl_ref, seq, part, dil)

    i = pl.program_id(1)
    tm, d = x_ref.shape
    vals = []
    slab = 0
    for ref in (o0, l0, o1, l1, o2, l2):
        r = ref.shape[0]
        if r == 1:
            vals.append(ref[0])
        else:
            nl = GROUP_WIDTH // LANES
            for c in range(r):
                for j in range(nl):
                    nat_scr[slab + j, pl.ds(c, tm // r, stride=r), :] = (
                        ref[c, :, j * LANES:(j + 1) * LANES])
            vals.append(jnp.concatenate([nat_scr[slab + j] for j in range(nl)], axis=1))
            slab += nl
    attn = _merge_groups(vals[0::2], vals[1::2], exp=jnp.exp2)
    c_b, u, g_a, g_c = _rest_columns(ra_ref, rb_ref, 0, d)
    hu = hcc_ref[...].astype(F32) * hch_ref[...].astype(F32)
    u_scr[0:SUBLANES, :] = jnp.where(i > 0, hu, jnp.zeros_like(hu))
    u_scr[SUBLANES:, :] = u
    w = convw_ref[...]
    conv_v = (w[0:1, :] * u_scr[SUBLANES - 2:SUBLANES - 2 + tm, :]
              + w[1:2, :] * u_scr[SUBLANES - 1:SUBLANES - 1 + tm, :]
              + w[2:3, :] * u)
    y_ref[...] = _block_core(x_ref[...], attn, c_b, conv_v, g_a, g_c,
                             wao_ref, wco_ref, wo_ref, w1_ref, w2_ref,
                             gpost_ref[...], gpre_ref[...], gfpost_ref[...], ff_chunk=ff_chunk,
                             row_parts=2, mid_mlp=window_unit)
    tail_ref[...] = u[tm - SUBLANES:tm, :]


def _block_sample_kernel(x_ref, rest_ref, st_ref, o0, l0, o1, l1, o2, l2,
                         convw_ref, wao_ref, wco_ref, wo_ref, w1_ref, w2_ref,
                         gpost_ref, gpre_ref, gfpost_ref, y_ref, u_ref, *, ff_chunk):
    d = x_ref.shape[1]
    attn = _merge_groups((o0[...], o1[...], o2[...]), (l0[...], l1[...], l2[...]))
    c_b, u, g_a, g_c = _rest_columns(rest_ref, rest_ref, 2 * d, d)
    w = convw_ref[...]
    conv_v = w[0:1, :] * st_ref[:, 0:d] + w[1:2, :] * st_ref[:, d:2 * d] + w[2:3, :] * u
    y_ref[...] = _block_core(x_ref[...], attn, c_b, conv_v, g_a, g_c,
                             wao_ref, wco_ref, wo_ref, w1_ref, w2_ref,
                             gpost_ref[...], gpre_ref[...], gfpost_ref[...], ff_chunk=ff_chunk)
    u_ref[...] = u


def _weight_specs(ws):
    return [_resident(w.shape) for w in ws]


def _block_prompt(x3, ra3, rb3, os_, ls_, cols, cache_t, weights, *, tm, ff_chunk, dil):
    b, s, d = x3.shape
    assert ra3.shape[-1] == 2 * d and rb3.shape[-1] == 3 * d
    per = tm // SUBLANES
    row = lambda width: pl.BlockSpec((None, tm, width), lambda bb, i: (bb, i, 0))
    halo = lambda col: pl.BlockSpec((None, SUBLANES, d),
                                    lambda bb, i: (bb, jnp.maximum(i * per - 1, 0), col))
    in_specs = [row(d), row(2 * d), row(3 * d), halo(1), halo(0)]
    args = [x3, ra3, rb3, ra3, rb3]
    for o, l in zip(os_, ls_):
        r = o.shape[1]
        res = pl.BlockSpec((None, r, tm // r, GROUP_WIDTH), lambda bb, i: (bb, 0, i, 0))
        in_specs += [res, res]
        args += [o, l]
    n_i = s // tm
    ups, blk, stat, stat_shape = _cache_specs(cache_t, b * n_i, n_i)
    in_specs += [_resident(cols.shape), blk]
    args += [cols, cache_t]
    in_specs += _weight_specs(weights)
    args += list(weights)
    n_slabs = 2 * sum(o.shape[1] > 1 for o in os_) * GROUP_WIDTH // LANES
    return pl.pallas_call(
        functools.partial(_block_prompt_kernel, ff_chunk=ff_chunk, dil=dil, units_per_seq=ups),
        out_shape=(jax.ShapeDtypeStruct((b, s, d), F32),
                   jax.ShapeDtypeStruct((b, SUBLANES, d), F32),
                   jax.ShapeDtypeStruct(cache_t.shape, cache_t.dtype), stat_shape, stat_shape),
        grid=(b, n_i),
        in_specs=in_specs,
        out_specs=(row(d), pl.BlockSpec((None, SUBLANES, d), lambda bb, i: (bb, 0, 0)),
                   blk, stat, stat),
        scratch_shapes=[pltpu.VMEM((SUBLANES + tm, d), F32),
                        pltpu.VMEM((n_slabs, tm, LANES), F32)],
        compiler_params=pltpu.CompilerParams(dimension_semantics=("arbitrary", "arbitrary"),
                                             vmem_limit_bytes=VMEM_LIMIT_BLOCK),
        name="block_prompt",
    )(*args)


def _block_sample(x2, rest2, state2, os_, ls_, weights, *, ff_chunk):
    m, d = x2.shape
    full = lambda a: pl.BlockSpec(a.shape, lambda i: (0,) * a.ndim)
    args = [x2, rest2, state2]
    for o, l in zip(os_, ls_):
        args += [o, l]
    in_specs = [full(a) for a in args] + _weight_specs(weights)
    args += list(weights)
    return pl.pallas_call(
        functools.partial(_block_sample_kernel, ff_chunk=ff_chunk),
        out_shape=(jax.ShapeDtypeStruct((m, d), F32), jax.ShapeDtypeStruct((m, d), F32)),
        grid=(1,),
        in_specs=in_specs,
        out_specs=(pl.BlockSpec((m, d), lambda i: (0, 0)), pl.BlockSpec((m, d), lambda i: (0, 0))),
        compiler_params=pltpu.CompilerParams(dimension_semantics=("arbitrary",),
                                             vmem_limit_bytes=VMEM_LIMIT),
        name="block_sample",
    )(*args)


def _cache_unit(cols_ref, cache_ref, out_ref, o_ref, l_ref, seq, part, dil):
    _, nrows, length = cache_ref.shape
    scale = HEAD_DIM ** -0.5
    mine = lax.broadcasted_iota(jnp.int32, (1, LANES), 1) == seq
    base = pl.multiple_of(part * nrows, SUBLANES)

    def column(segment):
        blk = cols_ref[pl.ds(segment * GROUP_WIDTH + base, nrows), :]
        return jnp.sum(jnp.where(mine, blk, 0.0), axis=-1, keepdims=True)

    qc, knc, vnc = column(0), column(1), column(2)
    row = lax.broadcasted_iota(jnp.int32, (1, length), 1)
    dist = length - row
    on_grid = ((dist % dil) == 0) & (dist <= BAND * dil)
    o_cols, l_cols = [], []
    for h in range(nrows // HEAD_DIM):
        rows = slice(h * HEAD_DIM, (h + 1) * HEAD_DIM)
        s = jnp.sum(cache_ref[0, rows, :] * qc[rows], axis=0, keepdims=True) * scale
        s = jnp.where(on_grid, s, NEG)
        s_new = jnp.sum(qc[rows] * knc[rows], axis=0, keepdims=True) * scale
        m = jnp.maximum(jnp.max(s, axis=-1, keepdims=True), s_new)
        p = jnp.exp(s - m)
        p_new = jnp.exp(s_new - m)
        den = jnp.sum(p, axis=-1, keepdims=True) + p_new
        acc = jnp.sum(cache_ref[1, rows, :] * p, axis=-1, keepdims=True) + p_new * vnc[rows]
        o_cols.append(jnp.broadcast_to(acc / den, (HEAD_DIM, LANES)))
        l_cols.append(jnp.broadcast_to(m + jnp.log(den), (HEAD_DIM, LANES)))
    o_ref[...] = jnp.concatenate(o_cols, axis=0).T[0:SUBLANES]
    l_ref[...] = jnp.concatenate(l_cols, axis=0).T[0:SUBLANES]

    last = row == length - 1
    for kv, newc in ((0, knc), (1, vnc)):
        rolled = pltpu.roll(cache_ref[kv], length - 1, axis=1)
        out_ref[kv] = jnp.where(last, newc, rolled)


def kernel(x_prompt, x_sample, cache_kv_w128, cache_kv_w512, cache_kv_w2048, state_conv, w_in, conv_w, w_attn_o, w_conv_o, w_o, w_ff1, w_ff2, g_mix_pre, g_mix_post, g_ffn_pre, g_ffn_post):
    depth = w_in.shape[0]
    b, s, d = x_prompt.shape
    db, ds_, _ = x_sample.shape
    assert ds_ == 1 and d % LANES == 0 and db <= LANES
    caches = (cache_kv_w128, cache_kv_w512, cache_kv_w2048)
    gw = GROUP_WIDTH

    yp, ys = x_prompt, x_sample.reshape(db, d)
    kv_p = [[] for _ in range(N_GROUPS)]
    kv_s = [[] for _ in range(N_GROUPS)]
    conv_p, conv_s = [], []
    for l in range(depth):
        na = 3 * N_GROUPS * gw + 2 * d
        w_in_b = w_in[l].astype(BF16)
        weights = (conv_w[l], w_attn_o[l].astype(BF16), w_conv_o[l].astype(BF16),
                   w_o[l].astype(BF16), w_ff1[l].astype(BF16), w_ff2[l].astype(BF16),
                   g_mix_post[l][None, :], g_ffn_pre[l][None, :], g_ffn_post[l][None, :])
        g_pre = g_mix_pre[l][None, :]

        *cols, rest_s = _project_rows(ys, g_pre, w_in_b)
        caches_t = [jnp.transpose(c[l], (0, 2, 3, 4, 1)).reshape(db, 2, gw, c.shape[2])
                    for c in caches]

        q0, q1, q2, ra3, t0, t1, t2 = _project_qkv(yp, g_pre, w_in_b, na, tm=TM_PROJ)
        qkv_groups = (q0, q1, q2)
        rb3, new0, so0, sl0, new1, so1, sl1 = _project_rest(
            yp, g_pre, w_in_b, na, cols[:2], caches_t[:2], tm=TM_PROJ, dils=DILATIONS[:2])
        os_, ls_ = [], []
        for g in range(N_GROUPS):
            o, lse = _prompt_attention(qkv_groups[g], g, tq=TQ_ATTN)
            os_.append(o)
            ls_.append(lse)
            tail_t = (t0, t1, t2)[g]
            w = tail_t.shape[-1]
            kv_p[g].append(jnp.transpose(tail_t.reshape(b, 2, HEADS, HEAD_DIM, w), (0, 4, 1, 2, 3)))
        yp, tail, new2, so2, sl2 = _block_prompt(yp, ra3, rb3, os_, ls_, cols[2], caches_t[2],
                                                 weights, tm=TM_BLOCK, ff_chunk=FF_CHUNK,
                                                 dil=DILATIONS[2])
        conv_p.append(tail[:, SUBLANES - (CONV_WIDTH - 1):, :])
        sample_parts = ((new0, so0, sl0), (new1, so1, sl1), (new2, so2, sl2))

        os_, ls_ = [], []
        for g in range(N_GROUPS):
            new_t, so, sl = sample_parts[g]
            length = new_t.shape[-1]
            os_.append(so[:, 0, :].reshape(db, gw))
            ls_.append(sl[:, 0, :].reshape(db, gw))
            new = jnp.transpose(new_t.reshape(db, 2, HEADS, HEAD_DIM, length), (0, 4, 1, 2, 3))
            kv_s[g].append(new)
        state = state_conv[l]
        ys, u_s = _block_sample(ys, rest_s, state.reshape(db, (CONV_WIDTH - 1) * d),
                                os_, ls_, weights, ff_chunk=FF_CHUNK)
        conv_s.append(jnp.stack([state[:, 1, :], u_s], axis=1))

    stack = lambda xs: jnp.stack(xs, axis=0)
    return (yp, ys.reshape(db, 1, d),
            stack(kv_p[0]), stack(kv_p[1]), stack(kv_p[2]), stack(conv_p),
            stack(kv_s[0]), stack(kv_s[1]), stack(kv_s[2]), stack(conv_s))
```

```python
import functools

import jax
import jax.numpy as jnp
from jax import lax
from jax.experimental import pallas as pl
from jax.experimental.pallas import tpu as pltpu

HEAD_DIM = 64
HEADS = 8
GROUP_WIDTH = HEADS * HEAD_DIM
N_GROUPS = 3
WINDOWS = (128, 512, 2048)
DILATIONS = (1, 4, 16)
BAND = 128
CONV_WIDTH = 3
EPS = 1e-6
NEG = -1e30
LOG2E = 1.4426950408889634
LANES = 128
SUBLANES = 8
VMEM_LIMIT = 56 * 1024 * 1024

TM_PROJ = 512
TM_BLOCK = 256
TQ_ATTN = 2048
FF_CHUNK = 1024

F32 = jnp.float32
BF16 = jnp.bfloat16


def _rms(x, g):
    return (x * lax.rsqrt(jnp.mean(x * x, axis=-1, keepdims=True) + EPS)) * g


def _resident(shape):
    nd = len(shape)
    return pl.BlockSpec(shape, lambda *_: (0,) * nd, pipeline_mode=pl.Buffered(1))


def _unit_index(units_per_seq):
    u = pl.program_id(0) * pl.num_programs(1) + pl.program_id(1)
    return u // units_per_seq, u % units_per_seq


def _proj_qkv_kernel(x_ref, g_ref, w_ref,
                     q0_ref, q1_ref, q2_ref, ra_ref, t0_ref, t1_ref, t2_ref, h_scr, hb_scr,
                     *, tn, tail_tiles):
    tm = x_ref.shape[0]
    h = _rms(x_ref[...], g_ref[...])
    hb_scr[0] = h.astype(BF16)
    for j in range(h_scr.shape[0]):
        h_scr[j] = h[:, j * LANES:(j + 1) * LANES]
    for g in range(1, N_GROUPS):
        r = DILATIONS[g]
        for c in range(r):
            for j in range(h_scr.shape[0]):
                hb_scr[g, c * (tm // r):(c + 1) * (tm // r), j * LANES:(j + 1) * LANES] = (
                    h_scr[j, pl.ds(c, tm // r, stride=r), :].astype(BF16))
    for g, out in enumerate((q0_ref, q1_ref, q2_ref)):
        r = DILATIONS[g]
        for j in range(3):
            c0 = (j * N_GROUPS + g) * tn
            res = jnp.dot(hb_scr[g], w_ref[:, c0:c0 + tn], preferred_element_type=F32)
            if j == 0:
                res = res * (HEAD_DIM ** -0.5 * LOG2E)
            out[:, :, j * tn:(j + 1) * tn] = res.reshape(r, tm // r, tn).astype(out.dtype)
    nq = 3 * N_GROUPS * tn
    for c0 in range(nq, w_ref.shape[-1], tn):
        res = jnp.dot(hb_scr[0], w_ref[:, c0:c0 + tn], preferred_element_type=F32)
        ra_ref[:, c0 - nq:c0 - nq + tn] = res.astype(ra_ref.dtype)
    for g, (src, out) in enumerate(zip((q0_ref, q1_ref, q2_ref), (t0_ref, t1_ref, t2_ref))):
        lanes = out.shape[-1]
        r = DILATIONS[g]

        @pl.when(pl.program_id(1) >= pl.num_programs(1) - tail_tiles[g])
        def _():
            for j in (1, 2):
                for c in range(r):
                    for n in range(tn // LANES):
                        h_scr[n, pl.ds(c, tm // r, stride=r), :] = (
                            src[c, :, j * tn + n * LANES:j * tn + (n + 1) * LANES].astype(F32))
                res = jnp.concatenate([h_scr[n] for n in range(tn // LANES)], axis=1)
                out[j - 1] = res[tm - lanes:, :].T


def _proj_rest_kernel(x_ref, g_ref, *refs, tn, n_w, dils, units_per_seq):
    w_refs = refs[:n_w]
    cols0_ref, cache0_ref, cols1_ref, cache1_ref = refs[n_w:n_w + 4]
    rest_ref, new0_ref, so0_ref, sl0_ref, new1_ref, so1_ref, sl1_ref = refs[n_w + 4:]
    h = _rms(x_ref[...], g_ref[...]).astype(BF16)
    for k, w_ref in enumerate(w_refs):
        res = jnp.dot(h, w_ref[...], preferred_element_type=F32)
        rest_ref[:, k * tn:(k + 1) * tn] = res.astype(rest_ref.dtype)
    seq, part = _unit_index(units_per_seq)
    _cache_unit(cols0_ref, cache0_ref, new0_ref, so0_ref, sl0_ref, seq, part, dils[0])
    _cache_unit(cols1_ref, cache1_ref, new1_ref, so1_ref, sl1_ref, seq, part, dils[1])


def _cache_specs(cache_t, steps, n_i):
    db, _, gw, length = cache_t.shape
    ups = steps // db
    assert steps == db * ups and gw % ups == 0 and (gw // ups) % HEAD_DIM == 0, (steps, db)
    rows = gw // ups
    unit = lambda bb, i: bb * n_i + i
    blk = pl.BlockSpec((None, 2, rows, length),
                       lambda bb, i: (unit(bb, i) // ups, 0, unit(bb, i) % ups, 0))
    stat = pl.BlockSpec((None, SUBLANES, rows), lambda bb, i: (unit(bb, i), 0, 0))
    stat_shape = jax.ShapeDtypeStruct((steps, SUBLANES, rows), F32)
    return ups, blk, stat, stat_shape


def _project_qkv(x3, g, w, n_cols, *, tm):
    b, s, d = x3.shape
    tn = GROUP_WIDTH
    n_i = s // tm
    extra = n_cols - 3 * N_GROUPS * tn
    w_spec = pl.BlockSpec((d, n_cols), lambda bb, i: (0, 0), pipeline_mode=pl.Buffered(1))
    grp_shape = lambda r: jax.ShapeDtypeStruct((b, r, s // r, 3 * tn), BF16)
    grp_spec = lambda r: pl.BlockSpec((None, r, tm // r, 3 * tn), lambda bb, i: (bb, 0, i, 0))
    tails = [min(w, s) for w in WINDOWS]
    tail_tiles = tuple(max(w // tm, 1) for w in tails)
    assert all(w % min(tm, w) == 0 and min(tm, w) % LANES == 0 for w in tails)
    tail_shape = lambda w: jax.ShapeDtypeStruct((b, 2, tn, w), F32)
    tail_spec = lambda w, nt: pl.BlockSpec(
        (None, 2, tn, min(tm, w)), lambda bb, i: (bb, 0, 0, jnp.maximum(i - (n_i - nt), 0)))
    return pl.pallas_call(
        functools.partial(_proj_qkv_kernel, tn=tn, tail_tiles=tail_tiles),
        out_shape=tuple(grp_shape(r) for r in DILATIONS)
        + (jax.ShapeDtypeStruct((b, s, extra), BF16),)
        + tuple(tail_shape(w) for w in tails),
        grid=(b, n_i),
        in_specs=[pl.BlockSpec((None, tm, d), lambda bb, i: (bb, i, 0)),
                  _resident((1, d)), w_spec],
        out_specs=tuple(grp_spec(r) for r in DILATIONS)
        + (pl.BlockSpec((None, tm, extra), lambda bb, i: (bb, i, 0)),)
        + tuple(tail_spec(w, nt) for w, nt in zip(tails, tail_tiles)),
        scratch_shapes=[pltpu.VMEM((d // LANES, tm, LANES), F32),
                        pltpu.VMEM((N_GROUPS, tm, d), BF16)],
        compiler_params=pltpu.CompilerParams(dimension_semantics=("arbitrary", "arbitrary"),
                                             vmem_limit_bytes=VMEM_LIMIT),
        name="proj_qkv",
    )(x3, g, w)


def _project_rest(x3, g, w, first_col, cols01, caches01, *, tm, dils):
    b, s, d = x3.shape
    tn = GROUP_WIDTH
    n = w.shape[1] - first_col
    assert first_col % tn == 0 and n % tn == 0
    n_w = n // tn
    n_i = s // tm
    in_specs = [pl.BlockSpec((None, tm, d), lambda bb, i: (bb, i, 0)), _resident((1, d))]
    in_specs += [pl.BlockSpec((d, tn), lambda bb, i, k=k: (0, first_col // tn + k),
                              pipeline_mode=pl.Buffered(1)) for k in range(n_w)]
    out_shape = [jax.ShapeDtypeStruct((b, s, n), BF16)]
    out_specs = [pl.BlockSpec((None, tm, n), lambda bb, i: (bb, i, 0))]
    args = [x3, g] + [w] * n_w
    for cols, cache_t in zip(cols01, caches01):
        ups, blk, stat, stat_shape = _cache_specs(cache_t, b * n_i, n_i)
        in_specs += [_resident(cols.shape), blk]
        args += [cols, cache_t]
        out_shape += [jax.ShapeDtypeStruct(cache_t.shape, cache_t.dtype), stat_shape, stat_shape]
        out_specs += [blk, stat, stat]
    return pl.pallas_call(
        functools.partial(_proj_rest_kernel, tn=tn, n_w=n_w, dils=dils, units_per_seq=ups),
        out_shape=tuple(out_shape),
        grid=(b, n_i),
        in_specs=in_specs,
        out_specs=tuple(out_specs),
        compiler_params=pltpu.CompilerParams(dimension_semantics=("arbitrary", "arbitrary"),
                                             vmem_limit_bytes=VMEM_LIMIT),
        name="proj_rest",
    )(*args)


def _proj_rows_kernel(x_ref, g_ref, w_ref, c0_ref, c1_ref, c2_ref, rest_ref, *, tn):
    m = x_ref.shape[0]
    h = _rms(x_ref[...], g_ref[...]).astype(BF16)
    nq = 3 * N_GROUPS * tn
    for g, out in enumerate((c0_ref, c1_ref, c2_ref)):
        for j in range(3):
            c0 = (j * N_GROUPS + g) * tn
            res = jnp.dot(h, w_ref[:, c0:c0 + tn], preferred_element_type=F32)
            res = jnp.concatenate([res, jnp.zeros((LANES - m, tn), F32)], axis=0)
            out[j * tn:(j + 1) * tn, :] = res.T
    for c0 in range(nq, w_ref.shape[-1], tn):
        res = jnp.dot(h, w_ref[:, c0:c0 + tn], preferred_element_type=F32)
        rest_ref[:, c0 - nq:c0 - nq + tn] = res


def _project_rows(x2d, g, w):
    m, d = x2d.shape
    tn = GROUP_WIDTH
    nq = 3 * N_GROUPS * tn
    assert m <= LANES and m % SUBLANES == 0
    whole = lambda a: pl.BlockSpec(a.shape, lambda i: (0,) * a.ndim)
    out_shape = tuple(jax.ShapeDtypeStruct((3 * tn, LANES), F32) for _ in range(N_GROUPS))
    out_shape += (jax.ShapeDtypeStruct((m, w.shape[1] - nq), F32),)
    return pl.pallas_call(
        functools.partial(_proj_rows_kernel, tn=tn),
        out_shape=out_shape,
        grid=(1,),
        in_specs=[whole(x2d), _resident((1, d)), _resident(w.shape)],
        out_specs=tuple(whole(o) for o in out_shape),
        compiler_params=pltpu.CompilerParams(dimension_semantics=("arbitrary",),
                                             vmem_limit_bytes=VMEM_LIMIT),
        name="proj_sample",
    )(x2d, g, w)


def _attn_kernel(q_ref, kc_ref, vc_ref, kp_ref, vp_ref, o_ref, l_ref, kcat, vcat, *, tq):
    for cls in range(q_ref.shape[0]):
        _attn_class(q_ref.at[cls], kc_ref.at[cls], vc_ref.at[cls], kp_ref.at[cls], vp_ref.at[cls],
                    o_ref.at[cls], l_ref.at[cls], kcat.at[cls], vcat.at[cls], tq)


def _attn_class(q_ref, kc_ref, vc_ref, kp_ref, vp_ref, o_ref, l_ref, kcat, vcat, tq):
    i = pl.program_id(2)
    kcat[0:BAND, :] = kp_ref[...]
    kcat[BAND:, :] = kc_ref[...]
    vcat[0:BAND, :] = vp_ref[...]
    vcat[BAND:, :] = vc_ref[...]

    qi = lax.broadcasted_iota(jnp.int32, (BAND, 2 * BAND), 0)
    ki = lax.broadcasted_iota(jnp.int32, (BAND, 2 * BAND), 1)
    rel = qi + BAND - ki
    band_ok = (rel >= 0) & (rel <= BAND)
    lane = lax.broadcasted_iota(jnp.int32, (BAND, LANES), 1)
    low = lane < HEAD_DIM

    for qb in range(tq // BAND):
        r0 = qb * BAND
        if qb == 0:
            valid = band_ok & ((ki >= BAND) | (i > 0))
        else:
            valid = band_ok
        valid2 = jnp.concatenate([valid, valid], axis=0)
        for hp in range(GROUP_WIDTH // LANES):
            c0 = hp * LANES
            q2 = q_ref[r0:r0 + BAND, c0:c0 + LANES]
            k2 = kcat[r0:r0 + 2 * BAND, c0:c0 + LANES]
            v2 = vcat[r0:r0 + 2 * BAND, c0:c0 + LANES]
            zero = jnp.zeros_like(q2)
            qm = jnp.concatenate([jnp.where(low, q2, zero), jnp.where(low, zero, q2)], axis=0)
            s = lax.dot_general(qm, k2, (((1,), (1,)), ((), ())), preferred_element_type=F32)
            s = jnp.where(valid2, s, NEG)
            m = jnp.max(s, axis=-1, keepdims=True)
            p = jnp.exp2(s - m)
            den = jnp.sum(p, axis=-1, keepdims=True)
            pv = jnp.dot(p.astype(BF16), v2, preferred_element_type=F32)
            o = pv * (1.0 / den)
            lse = m + jnp.log2(den)
            o_ref[r0:r0 + BAND, c0:c0 + LANES] = jnp.where(low, o[0:BAND], o[BAND:]).astype(o_ref.dtype)
            l_ref[r0:r0 + BAND, c0:c0 + LANES] = jnp.where(low, lse[0:BAND], lse[BAND:])


def _prompt_attention(qkv_g, g, *, tq):
    b, r, sr, _ = qkv_g.shape
    rows = tq
    tq = min(rows, sr)
    per = tq // BAND
    ncls = max(1, min(r, rows // tq))
    while r % ncls:
        ncls -= 1

    def cur(j):
        return pl.BlockSpec((None, ncls, tq, GROUP_WIDTH), lambda bb, c, i: (bb, c, i, j))

    def prev(j):
        return pl.BlockSpec((None, ncls, BAND, GROUP_WIDTH),
                            lambda bb, c, i: (bb, c, jnp.maximum(i * per - 1, 0), j))

    out_spec = pl.BlockSpec((None, ncls, tq, GROUP_WIDTH), lambda bb, c, i: (bb, c, i, 0))
    return pl.pallas_call(
        functools.partial(_attn_kernel, tq=tq),
        out_shape=(jax.ShapeDtypeStruct((b, r, sr, GROUP_WIDTH), F32),
                   jax.ShapeDtypeStruct((b, r, sr, GROUP_WIDTH), F32)),
        grid=(b, r // ncls, sr // tq),
        in_specs=[cur(0), cur(1), cur(2), prev(1), prev(2)],
        out_specs=(out_spec, out_spec),
        scratch_shapes=[pltpu.VMEM((ncls, BAND + tq, GROUP_WIDTH), BF16),
                        pltpu.VMEM((ncls, BAND + tq, GROUP_WIDTH), BF16)],
        compiler_params=pltpu.CompilerParams(
            dimension_semantics=("arbitrary", "arbitrary", "arbitrary"),
            vmem_limit_bytes=VMEM_LIMIT),
        name=f"attn_g{g}",
    )(qkv_g, qkv_g, qkv_g, qkv_g, qkv_g)


def _merge_groups(os_, ls, exp=jnp.exp):
    m = jnp.maximum(jnp.maximum(ls[0], ls[1]), ls[2])
    es = [exp(l - m) for l in ls]
    num = es[0] * os_[0] + es[1] * os_[1] + es[2] * os_[2]
    return num / (es[0] + es[1] + es[2])


def _mixer(x, attn, c_b, conv_v, g_a, g_c, wao_ref, wco_ref, wo_ref, gpost, gpre, row_parts=1):
    n = x.shape[0] // row_parts
    rows = [slice(p * n, (p + 1) * n) for p in range(row_parts)]
    dot = functools.partial(jnp.dot, preferred_element_type=F32)
    a = [dot(attn[r].astype(BF16), wao_ref[...]) for r in rows]
    c = [dot((c_b[r] * conv_v[r]).astype(BF16), wco_ref[...]) for r in rows]
    mix = [dot((jax.nn.sigmoid(g_a[r]) * a[p] + jax.nn.sigmoid(g_c[r]) * c[p]).astype(BF16),
               wo_ref[...]) for p, r in enumerate(rows)]
    x1 = [x[r] + _rms(mix[p], gpost) for p, r in enumerate(rows)]
    h2 = [_rms(v, gpre).astype(BF16) for v in x1]
    return jnp.concatenate(x1, axis=0), jnp.concatenate(h2, axis=0)


def _mlp(x1, h2, w1_ref, w2_ref, gfpost, *, ff_chunk, row_parts=1, mid_mlp=None):
    n = x1.shape[0] // row_parts
    rows = [slice(p * n, (p + 1) * n) for p in range(row_parts)]
    dot = functools.partial(jnp.dot, preferred_element_type=F32)
    f = [None] * row_parts
    for k0 in range(0, w1_ref.shape[1], ff_chunk):
        if mid_mlp is not None and k0 == ff_chunk:
            mid_mlp()
        t = [dot(h2[r], w1_ref[:, k0:k0 + ff_chunk]) for r in rows]
        t = [jnp.square(jnp.maximum(v, 0.0)).astype(BF16) for v in t]
        part = [dot(v, w2_ref[k0:k0 + ff_chunk, :]) for v in t]
        f = [q if acc is None else acc + q for acc, q in zip(f, part)]
    return jnp.concatenate([x1[r] + _rms(f[p], gfpost) for p, r in enumerate(rows)], axis=0)


def _rest_columns(ra_ref, rb_ref, b0, d):
    c_b = ra_ref[:, 0:d].astype(F32)
    u = ra_ref[:, d:2 * d].astype(F32) * rb_ref[:, b0:b0 + d].astype(F32)
    g_a = rb_ref[:, b0 + d:b0 + 2 * d].astype(F32)
    g_c = rb_ref[:, b0 + 2 * d:b0 + 3 * d].astype(F32)
    return c_b, u, g_a, g_c


def _mlp_prompt_kernel(x1_ref, h2_ref, cols_ref, cache_ref, w1_ref, w2_ref, gfpost_ref,
                       y_ref, new_ref, so_ref, sl_ref, *, ff_chunk, dil, units_per_seq):
    def window_unit():
        seq, part = _unit_index(units_per_seq)
        _cache_unit(cols_ref, cache_ref, new_ref, so_ref, sl_ref, seq, part, dil)

    y_ref[...] = _mlp(x1_ref[...], h2_ref[...], w1_ref, w2_ref, gfpost_ref[...],
                      ff_chunk=ff_chunk, row_parts=2, mid_mlp=window_unit)


def _mixer_prompt_kernel(x_ref, ra_ref, rb_ref, hcc_ref, hch_ref, o0, l0, o1, l1, o2, l2,
                         convw_ref, wao_ref, wco_ref, wo_ref, gpost_ref, gpre_ref,
                         x1_ref, h2_ref, tail_ref, u_scr, nat_scr):
    i = pl.program_id(1)
    tm, d = x_ref.shape
    vals = []
    slab = 0
    for ref in (o0, l0, o1, l1, o2, l2):
        r = ref.shape[0]
        if r == 1:
            vals.append(ref[0])
        else:
            nl = GROUP_WIDTH // LANES
            for c in range(r):
                for j in range(nl):
                    nat_scr[slab + j, pl.ds(c, tm // r, stride=r), :] = (
                        ref[c, :, j * LANES:(j + 1) * LANES])
            vals.append(jnp.concatenate([nat_scr[slab + j] for j in range(nl)], axis=1))
            slab += nl
    attn = _merge_groups(vals[0::2], vals[1::2], exp=jnp.exp2)
    c_b, u, g_a, g_c = _rest_columns(ra_ref, rb_ref, 0, d)
    hu = hcc_ref[...].astype(F32) * hch_ref[...].astype(F32)
    u_scr[0:SUBLANES, :] = jnp.where(i > 0, hu, jnp.zeros_like(hu))
    u_scr[SUBLANES:, :] = u
    w = convw_ref[...]
    conv_v = (w[0:1, :] * u_scr[SUBLANES - 2:SUBLANES - 2 + tm, :]
              + w[1:2, :] * u_scr[SUBLANES - 1:SUBLANES - 1 + tm, :]
              + w[2:3, :] * u)
    x1, h2 = _mixer(x_ref[...], attn, c_b, conv_v, g_a, g_c, wao_ref, wco_ref, wo_ref,
                    gpost_ref[...], gpre_ref[...], row_parts=2)
    x1_ref[...] = x1
    h2_ref[...] = h2
    tail_ref[...] = u[tm - SUBLANES:tm, :]


def _block_sample_kernel(x_ref, rest_ref, st_ref, o0, l0, o1, l1, o2, l2,
                         convw_ref, wao_ref, wco_ref, wo_ref, w1_ref, w2_ref,
                         gpost_ref, gpre_ref, gfpost_ref, y_ref, u_ref, *, ff_chunk):
    d = x_ref.shape[1]
    attn = _merge_groups((o0[...], o1[...], o2[...]), (l0[...], l1[...], l2[...]))
    c_b, u, g_a, g_c = _rest_columns(rest_ref, rest_ref, 2 * d, d)
    w = convw_ref[...]
    conv_v = w[0:1, :] * st_ref[:, 0:d] + w[1:2, :] * st_ref[:, d:2 * d] + w[2:3, :] * u
    x1, h2 = _mixer(x_ref[...], attn, c_b, conv_v, g_a, g_c, wao_ref, wco_ref, wo_ref,
                    gpost_ref[...], gpre_ref[...])
    y_ref[...] = _mlp(x1, h2, w1_ref, w2_ref, gfpost_ref[...], ff_chunk=ff_chunk)
    u_ref[...] = u


def _weight_specs(ws):
    return [_resident(w.shape) for w in ws]


def _mlp_prompt(x1, h2, cols, cache_t, weights, *, tm, ff_chunk, dil):
    b, s, d = x1.shape
    n_i = s // tm
    row = pl.BlockSpec((None, tm, d), lambda bb, i: (bb, i, 0))
    ups, blk, stat, stat_shape = _cache_specs(cache_t, b * n_i, n_i)
    return pl.pallas_call(
        functools.partial(_mlp_prompt_kernel, ff_chunk=ff_chunk, dil=dil, units_per_seq=ups),
        out_shape=(jax.ShapeDtypeStruct((b, s, d), F32),
                   jax.ShapeDtypeStruct(cache_t.shape, cache_t.dtype), stat_shape, stat_shape),
        grid=(b, n_i),
        in_specs=[row, row, _resident(cols.shape), blk] + _weight_specs(weights),
        out_specs=(row, blk, stat, stat),
        compiler_params=pltpu.CompilerParams(dimension_semantics=("arbitrary", "arbitrary"),
                                             vmem_limit_bytes=VMEM_LIMIT),
        name="mlp_prompt",
    )(x1, h2, cols, cache_t, *weights)


def _mixer_prompt(x3, ra3, rb3, os_, ls_, weights, *, tm):
    b, s, d = x3.shape
    assert ra3.shape[-1] == 2 * d and rb3.shape[-1] == 3 * d
    per = tm // SUBLANES
    row = lambda width: pl.BlockSpec((None, tm, width), lambda bb, i: (bb, i, 0))
    halo = lambda col: pl.BlockSpec((None, SUBLANES, d),
                                    lambda bb, i: (bb, jnp.maximum(i * per - 1, 0), col))
    in_specs = [row(d), row(2 * d), row(3 * d), halo(1), halo(0)]
    args = [x3, ra3, rb3, ra3, rb3]
    for o, l in zip(os_, ls_):
        r = o.shape[1]
        res = pl.BlockSpec((None, r, tm // r, GROUP_WIDTH), lambda bb, i: (bb, 0, i, 0))
        in_specs += [res, res]
        args += [o, l]
    in_specs += _weight_specs(weights)
    args += list(weights)
    n_slabs = 2 * sum(o.shape[1] > 1 for o in os_) * GROUP_WIDTH // LANES
    return pl.pallas_call(
        _mixer_prompt_kernel,
        out_shape=(jax.ShapeDtypeStruct((b, s, d), F32),
                   jax.ShapeDtypeStruct((b, s, d), BF16),
                   jax.ShapeDtypeStruct((b, SUBLANES, d), F32)),
        grid=(b, s // tm),
        in_specs=in_specs,
        out_specs=(row(d), row(d), pl.BlockSpec((None, SUBLANES, d), lambda bb, i: (bb, 0, 0))),
        scratch_shapes=[pltpu.VMEM((SUBLANES + tm, d), F32),
                        pltpu.VMEM((n_slabs, tm, LANES), F32)],
        compiler_params=pltpu.CompilerParams(dimension_semantics=("arbitrary", "arbitrary"),
                                             vmem_limit_bytes=VMEM_LIMIT),
        name="mixer_prompt",
    )(*args)


def _block_sample(x2, rest2, state2, os_, ls_, weights, *, ff_chunk):
    m, d = x2.shape
    full = lambda a: pl.BlockSpec(a.shape, lambda i: (0,) * a.ndim)
    args = [x2, rest2, state2]
    for o, l in zip(os_, ls_):
        args += [o, l]
    in_specs = [full(a) for a in args] + _weight_specs(weights)
    args += list(weights)
    return pl.pallas_call(
        functools.partial(_block_sample_kernel, ff_chunk=ff_chunk),
        out_shape=(jax.ShapeDtypeStruct((m, d), F32), jax.ShapeDtypeStruct((m, d), F32)),
        grid=(1,),
        in_specs=in_specs,
        out_specs=(pl.BlockSpec((m, d), lambda i: (0, 0)), pl.BlockSpec((m, d), lambda i: (0, 0))),
        compiler_params=pltpu.CompilerParams(dimension_semantics=("arbitrary",),
                                             vmem_limit_bytes=VMEM_LIMIT),
        name="block_sample",
    )(*args)


def _cache_unit(cols_ref, cache_ref, out_ref, o_ref, l_ref, seq, part, dil):
    _, nrows, length = cache_ref.shape
    scale = HEAD_DIM ** -0.5
    mine = lax.broadcasted_iota(jnp.int32, (1, LANES), 1) == seq
    base = pl.multiple_of(part * nrows, SUBLANES)

    def column(segment):
        blk = cols_ref[pl.ds(segment * GROUP_WIDTH + base, nrows), :]
        return jnp.sum(jnp.where(mine, blk, 0.0), axis=-1, keepdims=True)

    qc, knc, vnc = column(0), column(1), column(2)
    row = lax.broadcasted_iota(jnp.int32, (1, length), 1)
    dist = length - row
    on_grid = ((dist % dil) == 0) & (dist <= BAND * dil)
    o_cols, l_cols = [], []
    for h in range(nrows // HEAD_DIM):
        rows = slice(h * HEAD_DIM, (h + 1) * HEAD_DIM)
        s = jnp.sum(cache_ref[0, rows, :] * qc[rows], axis=0, keepdims=True) * scale
        s = jnp.where(on_grid, s, NEG)
        s_new = jnp.sum(qc[rows] * knc[rows], axis=0, keepdims=True) * scale
        m = jnp.maximum(jnp.max(s, axis=-1, keepdims=True), s_new)
        p = jnp.exp(s - m)
        p_new = jnp.exp(s_new - m)
        den = jnp.sum(p, axis=-1, keepdims=True) + p_new
        acc = jnp.sum(cache_ref[1, rows, :] * p, axis=-1, keepdims=True) + p_new * vnc[rows]
        o_cols.append(jnp.broadcast_to(acc / den, (HEAD_DIM, LANES)))
        l_cols.append(jnp.broadcast_to(m + jnp.log(den), (HEAD_DIM, LANES)))
    o_ref[...] = jnp.concatenate(o_cols, axis=0).T[0:SUBLANES]
    l_ref[...] = jnp.concatenate(l_cols, axis=0).T[0:SUBLANES]

    last = row == length - 1
    for kv, newc in ((0, knc), (1, vnc)):
        rolled = pltpu.roll(cache_ref[kv], length - 1, axis=1)
        out_ref[kv] = jnp.where(last, newc, rolled)


def kernel(x_prompt, x_sample, cache_kv_w128, cache_kv_w512, cache_kv_w2048, state_conv, w_in, conv_w, w_attn_o, w_conv_o, w_o, w_ff1, w_ff2, g_mix_pre, g_mix_post, g_ffn_pre, g_ffn_post):
    depth = w_in.shape[0]
    b, s, d = x_prompt.shape
    db, ds_, _ = x_sample.shape
    assert ds_ == 1 and d % LANES == 0 and db <= LANES
    caches = (cache_kv_w128, cache_kv_w512, cache_kv_w2048)
    gw = GROUP_WIDTH

    yp, ys = x_prompt, x_sample.reshape(db, d)
    kv_p = [[] for _ in range(N_GROUPS)]
    kv_s = [[] for _ in range(N_GROUPS)]
    conv_p, conv_s = [], []
    for l in range(depth):
        na = 3 * N_GROUPS * gw + 2 * d
        w_in_b = w_in[l].astype(BF16)
        weights = (conv_w[l], w_attn_o[l].astype(BF16), w_conv_o[l].astype(BF16),
                   w_o[l].astype(BF16), w_ff1[l].astype(BF16), w_ff2[l].astype(BF16),
                   g_mix_post[l][None, :], g_ffn_pre[l][None, :], g_ffn_post[l][None, :])
        g_pre = g_mix_pre[l][None, :]

        *cols, rest_s = _project_rows(ys, g_pre, w_in_b)
        caches_t = [jnp.transpose(c[l], (0, 2, 3, 4, 1)).reshape(db, 2, gw, c.shape[2])
                    for c in caches]

        q0, q1, q2, ra3, t0, t1, t2 = _project_qkv(yp, g_pre, w_in_b, na, tm=TM_PROJ)
        qkv_groups = (q0, q1, q2)
        rb3, new0, so0, sl0, new1, so1, sl1 = _project_rest(
            yp, g_pre, w_in_b, na, cols[:2], caches_t[:2], tm=TM_PROJ, dils=DILATIONS[:2])
        os_, ls_ = [], []
        for g in range(N_GROUPS):
            o, lse = _prompt_attention(qkv_groups[g], g, tq=TQ_ATTN)
            os_.append(o)
            ls_.append(lse)
            tail_t = (t0, t1, t2)[g]
            w = tail_t.shape[-1]
            kv_p[g].append(jnp.transpose(tail_t.reshape(b, 2, HEADS, HEAD_DIM, w), (0, 4, 1, 2, 3)))
        x1, h2, tail = _mixer_prompt(yp, ra3, rb3, os_, ls_,
                                     weights[0:4] + weights[6:8], tm=TM_BLOCK)
        yp, new2, so2, sl2 = _mlp_prompt(x1, h2, cols[2], caches_t[2],
                                         weights[4:6] + weights[8:9], tm=TM_BLOCK,
                                         ff_chunk=FF_CHUNK, dil=DILATIONS[2])
        conv_p.append(tail[:, SUBLANES - (CONV_WIDTH - 1):, :])
        sample_parts = ((new0, so0, sl0), (new1, so1, sl1), (new2, so2, sl2))

        os_, ls_ = [], []
        for g in range(N_GROUPS):
            new_t, so, sl = sample_parts[g]
            length = new_t.shape[-1]
            os_.append(so[:, 0, :].reshape(db, gw))
            ls_.append(sl[:, 0, :].reshape(db, gw))
            new = jnp.transpose(new_t.reshape(db, 2, HEADS, HEAD_DIM, length), (0, 4, 1, 2, 3))
            kv_s[g].append(new)
        state = state_conv[l]
        ys, u_s = _block_sample(ys, rest_s, state.reshape(db, (CONV_WIDTH - 1) * d),
                                os_, ls_, weights, ff_chunk=FF_CHUNK)
        conv_s.append(jnp.stack([state[:, 1, :], u_s], axis=1))

    stack = lambda xs: jnp.stack(xs, axis=0)
    return (yp, ys.reshape(db, 1, d),
            stack(kv_p[0]), stack(kv_p[1]), stack(kv_p[2]), stack(conv_p),
            stack(kv_s[0]), stack(kv_s[1]), stack(kv_s[2]), stack(conv_s))
```

```python
import functools

import jax
import jax.numpy as jnp
from jax import lax
from jax.experimental import pallas as pl
from jax.experimental.pallas import tpu as pltpu

HEAD_DIM = 64
HEADS = 8
GROUP_WIDTH = HEADS * HEAD_DIM
N_GROUPS = 3
WINDOWS = (128, 512, 2048)
DILATIONS = (1, 4, 16)
BAND = 128
CONV_WIDTH = 3
EPS = 1e-6
NEG = -1e30
LOG2E = 1.4426950408889634
LANES = 128
SUBLANES = 8
VMEM_LIMIT = 56 * 1024 * 1024
VMEM_LIMIT_BLOCK = 62 * 1024 * 1024

TM_PROJ = 512
TM_BLOCK = 256
TQ_ATTN = 2048
FF_CHUNK = 1024

F32 = jnp.float32
BF16 = jnp.bfloat16


def _rms(x, g):
    return (x * lax.rsqrt(jnp.mean(x * x, axis=-1, keepdims=True) + EPS)) * g


def _resident(shape):
    nd = len(shape)
    return pl.BlockSpec(shape, lambda *_: (0,) * nd, pipeline_mode=pl.Buffered(1))


def _unit_index(units_per_seq):
    u = pl.program_id(0) * pl.num_programs(1) + pl.program_id(1)
    return u // units_per_seq, u % units_per_seq


def _proj_qkv_kernel(x_ref, g_ref, w_ref,
                     q0_ref, q1_ref, q2_ref, ra_ref, t0_ref, t1_ref, t2_ref, h_scr, hb_scr,
                     *, tn, tail_tiles):
    tm = x_ref.shape[0]
    h = _rms(x_ref[...], g_ref[...])
    hb_scr[0] = h.astype(BF16)
    for j in range(h_scr.shape[0]):
        h_scr[j] = h[:, j * LANES:(j + 1) * LANES]
    for g in range(1, N_GROUPS):
        r = DILATIONS[g]
        for c in range(r):
            for j in range(h_scr.shape[0]):
                hb_scr[g, c * (tm // r):(c + 1) * (tm // r), j * LANES:(j + 1) * LANES] = (
                    h_scr[j, pl.ds(c, tm // r, stride=r), :].astype(BF16))
    for g, out in enumerate((q0_ref, q1_ref, q2_ref)):
        r = DILATIONS[g]
        for j in range(3):
            c0 = (j * N_GROUPS + g) * tn
            res = jnp.dot(hb_scr[g], w_ref[:, c0:c0 + tn], preferred_element_type=F32)
            if j == 0:
                res = res * (HEAD_DIM ** -0.5 * LOG2E)
            out[:, :, j * tn:(j + 1) * tn] = res.reshape(r, tm // r, tn).astype(out.dtype)
    nq = 3 * N_GROUPS * tn
    for c0 in range(nq, w_ref.shape[-1], tn):
        res = jnp.dot(hb_scr[0], w_ref[:, c0:c0 + tn], preferred_element_type=F32)
        ra_ref[:, c0 - nq:c0 - nq + tn] = res.astype(ra_ref.dtype)
    for g, (src, out) in enumerate(zip((q0_ref, q1_ref, q2_ref), (t0_ref, t1_ref, t2_ref))):
        lanes = out.shape[-1]
        r = DILATIONS[g]

        @pl.when(pl.program_id(1) >= pl.num_programs(1) - tail_tiles[g])
        def _():
            for j in (1, 2):
                for c in range(r):
                    for n in range(tn // LANES):
                        h_scr[n, pl.ds(c, tm // r, stride=r), :] = (
                            src[c, :, j * tn + n * LANES:j * tn + (n + 1) * LANES].astype(F32))
                res = jnp.concatenate([h_scr[n] for n in range(tn // LANES)], axis=1)
                out[j - 1] = res[tm - lanes:, :].T


def _proj_rest_kernel(x_ref, g_ref, *refs, tn, n_w, dils, units_per_seq):
    w_refs = refs[:n_w]
    cols0_ref, cache0_ref, cols1_ref, cache1_ref = refs[n_w:n_w + 4]
    rest_ref, new0_ref, so0_ref, sl0_ref, new1_ref, so1_ref, sl1_ref = refs[n_w + 4:]
    h = _rms(x_ref[...], g_ref[...]).astype(BF16)
    for k, w_ref in enumerate(w_refs):
        res = jnp.dot(h, w_ref[...], preferred_element_type=F32)
        rest_ref[:, k * tn:(k + 1) * tn] = res.astype(rest_ref.dtype)
    seq, part = _unit_index(units_per_seq)
    _cache_unit(cols0_ref, cache0_ref, new0_ref, so0_ref, sl0_ref, seq, part, dils[0])
    _cache_unit(cols1_ref, cache1_ref, new1_ref, so1_ref, sl1_ref, seq, part, dils[1])


def _cache_specs(cache_t, steps, n_i):
    db, _, gw, length = cache_t.shape
    ups = steps // db
    assert steps == db * ups and gw % ups == 0 and (gw // ups) % HEAD_DIM == 0, (steps, db)
    rows = gw // ups
    unit = lambda bb, i: bb * n_i + i
    blk = pl.BlockSpec((None, 2, rows, length),
                       lambda bb, i: (unit(bb, i) // ups, 0, unit(bb, i) % ups, 0))
    stat = pl.BlockSpec((None, SUBLANES, rows), lambda bb, i: (unit(bb, i), 0, 0))
    stat_shape = jax.ShapeDtypeStruct((steps, SUBLANES, rows), F32)
    return ups, blk, stat, stat_shape


def _project_qkv(x3, g, w, n_cols, *, tm):
    b, s, d = x3.shape
    tn = GROUP_WIDTH
    n_i = s // tm
    extra = n_cols - 3 * N_GROUPS * tn
    w_spec = pl.BlockSpec((d, n_cols), lambda bb, i: (0, 0), pipeline_mode=pl.Buffered(1))
    grp_shape = lambda r: jax.ShapeDtypeStruct((b, r, s // r, 3 * tn), BF16)
    grp_spec = lambda r: pl.BlockSpec((None, r, tm // r, 3 * tn), lambda bb, i: (bb, 0, i, 0))
    tails = [min(w, s) for w in WINDOWS]
    tail_tiles = tuple(max(w // tm, 1) for w in tails)
    assert all(w % min(tm, w) == 0 and min(tm, w) % LANES == 0 for w in tails)
    tail_shape = lambda w: jax.ShapeDtypeStruct((b, 2, tn, w), F32)
    tail_spec = lambda w, nt: pl.BlockSpec(
        (None, 2, tn, min(tm, w)), lambda bb, i: (bb, 0, 0, jnp.maximum(i - (n_i - nt), 0)))
    return pl.pallas_call(
        functools.partial(_proj_qkv_kernel, tn=tn, tail_tiles=tail_tiles),
        out_shape=tuple(grp_shape(r) for r in DILATIONS)
        + (jax.ShapeDtypeStruct((b, s, extra), BF16),)
        + tuple(tail_shape(w) for w in tails),
        grid=(b, n_i),
        in_specs=[pl.BlockSpec((None, tm, d), lambda bb, i: (bb, i, 0)),
                  _resident((1, d)), w_spec],
        out_specs=tuple(grp_spec(r) for r in DILATIONS)
        + (pl.BlockSpec((None, tm, extra), lambda bb, i: (bb, i, 0)),)
        + tuple(tail_spec(w, nt) for w, nt in zip(tails, tail_tiles)),
        scratch_shapes=[pltpu.VMEM((d // LANES, tm, LANES), F32),
                        pltpu.VMEM((N_GROUPS, tm, d), BF16)],
        compiler_params=pltpu.CompilerParams(dimension_semantics=("arbitrary", "arbitrary"),
                                             vmem_limit_bytes=VMEM_LIMIT),
        name="proj_qkv",
    )(x3, g, w)


def _project_rest(x3, g, w, first_col, cols01, caches01, *, tm, dils):
    b, s, d = x3.shape
    tn = GROUP_WIDTH
    n = w.shape[1] - first_col
    assert first_col % tn == 0 and n % tn == 0
    n_w = n // tn
    n_i = s // tm
    in_specs = [pl.BlockSpec((None, tm, d), lambda bb, i: (bb, i, 0)), _resident((1, d))]
    in_specs += [pl.BlockSpec((d, tn), lambda bb, i, k=k: (0, first_col // tn + k),
                              pipeline_mode=pl.Buffered(1)) for k in range(n_w)]
    out_shape = [jax.ShapeDtypeStruct((b, s, n), BF16)]
    out_specs = [pl.BlockSpec((None, tm, n), lambda bb, i: (bb, i, 0))]
    args = [x3, g] + [w] * n_w
    for cols, cache_t in zip(cols01, caches01):
        ups, blk, stat, stat_shape = _cache_specs(cache_t, b * n_i, n_i)
        in_specs += [_resident(cols.shape), blk]
        args += [cols, cache_t]
        out_shape += [jax.ShapeDtypeStruct(cache_t.shape, cache_t.dtype), stat_shape, stat_shape]
        out_specs += [blk, stat, stat]
    return pl.pallas_call(
        functools.partial(_proj_rest_kernel, tn=tn, n_w=n_w, dils=dils, units_per_seq=ups),
        out_shape=tuple(out_shape),
        grid=(b, n_i),
        in_specs=in_specs,
        out_specs=tuple(out_specs),
        compiler_params=pltpu.CompilerParams(dimension_semantics=("arbitrary", "arbitrary"),
                                             vmem_limit_bytes=VMEM_LIMIT),
        name="proj_rest",
    )(*args)


def _proj_rows_kernel(x_ref, g_ref, w_ref, c0_ref, c1_ref, c2_ref, rest_ref, *, tn):
    m = x_ref.shape[0]
    h = _rms(x_ref[...], g_ref[...]).astype(BF16)
    nq = 3 * N_GROUPS * tn
    for g, out in enumerate((c0_ref, c1_ref, c2_ref)):
        for j in range(3):
            c0 = (j * N_GROUPS + g) * tn
            res = jnp.dot(h, w_ref[:, c0:c0 + tn], preferred_element_type=F32)
            res = jnp.concatenate([res, jnp.zeros((LANES - m, tn), F32)], axis=0)
            out[j * tn:(j + 1) * tn, :] = res.T
    for c0 in range(nq, w_ref.shape[-1], tn):
        res = jnp.dot(h, w_ref[:, c0:c0 + tn], preferred_element_type=F32)
        rest_ref[:, c0 - nq:c0 - nq + tn] = res


def _project_rows(x2d, g, w):
    m, d = x2d.shape
    tn = GROUP_WIDTH
    nq = 3 * N_GROUPS * tn
    assert m <= LANES and m % SUBLANES == 0
    whole = lambda a: pl.BlockSpec(a.shape, lambda i: (0,) * a.ndim)
    out_shape = tuple(jax.ShapeDtypeStruct((3 * tn, LANES), F32) for _ in range(N_GROUPS))
    out_shape += (jax.ShapeDtypeStruct((m, w.shape[1] - nq), F32),)
    return pl.pallas_call(
        functools.partial(_proj_rows_kernel, tn=tn),
        out_shape=out_shape,
        grid=(1,),
        in_specs=[whole(x2d), _resident((1, d)), _resident(w.shape)],
        out_specs=tuple(whole(o) for o in out_shape),
        compiler_params=pltpu.CompilerParams(dimension_semantics=("arbitrary",),
                                             vmem_limit_bytes=VMEM_LIMIT),
        name="proj_sample",
    )(x2d, g, w)


def _attn_kernel(q_ref, kc_ref, vc_ref, kp_ref, vp_ref, o_ref, l_ref, kcat, vcat, *, tq):
    for cls in range(q_ref.shape[0]):
        _attn_class(q_ref.at[cls], kc_ref.at[cls], vc_ref.at[cls], kp_ref.at[cls], vp_ref.at[cls],
                    o_ref.at[cls], l_ref.at[cls], kcat.at[cls], vcat.at[cls], tq)


def _attn_class(q_ref, kc_ref, vc_ref, kp_ref, vp_ref, o_ref, l_ref, kcat, vcat, tq):
    i = pl.program_id(2)
    kcat[0:BAND, :] = kp_ref[...]
    kcat[BAND:, :] = kc_ref[...]
    vcat[0:BAND, :] = vp_ref[...]
    vcat[BAND:, :] = vc_ref[...]

    qi = lax.broadcasted_iota(jnp.int32, (BAND, 2 * BAND), 0)
    ki = lax.broadcasted_iota(jnp.int32, (BAND, 2 * BAND), 1)
    rel = qi + BAND - ki
    band_ok = (rel >= 0) & (rel <= BAND)
    lane = lax.broadcasted_iota(jnp.int32, (BAND, LANES), 1)
    low = lane < HEAD_DIM

    for qb in range(tq // BAND):
        r0 = qb * BAND
        if qb == 0:
            valid = band_ok & ((ki >= BAND) | (i > 0))
        else:
            valid = band_ok
        valid2 = jnp.concatenate([valid, valid], axis=0)
        for hp in range(GROUP_WIDTH // LANES):
            c0 = hp * LANES
            q2 = q_ref[r0:r0 + BAND, c0:c0 + LANES]
            k2 = kcat[r0:r0 + 2 * BAND, c0:c0 + LANES]
            v2 = vcat[r0:r0 + 2 * BAND, c0:c0 + LANES]
            zero = jnp.zeros_like(q2)
            qm = jnp.concatenate([jnp.where(low, q2, zero), jnp.where(low, zero, q2)], axis=0)
            s = lax.dot_general(qm, k2, (((1,), (1,)), ((), ())), preferred_element_type=F32)
            s = jnp.where(valid2, s, NEG)
            m = jnp.max(s, axis=-1, keepdims=True)
            p = jnp.exp2(s - m)
            den = jnp.sum(p, axis=-1, keepdims=True)
            pv = jnp.dot(p.astype(BF16), v2, preferred_element_type=F32)
            o = pv * (1.0 / den)
            lse = m + jnp.log2(den)
            o_ref[r0:r0 + BAND, c0:c0 + LANES] = jnp.where(low, o[0:BAND], o[BAND:]).astype(o_ref.dtype)
            l_ref[r0:r0 + BAND, c0:c0 + LANES] = jnp.where(low, lse[0:BAND], lse[BAND:])


def _prompt_attention(qkv_g, g, *, tq):
    b, r, sr, _ = qkv_g.shape
    rows = tq
    tq = min(rows, sr)
    per = tq // BAND
    ncls = max(1, min(r, rows // tq))
    while r % ncls:
        ncls -= 1

    def cur(j):
        return pl.BlockSpec((None, ncls, tq, GROUP_WIDTH), lambda bb, c, i: (bb, c, i, j))

    def prev(j):
        return pl.BlockSpec((None, ncls, BAND, GROUP_WIDTH),
                            lambda bb, c, i: (bb, c, jnp.maximum(i * per - 1, 0), j))

    out_spec = pl.BlockSpec((None, ncls, tq, GROUP_WIDTH), lambda bb, c, i: (bb, c, i, 0))
    return pl.pallas_call(
        functools.partial(_attn_kernel, tq=tq),
        out_shape=(jax.ShapeDtypeStruct((b, r, sr, GROUP_WIDTH), BF16),
                   jax.ShapeDtypeStruct((b, r, sr, GROUP_WIDTH), F32)),
        grid=(b, r // ncls, sr // tq),
        in_specs=[cur(0), cur(1), cur(2), prev(1), prev(2)],
        out_specs=(out_spec, out_spec),
        scratch_shapes=[pltpu.VMEM((ncls, BAND + tq, GROUP_WIDTH), BF16),
                        pltpu.VMEM((ncls, BAND + tq, GROUP_WIDTH), BF16)],
        compiler_params=pltpu.CompilerParams(
            dimension_semantics=("arbitrary", "arbitrary", "arbitrary"),
            vmem_limit_bytes=VMEM_LIMIT),
        name=f"attn_g{g}",
    )(qkv_g, qkv_g, qkv_g, qkv_g, qkv_g)


def _merge_groups(os_, ls, exp=jnp.exp):
    m = jnp.maximum(jnp.maximum(ls[0], ls[1]), ls[2])
    es = [exp(l - m) for l in ls]
    num = es[0] * os_[0] + es[1] * os_[1] + es[2] * os_[2]
    return num / (es[0] + es[1] + es[2])


def _block_core(x, attn, c_b, conv_v, g_a, g_c, wao_ref, wco_ref, wo_ref, w1_ref, w2_ref,
                gpost, gpre, gfpost, *, ff_chunk, row_parts=1, mid_mlp=None):
    n = x.shape[0] // row_parts
    rows = [slice(p * n, (p + 1) * n) for p in range(row_parts)]
    dot = functools.partial(jnp.dot, preferred_element_type=F32)
    a = [dot(attn[r].astype(BF16), wao_ref[...]) for r in rows]
    c = [dot((c_b[r] * conv_v[r]).astype(BF16), wco_ref[...]) for r in rows]
    mix = [dot((jax.nn.sigmoid(g_a[r]) * a[p] + jax.nn.sigmoid(g_c[r]) * c[p]).astype(BF16),
               wo_ref[...]) for p, r in enumerate(rows)]
    x1 = [x[r] + _rms(mix[p], gpost) for p, r in enumerate(rows)]
    h2 = [_rms(v, gpre).astype(BF16) for v in x1]
    f = [None] * row_parts
    for k0 in range(0, w1_ref.shape[1], ff_chunk):
        if mid_mlp is not None and k0 == ff_chunk:
            mid_mlp()
        t = [dot(h, w1_ref[:, k0:k0 + ff_chunk]) for h in h2]
        t = [jnp.square(jnp.maximum(v, 0.0)).astype(BF16) for v in t]
        part = [dot(v, w2_ref[k0:k0 + ff_chunk, :]) for v in t]
        f = [q if acc is None else acc + q for acc, q in zip(f, part)]
    return jnp.concatenate([x1[p] + _rms(f[p], gfpost) for p in range(row_parts)], axis=0)


def _rest_columns(ra_ref, rb_ref, b0, d):
    c_b = ra_ref[:, 0:d].astype(F32)
    u = ra_ref[:, d:2 * d].astype(F32) * rb_ref[:, b0:b0 + d].astype(F32)
    g_a = rb_ref[:, b0 + d:b0 + 2 * d].astype(F32)
    g_c = rb_ref[:, b0 + 2 * d:b0 + 3 * d].astype(F32)
    return c_b, u, g_a, g_c


def _block_prompt_kernel(x_ref, ra_ref, rb_ref, hcc_ref, hch_ref, o0, l0, o1, l1, o2, l2,
                         cols_ref, cache_ref,
                         convw_ref, wao_ref, wco_ref, wo_ref, w1_ref, w2_ref,
                         gpost_ref, gpre_ref, gfpost_ref, y_ref, tail_ref, new_ref, so_ref, sl_ref,
                         u_scr, nat_scr, *, ff_chunk, dil, units_per_seq):
    def window_unit():
        seq, part = _unit_index(units_per_seq)
        _cache_unit(cols_ref, cache_ref, new_ref, so_ref, sl_ref, seq, part, dil)

    i = pl.program_id(1)
    tm, d = x_ref.shape
    vals = []
    slab = 0
    for ref in (o0, l0, o1, l1, o2, l2):
        r = ref.shape[0]
        if r == 1:
            vals.append(ref[0].astype(F32))
        else:
            nl = GROUP_WIDTH // LANES
            for c in range(r):
                for j in range(nl):
                    nat_scr[slab + j, pl.ds(c, tm // r, stride=r), :] = (
                        ref[c, :, j * LANES:(j + 1) * LANES].astype(F32))
            vals.append(jnp.concatenate([nat_scr[slab + j] for j in range(nl)], axis=1))
            slab += nl
    attn = _merge_groups(vals[0::2], vals[1::2], exp=jnp.exp2)
    c_b, u, g_a, g_c = _rest_columns(ra_ref, rb_ref, 0, d)
    hu = hcc_ref[...].astype(F32) * hch_ref[...].astype(F32)
    u_scr[0:SUBLANES, :] = jnp.where(i > 0, hu, jnp.zeros_like(hu))
    u_scr[SUBLANES:, :] = u
    w = convw_ref[...]
    conv_v = (w[0:1, :] * u_scr[SUBLANES - 2:SUBLANES - 2 + tm, :]
              + w[1:2, :] * u_scr[SUBLANES - 1:SUBLANES - 1 + tm, :]
              + w[2:3, :] * u)
    y_ref[...] = _block_core(x_ref[...], attn, c_b, conv_v, g_a, g_c,
                             wao_ref, wco_ref, wo_ref, w1_ref, w2_ref,
                             gpost_ref[...], gpre_ref[...], gfpost_ref[...], ff_chunk=ff_chunk,
                             row_parts=2, mid_mlp=window_unit)
    tail_ref[...] = u[tm - SUBLANES:tm, :]


def _block_sample_kernel(x_ref, rest_ref, st_ref, o0, l0, o1, l1, o2, l2,
                         convw_ref, wao_ref, wco_ref, wo_ref, w1_ref, w2_ref,
                         gpost_ref, gpre_ref, gfpost_ref, y_ref, u_ref, *, ff_chunk):
    d = x_ref.shape[1]
    attn = _merge_groups((o0[...], o1[...], o2[...]), (l0[...], l1[...], l2[...]))
    c_b, u, g_a, g_c = _rest_columns(rest_ref, rest_ref, 2 * d, d)
    w = convw_ref[...]
    conv_v = w[0:1, :] * st_ref[:, 0:d] + w[1:2, :] * st_ref[:, d:2 * d] + w[2:3, :] * u
    y_ref[...] = _block_core(x_ref[...], attn, c_b, conv_v, g_a, g_c,
                             wao_ref, wco_ref, wo_ref, w1_ref, w2_ref,
                             gpost_ref[...], gpre_ref[...], gfpost_ref[...], ff_chunk=ff_chunk)
    u_ref[...] = u


def _weight_specs(ws):
    return [_resident(w.shape) for w in ws]


def _block_prompt(x3, ra3, rb3, os_, ls_, cols, cache_t, weights, *, tm, ff_chunk, dil):
    b, s, d = x3.shape
    assert ra3.shape[-1] == 2 * d and rb3.shape[-1] == 3 * d
    per = tm // SUBLANES
    row = lambda width: pl.BlockSpec((None, tm, width), lambda bb, i: (bb, i, 0))
    halo = lambda col: pl.BlockSpec((None, SUBLANES, d),
                                    lambda bb, i: (bb, jnp.maximum(i * per - 1, 0), col))
    in_specs = [row(d), row(2 * d), row(3 * d), halo(1), halo(0)]
    args = [x3, ra3, rb3, ra3, rb3]
    for o, l in zip(os_, ls_):
        r = o.shape[1]
        res = pl.BlockSpec((None, r, tm // r, GROUP_WIDTH), lambda bb, i: (bb, 0, i, 0))
        in_specs += [res, res]
        args += [o, l]
    n_i = s // tm
    ups, blk, stat, stat_shape = _cache_specs(cache_t, b * n_i, n_i)
    in_specs += [_resident(cols.shape), blk]
    args += [cols, cache_t]
    in_specs += _weight_specs(weights)
    args += list(weights)
    n_slabs = 2 * sum(o.shape[1] > 1 for o in os_) * GROUP_WIDTH // LANES
    return pl.pallas_call(
        functools.partial(_block_prompt_kernel, ff_chunk=ff_chunk, dil=dil, units_per_seq=ups),
        out_shape=(jax.ShapeDtypeStruct((b, s, d), F32),
                   jax.ShapeDtypeStruct((b, SUBLANES, d), F32),
                   jax.ShapeDtypeStruct(cache_t.shape, cache_t.dtype), stat_shape, stat_shape),
        grid=(b, n_i),
        in_specs=in_specs,
        out_specs=(row(d), pl.BlockSpec((None, SUBLANES, d), lambda bb, i: (bb, 0, 0)),
                   blk, stat, stat),
        scratch_shapes=[pltpu.VMEM((SUBLANES + tm, d), F32),
                        pltpu.VMEM((n_slabs, tm, LANES), F32)],
        compiler_params=pltpu.CompilerParams(dimension_semantics=("arbitrary", "arbitrary"),
                                             vmem_limit_bytes=VMEM_LIMIT_BLOCK),
        name="block_prompt",
    )(*args)


def _block_sample(x2, rest2, state2, os_, ls_, weights, *, ff_chunk):
    m, d = x2.shape
    full = lambda a: pl.BlockSpec(a.shape, lambda i: (0,) * a.ndim)
    args = [x2, rest2, state2]
    for o, l in zip(os_, ls_):
        args += [o, l]
    in_specs = [full(a) for a in args] + _weight_specs(weights)
    args += list(weights)
    return pl.pallas_call(
        functools.partial(_block_sample_kernel, ff_chunk=ff_chunk),
        out_shape=(jax.ShapeDtypeStruct((m, d), F32), jax.ShapeDtypeStruct((m, d), F32)),
        grid=(1,),
        in_specs=in_specs,
        out_specs=(pl.BlockSpec((m, d), lambda i: (0, 0)), pl.BlockSpec((m, d), lambda i: (0, 0))),
        compiler_params=pltpu.CompilerParams(dimension_semantics=("arbitrary",),
                                             vmem_limit_bytes=VMEM_LIMIT),
        name="block_sample",
    )(*args)


def _cache_unit(cols_ref, cache_ref, out_ref, o_ref, l_ref, seq, part, dil):
    _, nrows, length = cache_ref.shape
    scale = HEAD_DIM ** -0.5
    mine = lax.broadcasted_iota(jnp.int32, (1, LANES), 1) == seq
    base = pl.multiple_of(part * nrows, SUBLANES)

    def column(segment):
        blk = cols_ref[pl.ds(segment * GROUP_WIDTH + base, nrows), :]
        return jnp.sum(jnp.where(mine, blk, 0.0), axis=-1, keepdims=True)

    qc, knc, vnc = column(0), column(1), column(2)
    row = lax.broadcasted_iota(jnp.int32, (1, length), 1)
    dist = length - row
    on_grid = ((dist % dil) == 0) & (dist <= BAND * dil)
    o_cols, l_cols = [], []
    for h in range(nrows // HEAD_DIM):
        rows = slice(h * HEAD_DIM, (h + 1) * HEAD_DIM)
        s = jnp.sum(cache_ref[0, rows, :] * qc[rows], axis=0, keepdims=True) * scale
        s = jnp.where(on_grid, s, NEG)
        s_new = jnp.sum(qc[rows] * knc[rows], axis=0, keepdims=True) * scale
        m = jnp.maximum(jnp.max(s, axis=-1, keepdims=True), s_new)
        p = jnp.exp(s - m)
        p_new = jnp.exp(s_new - m)
        den = jnp.sum(p, axis=-1, keepdims=True) + p_new
        acc = jnp.sum(cache_ref[1, rows, :] * p, axis=-1, keepdims=True) + p_new * vnc[rows]
        o_cols.append(jnp.broadcast_to(acc / den, (HEAD_DIM, LANES)))
        l_cols.append(jnp.broadcast_to(m + jnp.log(den), (HEAD_DIM, LANES)))
    o_ref[...] = jnp.concatenate(o_cols, axis=0).T[0:SUBLANES]
    l_ref[...] = jnp.concatenate(l_cols, axis=0).T[0:SUBLANES]

    last = row == length - 1
    for kv, newc in ((0, knc), (1, vnc)):
        rolled = pltpu.roll(cache_ref[kv], length - 1, axis=1)
        out_ref[kv] = jnp.where(last, newc, rolled)


def kernel(x_prompt, x_sample, cache_kv_w128, cache_kv_w512, cache_kv_w2048, state_conv, w_in, conv_w, w_attn_o, w_conv_o, w_o, w_ff1, w_ff2, g_mix_pre, g_mix_post, g_ffn_pre, g_ffn_post):
    depth = w_in.shape[0]
    b, s, d = x_prompt.shape
    db, ds_, _ = x_sample.shape
    assert ds_ == 1 and d % LANES == 0 and db <= LANES
    caches = (cache_kv_w128, cache_kv_w512, cache_kv_w2048)
    gw = GROUP_WIDTH

    yp, ys = x_prompt, x_sample.reshape(db, d)
    kv_p = [[] for _ in range(N_GROUPS)]
    kv_s = [[] for _ in range(N_GROUPS)]
    conv_p, conv_s = [], []
    for l in range(depth):
        na = 3 * N_GROUPS * gw + 2 * d
        w_in_b = w_in[l].astype(BF16)
        weights = (conv_w[l], w_attn_o[l].astype(BF16), w_conv_o[l].astype(BF16),
                   w_o[l].astype(BF16), w_ff1[l].astype(BF16), w_ff2[l].astype(BF16),
                   g_mix_post[l][None, :], g_ffn_pre[l][None, :], g_ffn_post[l][None, :])
        g_pre = g_mix_pre[l][None, :]

        *cols, rest_s = _project_rows(ys, g_pre, w_in_b)
        caches_t = [jnp.transpose(c[l], (0, 2, 3, 4, 1)).reshape(db, 2, gw, c.shape[2])
                    for c in caches]

        q0, q1, q2, ra3, t0, t1, t2 = _project_qkv(yp, g_pre, w_in_b, na, tm=TM_PROJ)
        qkv_groups = (q0, q1, q2)
        rb3, new0, so0, sl0, new1, so1, sl1 = _project_rest(
            yp, g_pre, w_in_b, na, cols[:2], caches_t[:2], tm=TM_PROJ, dils=DILATIONS[:2])
        os_, ls_ = [], []
        for g in range(N_GROUPS):
            o, lse = _prompt_attention(qkv_groups[g], g, tq=TQ_ATTN)
            os_.append(o)
            ls_.append(lse)
            tail_t = (t0, t1, t2)[g]
            w = tail_t.shape[-1]
            kv_p[g].append(jnp.transpose(tail_t.reshape(b, 2, HEADS, HEAD_DIM, w), (0, 4, 1, 2, 3)))
        yp, tail, new2, so2, sl2 = _block_prompt(yp, ra3, rb3, os_, ls_, cols[2], caches_t[2],
                                                 weights, tm=TM_BLOCK, ff_chunk=FF_CHUNK,
                                                 dil=DILATIONS[2])
        conv_p.append(tail[:, SUBLANES - (CONV_WIDTH - 1):, :])
        sample_parts = ((new0, so0, sl0), (new1, so1, sl1), (new2, so2, sl2))

        os_, ls_ = [], []
        for g in range(N_GROUPS):
            new_t, so, sl = sample_parts[g]
            length = new_t.shape[-1]
            os_.append(so[:, 0, :].reshape(db, gw))
            ls_.append(sl[:, 0, :].reshape(db, gw))
            new = jnp.transpose(new_t.reshape(db, 2, HEADS, HEAD_DIM, length), (0, 4, 1, 2, 3))
            kv_s[g].append(new)
        state = state_conv[l]
        ys, u_s = _block_sample(ys, rest_s, state.reshape(db, (CONV_WIDTH - 1) * d),
                                os_, ls_, weights, ff_chunk=FF_CHUNK)
        conv_s.append(jnp.stack([state[:, 1, :], u_s], axis=1))

    stack = lambda xs: jnp.stack(xs, axis=0)
    return (yp, ys.reshape(db, 1, d),
            stack(kv_p[0]), stack(kv_p[1]), stack(kv_p[2]), stack(conv_p),
            stack(kv_s[0]), stack(kv_s[1]), stack(kv_s[2]), stack(conv_s))
```
